```python
import math
import jax, jax.numpy as jnp
from jax import lax
import numpy as np

D_MODEL = 1024
BATCH = 1
SEQ = 16384
DEPTH = 1
DEC_BATCH = 128
DEC_SEQ = 1
PAST_LEN = 8192
PAGE_SIZE = 128

HEAD_DIM = 128
HEADS_PER_GROUP = 4
GROUPS = ((128, 1), (512, 4), (2048, 16))
N_GROUPS = len(GROUPS)
N_HEADS = N_GROUPS * HEADS_PER_GROUP
ATTN_QKV = N_HEADS * HEAD_DIM
ATTN_OUT = HEADS_PER_GROUP * HEAD_DIM
CONV_CH = D_MODEL // 2
CONV_WIDTH = 31
ALPHA = (2.0 * DEPTH) ** 0.25
BETA = (8.0 * DEPTH) ** -0.25
LN_EPS = 1e-5
NEG = -1e30
IN_SIZES = (ATTN_QKV, ATTN_QKV, ATTN_QKV, ATTN_OUT, 2 * CONV_CH, CONV_CH, D_MODEL, D_MODEL)
IN_COLS = sum(IN_SIZES)
SPLIT_POINTS = tuple(int(s) for s in np.cumsum(IN_SIZES)[:-1])

kernel_name = "hybrid_dilated_swa_conformer_conv_decode_step"


def _alibi_slopes():
    h = jnp.arange(1, N_HEADS + 1, dtype=jnp.float32)
    return (2.0 ** (-8.0 * h / N_HEADS)).reshape(N_GROUPS, HEADS_PER_GROUP)


def _layer_norm(x, g, b):
    xf = x.astype(jnp.float32)
    mu = jnp.mean(xf, axis=-1, keepdims=True)
    var = jnp.mean(jnp.square(xf - mu), axis=-1, keepdims=True)
    return ((xf - mu) * lax.rsqrt(var + LN_EPS) * g.astype(jnp.float32) + b.astype(jnp.float32)).astype(x.dtype)


def _branches_in(x, c, w_c, b_c, w_in, b_in):
    mod = c @ w_c + b_c
    shift, scale, gate = jnp.split(mod, 3, axis=-1)
    h = x * (1.0 + scale[:, None]) + shift[:, None]
    z = h @ w_in + b_in
    return gate, jnp.split(z, SPLIT_POINTS, axis=-1)


def _heads(a):
    return a.reshape(a.shape[0], a.shape[1], N_GROUPS, HEADS_PER_GROUP, HEAD_DIM)


def _dilated_window_prompt(q, k, v, slopes, window, dilation):
    B, L, H, Dh = q.shape
    n_keys = window // dilation
    blk = n_keys
    span = dilation * blk
    Lp = -(-L // span) * span
    nb = Lp // span
    ls = Lp // dilation

    def split(a):
        a = jnp.pad(a, ((0, 0), (0, Lp - L), (0, 0), (0, 0)))
        a = a.reshape(B, ls, dilation, H, Dh).transpose(0, 2, 1, 3, 4)
        return a.reshape(B, dilation, nb, blk, H, Dh)

    def with_prev(a):
        prev = jnp.pad(a, ((0, 0), (0, 0), (1, 0), (0, 0), (0, 0), (0, 0)))[:, :, :-1]
        return jnp.concatenate([prev, a], axis=3)

    qb = split(q)
    kk = with_prev(split(k))
    vv = with_prev(split(v))
    s = jnp.einsum('bdnqhe,bdnkhe->bdnhqk', qb, kk).astype(jnp.float32)
    qi = jnp.arange(blk)
    ki = jnp.arange(2 * blk) - blk
    dist = qi[:, None] - ki[None, :]
    in_range = (dist >= 0) & (dist <= n_keys)
    after_start = (jnp.arange(nb)[:, None, None] * blk + ki[None, None, :]) >= 0
    mask = in_range[None] & after_start
    bias = -(slopes[:, None, None] * (dilation * dist).astype(jnp.float32)[None])
    s = jnp.where(mask[None, None, :, None], s + bias, NEG)
    lse = jax.nn.logsumexp(s, axis=-1)
    p = jnp.exp(s - lse[..., None])
    o = jnp.einsum('bdnhqk,bdnkhe->bdnqhe', p, vv.astype(jnp.float32))
    o = o.reshape(B, dilation, ls, H, Dh).transpose(0, 2, 1, 3, 4).reshape(B, Lp, H, Dh)[:, :L]
    lse = lse.transpose(0, 1, 2, 4, 3).reshape(B, dilation, ls, H).transpose(0, 2, 1, 3).reshape(B, Lp, H)[:, :L]
    return o, lse


def _dilated_window_sample(q, k_new, v_new, kv_cache, slopes, window, dilation):
    S = q.shape[1]
    Wb = kv_cache.shape[1]
    n_keys = window // dilation
    k_all = jnp.concatenate([kv_cache[:, :, 0], k_new.astype(kv_cache.dtype)], axis=1)
    v_all = jnp.concatenate([kv_cache[:, :, 1], v_new.astype(kv_cache.dtype)], axis=1)
    steps = jnp.arange(n_keys + 1)
    idx = Wb + jnp.arange(S)[:, None] - dilation * steps[None, :]
    valid = idx >= 0
    idx_c = jnp.maximum(idx, 0)
    kg = k_all[:, idx_c]
    vg = v_all[:, idx_c]
    s = jnp.einsum('bshe,bskhe->bhsk', q, kg.astype(q.dtype)).astype(jnp.float32)
    bias = -(slopes[:, None, None] * (dilation * steps).astype(jnp.float32)[None, None, :])
    s = jnp.where(valid[None, None], s + bias, NEG)
    lse = jax.nn.logsumexp(s, axis=-1)
    p = jnp.exp(s - lse[..., None])
    o = jnp.einsum('bhsk,bskhe->bshe', p, vg.astype(jnp.float32))
    return o, lse.transpose(0, 2, 1)


def _combine_groups(outs, lses):
    w = jax.nn.softmax(jnp.stack(lses, axis=0), axis=0)
    return jnp.sum(w[..., None] * jnp.stack(outs, axis=0), axis=0)


def _glu(u2):
    a, g = jnp.split(u2, 2, axis=-1)
    return a * jax.nn.sigmoid(g)


def _conv_tail(u_ext, conv_w, conv_b, cn_g, cn_b):
    y = lax.conv_general_dilated(u_ext, conv_w[:, None, :].astype(u_ext.dtype), window_strides=(1,),
                                 padding='VALID', dimension_numbers=('NWC', 'WIO', 'NWC'),
                                 feature_group_count=CONV_CH)
    y = y + conv_b
    return jax.nn.silu(_layer_norm(y, cn_g, cn_b))


def _branches_out(x, gate, o_attn, ga, conv_out, gb, ma, mb, w_pa, w_pb, w_o, ln_g, ln_b):
    B, T = x.shape[0], x.shape[1]
    a = (o_attn.reshape(B, T, ATTN_OUT).astype(x.dtype) * jax.nn.silu(ga)) @ w_pa
    b = (conv_out * jax.nn.silu(gb)) @ w_pb
    y = (jax.nn.sigmoid(ma) * a + jax.nn.sigmoid(mb) * b) @ w_o
    return _layer_norm(ALPHA * x + gate[:, None] * y, ln_g, ln_b)


def setup_inputs(seed: int = 0) -> dict:
    key = jax.random.key(seed)
    ks = jax.random.split(key, 24)
    f32 = jnp.float32

    def nrm(k, shape, s):
        return jax.random.normal(k, shape, f32) * s

    w_bufs = [min(w, PAST_LEN) for w, _ in GROUPS]
    w_in = nrm(ks[8], (D_MODEL, IN_COLS), D_MODEL ** -0.5)
    w_in = w_in.at[:, 2 * ATTN_QKV:3 * ATTN_QKV].multiply(BETA)
    return {
        "x_prompt": nrm(ks[0], (BATCH, SEQ, D_MODEL), 1.0),
        "x_sample": nrm(ks[1], (DEC_BATCH, DEC_SEQ, D_MODEL), 1.0),
        "c_prompt": nrm(ks[2], (BATCH, D_MODEL), 1.0),
        "c_sample": nrm(ks[3], (DEC_BATCH, D_MODEL), 1.0),
        "cache_kv_w128": nrm(ks[4], (DEC_BATCH, w_bufs[0], 2, HEADS_PER_GROUP, HEAD_DIM), 1.0),
        "cache_kv_w512": nrm(ks[5], (DEC_BATCH, w_bufs[1], 2, HEADS_PER_GROUP, HEAD_DIM), 1.0),
        "cache_kv_w2048": nrm(ks[6], (DEC_BATCH, w_bufs[2], 2, HEADS_PER_GROUP, HEAD_DIM), 1.0),
        "state_conv": nrm(ks[7], (DEC_BATCH, CONV_WIDTH - 1, CONV_CH), 0.5),
        "w_c": nrm(ks[9], (D_MODEL, 3 * D_MODEL), 0.5 * D_MODEL ** -0.5),
        "b_c": nrm(ks[10], (3 * D_MODEL,), 0.02),
        "w_in": w_in,
        "b_in": nrm(ks[11], (IN_COLS,), 0.02),
        "conv_w": nrm(ks[12], (CONV_WIDTH, CONV_CH), CONV_WIDTH ** -0.5),
        "conv_b": nrm(ks[13], (CONV_CH,), 0.02),
        "conv_norm_g": 1.0 + nrm(ks[14], (CONV_CH,), 0.02),
        "conv_norm_b": nrm(ks[15], (CONV_CH,), 0.02),
        "w_pa": nrm(ks[16], (ATTN_OUT, D_MODEL), BETA * ATTN_OUT ** -0.5),
        "w_pb": nrm(ks[17], (CONV_CH, D_MODEL), BETA * CONV_CH ** -0.5),
        "w_o": nrm(ks[18], (D_MODEL, D_MODEL), BETA * D_MODEL ** -0.5),
        "ln_g": 1.0 + nrm(ks[19], (D_MODEL,), 0.02),
        "ln_b": nrm(ks[20], (D_MODEL,), 0.02),
    }


def reference(x_prompt, x_sample, c_prompt, c_sample, cache_kv_w128, cache_kv_w512, cache_kv_w2048,
              state_conv, w_c, b_c, w_in, b_in, conv_w, conv_b, conv_norm_g, conv_norm_b,
              w_pa, w_pb, w_o, ln_g, ln_b):
    slopes = _alibi_slopes()
    caches = (cache_kv_w128, cache_kv_w512, cache_kv_w2048)
    seq = x_prompt.shape[1]

    x_p, x_s = x_prompt, x_sample
    for _layer in range(DEPTH):
        gate_p, (q, k, v, ga, glu, gb, ma, mb) = _branches_in(x_p, c_prompt, w_c, b_c, w_in, b_in)
        q, k, v = _heads(q) * (HEAD_DIM ** -0.5), _heads(k), _heads(v)
        outs, lses, kv_p = [], [], []
        for g, (window, dilation) in enumerate(GROUPS):
            o, l = _dilated_window_prompt(q[:, :, g], k[:, :, g], v[:, :, g], slopes[g], window, dilation)
            outs.append(o)
            lses.append(l)
            keep = min(window, seq)
            kv_p.append(jnp.stack([k[:, seq - keep:, g], v[:, seq - keep:, g]], axis=2))
        o_attn = _combine_groups(outs, lses)
        u = _glu(glu)
        u_ext = jnp.pad(u, ((0, 0), (CONV_WIDTH - 1, 0), (0, 0)))
        conv_out = _conv_tail(u_ext, conv_w, conv_b, conv_norm_g, conv_norm_b)
        conv_p = u[:, seq - (CONV_WIDTH - 1):]
        x_p = _branches_out(x_p, gate_p, o_attn, ga, conv_out, gb, ma, mb, w_pa, w_pb, w_o, ln_g, ln_b)

        gate_s, (q, k, v, ga, glu, gb, ma, mb) = _branches_in(x_s, c_sample, w_c, b_c, w_in, b_in)
        q, k, v = _heads(q) * (HEAD_DIM ** -0.5), _heads(k), _heads(v)
        outs, lses, kv_s = [], [], []
        for g, (window, dilation) in enumerate(GROUPS):
            o, l = _dilated_window_sample(q[:, :, g], k[:, :, g], v[:, :, g], caches[g], slopes[g], window, dilation)
            outs.append(o)
            lses.append(l)
            kv_s.append(jnp.stack([k[:, :, g], v[:, :, g]], axis=2))
        o_attn = _combine_groups(outs, lses)
        u = _glu(glu)
        u_ext = jnp.concatenate([state_conv.astype(u.dtype), u], axis=1)
        conv_out = _conv_tail(u_ext, conv_w, conv_b, conv_norm_g, conv_norm_b)
        conv_s = u_ext[:, u_ext.shape[1] - (CONV_WIDTH - 1):]
        x_s = _branches_out(x_s, gate_s, o_attn, ga, conv_out, gb, ma, mb, w_pa, w_pb, w_o, ln_g, ln_b)

    return (x_p, x_s, kv_p[0], kv_p[1], kv_p[2], conv_p, kv_s[0], kv_s[1], kv_s[2], conv_s)
```

```python
import functools

import jax
import jax.numpy as jnp
from jax import lax
from jax.experimental import pallas as pl
from jax.experimental.pallas import tpu as pltpu

F32 = jnp.float32
BF16 = jnp.bfloat16

D_MODEL = 1024
SEQ = 16384
DEC_BATCH = 128
DEPTH = 1
HEAD_DIM = 128
HEADS_PER_GROUP = 4
GROUPS = ((128, 1), (512, 4), (2048, 16))
N_GROUPS = len(GROUPS)
N_HEADS = N_GROUPS * HEADS_PER_GROUP
GROUP_COLS = HEADS_PER_GROUP * HEAD_DIM
ATTN_QKV = N_HEADS * HEAD_DIM
QKV_COLS = 3 * ATTN_QKV
CONV_CH = D_MODEL // 2
CONV_WIDTH = 31
REST_COLS = GROUP_COLS + 2 * CONV_CH + CONV_CH + 2 * D_MODEL
ALPHA = (2.0 * DEPTH) ** 0.25
LN_EPS = 1e-5
NEG = -1e30
Q_SCALE = HEAD_DIM ** -0.5
KEYS_PER_BLOCK = 128
KV_TILE_ROWS = 2 * HEADS_PER_GROUP

LANES = 128
ROW_TILE = 512
ATTN_STEP_ROWS = 2048
BLOCKS_PER_ATTN_STEP = ATTN_STEP_ROWS // KEYS_PER_BLOCK
HIST_ROWS = 32
SAMPLE_BLOCK = 8
VMEM_LIMIT_BYTES = 56 * 1024 * 1024


def _slope(group, head):
    return 2.0 ** (-8.0 * (group * HEADS_PER_GROUP + head + 1) / N_HEADS)


def _sigmoid(x):
    return 1.0 / (1.0 + jnp.exp(-x))


def _silu(x):
    return x * _sigmoid(x)


def _layer_norm(x, g, b):
    mu = jnp.mean(x, axis=-1, keepdims=True)
    xc = x - mu
    var = jnp.mean(xc * xc, axis=-1, keepdims=True)
    return xc * lax.rsqrt(var + LN_EPS) * g + b


def _resident(shape):
    return pl.BlockSpec(shape, lambda *_: (0,) * len(shape), pipeline_mode=pl.Buffered(1))


def _params(semantics):
    return pltpu.CompilerParams(dimension_semantics=semantics, vmem_limit_bytes=VMEM_LIMIT_BYTES)


def _store_kv_tiles(ref, k, v):
    rows = k.shape[0]
    for hh in range(HEADS_PER_GROUP):
        cols = slice(hh * HEAD_DIM, (hh + 1) * HEAD_DIM)
        ref[pl.ds(hh, rows, stride=KV_TILE_ROWS), :] = k[:, cols]
        ref[pl.ds(HEADS_PER_GROUP + hh, rows, stride=KV_TILE_ROWS), :] = v[:, cols]


def _mod_kernel(cp_ref, cs_ref, w_ref, b_ref, mp_ref, ms_ref):
    w = w_ref[...].astype(BF16)
    cp = jnp.broadcast_to(cp_ref[...], (8, D_MODEL)).astype(BF16)
    mp_ref[...] = jnp.dot(cp, w, preferred_element_type=F32)[0:1] + b_ref[...]
    ms_ref[...] = jnp.dot(cs_ref[...].astype(BF16), w, preferred_element_type=F32) + b_ref[...]


def _mod_call(c_prompt, c_sample, w_c, b_c):
    n_blk = 3
    return pl.pallas_call(
        _mod_kernel,
        grid=(n_blk,),
        in_specs=[
            pl.BlockSpec((1, D_MODEL), lambda n: (0, 0)),
            pl.BlockSpec((DEC_BATCH, D_MODEL), lambda n: (0, 0)),
            pl.BlockSpec((D_MODEL, D_MODEL), lambda n: (0, n)),
            pl.BlockSpec((1, D_MODEL), lambda n: (0, n)),
        ],
        out_specs=[
            pl.BlockSpec((1, D_MODEL), lambda n: (0, n)),
            pl.BlockSpec((DEC_BATCH, D_MODEL), lambda n: (0, n)),
        ],
        out_shape=[
            jax.ShapeDtypeStruct((1, 3 * D_MODEL), F32),
            jax.ShapeDtypeStruct((DEC_BATCH, 3 * D_MODEL), F32),
        ],
        compiler_params=_params(("arbitrary",)),
        name="adaln_mod",
    )(c_prompt, c_sample, w_c, b_c.reshape(1, 3 * D_MODEL))


N_ROW_TILES = SEQ // ROW_TILE
KV2_TILES = GROUPS[2][0] // ROW_TILE


def _qkv_kernel(x_ref, mod_ref, w_ref, b_ref,
                q0_ref, k0_ref, v0_ref, q1_ref, k1_ref, v1_ref, q2_ref, k2_ref, v2_ref,
                kv0_ref, kv1_ref, kv2_ref,
                hs_ref, hp4_ref, hp16_ref):
    i = pl.program_id(0)
    shift = mod_ref[:, 0:D_MODEL]
    scale = mod_ref[:, D_MODEL:2 * D_MODEL]
    h = x_ref[...] * (1.0 + scale) + shift
    hb = h.astype(BF16)

    n_slab = D_MODEL // LANES
    for c in range(n_slab):
        hs_ref[c] = h[:, c * LANES:(c + 1) * LANES]
    d1, d2 = GROUPS[1][1], GROUPS[2][1]
    r1, r2 = ROW_TILE // d1, ROW_TILE // d2
    for c in range(n_slab):
        cols = slice(c * LANES, (c + 1) * LANES)
        for r in range(d1):
            hp4_ref[r * r1:(r + 1) * r1, cols] = hs_ref[c, pl.ds(r, r1, stride=d1), :].astype(BF16)
        for r in range(d2):
            hp16_ref[r * r2:(r + 1) * r2, cols] = hs_ref[c, pl.ds(r, r2, stride=d2), :].astype(BF16)

    def proj(lhs, blk):
        cols = slice(blk * GROUP_COLS, (blk + 1) * GROUP_COLS)
        return jnp.dot(lhs, w_ref[:, cols], preferred_element_type=F32) + b_ref[:, cols]

    q0_ref[...] = (proj(hb, 0) * Q_SCALE).astype(BF16)
    k0 = proj(hb, 3)
    v0 = proj(hb, 6)
    k0_ref[...] = k0.astype(BF16)
    v0_ref[...] = v0.astype(BF16)
    keep0 = GROUPS[0][0]
    _store_kv_tiles(kv0_ref, k0[ROW_TILE - keep0:, :], v0[ROW_TILE - keep0:, :])

    h4 = hp4_ref[...]
    q1_ref[...] = (proj(h4, 1) * Q_SCALE).astype(BF16)
    k1_ref[...] = proj(h4, 4).astype(BF16)
    v1_ref[...] = proj(h4, 7).astype(BF16)

    h16 = hp16_ref[...]
    for ref, val in ((q2_ref, proj(h16, 2) * Q_SCALE), (k2_ref, proj(h16, 5)), (v2_ref, proj(h16, 8))):
        val = val.astype(BF16)
        for r in range(d2):
            ref[0, r] = val[r * r2:(r + 1) * r2, :]

    @pl.when(i == N_ROW_TILES - 1)
    def _():
        _store_kv_tiles(kv1_ref, proj(hb, 4), proj(hb, 7))

    @pl.when(i >= N_ROW_TILES - KV2_TILES)
    def _():
        _store_kv_tiles(kv2_ref, proj(hb, 5), proj(hb, 8))


def _qkv_call(x, mod_p, w_qkv, b_qkv):
    d2 = GROUPS[2][1]
    spans2 = SEQ // GROUPS[2][0]
    tiles_per_span2 = GROUPS[2][0] // ROW_TILE
    nat = pl.BlockSpec((ROW_TILE, GROUP_COLS), lambda i: (i, 0))
    perm16 = pl.BlockSpec((1, d2, ROW_TILE // d2, GROUP_COLS),
                          lambda i: (i // tiles_per_span2, 0, i % tiles_per_span2, 0))
    nat_shape = jax.ShapeDtypeStruct((SEQ, GROUP_COLS), BF16)
    p16_shape = jax.ShapeDtypeStruct((spans2, d2, KEYS_PER_BLOCK, GROUP_COLS), BF16)
    first_kv2_tile = N_ROW_TILES - KV2_TILES
    kv_shape = lambda g: jax.ShapeDtypeStruct((GROUPS[g][0] * KV_TILE_ROWS, HEAD_DIM), F32)
    return pl.pallas_call(
        _qkv_kernel,
        grid=(N_ROW_TILES,),
        in_specs=[
            pl.BlockSpec((ROW_TILE, D_MODEL), lambda i: (i, 0)),
            pl.BlockSpec((1, 3 * D_MODEL), lambda i: (0, 0)),
            _resident((D_MODEL, QKV_COLS)),
            pl.BlockSpec((1, QKV_COLS), lambda i: (0, 0)),
        ],
        out_specs=[nat, nat, nat, nat, nat, nat, perm16, perm16, perm16,
                   pl.BlockSpec((GROUPS[0][0] * KV_TILE_ROWS, HEAD_DIM), lambda i: (0, 0)),
                   pl.BlockSpec((GROUPS[1][0] * KV_TILE_ROWS, HEAD_DIM), lambda i: (0, 0)),
                   pl.BlockSpec((ROW_TILE * KV_TILE_ROWS, HEAD_DIM),
                                lambda i: (jnp.maximum(i - first_kv2_tile, 0), 0))],
        out_shape=[nat_shape] * 6 + [p16_shape] * 3 + [kv_shape(0), kv_shape(1), kv_shape(2)],
        scratch_shapes=[
            pltpu.VMEM((D_MODEL // LANES, ROW_TILE, LANES), F32),
            pltpu.VMEM((ROW_TILE, D_MODEL), BF16),
            pltpu.VMEM((ROW_TILE, D_MODEL), BF16),
        ],
        compiler_params=_params(("arbitrary",)),
        name="prompt_qkv",
    )(x, mod_p, w_qkv, b_qkv)


def _attn_kernel(q_ref, k_ref, v_ref, kp_ref, vp_ref, o_ref, lse_ref, kcat_ref, vcat_ref, bias_ref, *, group):
    dil = GROUPS[group][1]
    tail = KEYS_PER_BLOCK * dil
    t = pl.program_id(0)

    @pl.when(t == 0)
    def _():
        row = lax.broadcasted_iota(jnp.int32, (KEYS_PER_BLOCK, 2 * KEYS_PER_BLOCK), 0)
        col = lax.broadcasted_iota(jnp.int32, (KEYS_PER_BLOCK, 2 * KEYS_PER_BLOCK), 1)
        dist = row - col + KEYS_PER_BLOCK
        valid = (dist >= 0) & (dist <= KEYS_PER_BLOCK)
        valid_first = valid & (col >= KEYS_PER_BLOCK)
        distf = (dil * dist).astype(F32)
        for hh in range(HEADS_PER_GROUP):
            b = -(_slope(group, hh) * distf)
            bias_ref[0, hh] = jnp.where(valid, b, NEG)
            bias_ref[1, hh] = jnp.where(valid_first, b, NEG)

    kcat_ref[0:tail] = kp_ref[...]
    kcat_ref[tail:] = k_ref[...]
    vcat_ref[0:tail] = vp_ref[...]
    vcat_ref[tail:] = v_ref[...]

    shift = dil.bit_length() - 1

    def body(j, carry):
        row0 = pl.multiple_of(j * KEYS_PER_BLOCK, KEYS_PER_BLOCK)
        qb = q_ref[pl.ds(row0, KEYS_PER_BLOCK), :]
        kprev = kcat_ref[pl.ds(row0, KEYS_PER_BLOCK), :]
        kself = kcat_ref[pl.ds(row0 + tail, KEYS_PER_BLOCK), :]
        vprev = vcat_ref[pl.ds(row0, KEYS_PER_BLOCK), :]
        vself = vcat_ref[pl.ds(row0 + tail, KEYS_PER_BLOCK), :]
        first = jnp.logical_and(t == 0, j < dil).astype(jnp.int32)
        start = (j >> shift) * tail + (j & (dil - 1))
        rows = pl.ds(start, KEYS_PER_BLOCK, stride=dil) if dil > 1 else pl.ds(start, KEYS_PER_BLOCK)
        for hh in range(HEADS_PER_GROUP):
            cols = slice(hh * HEAD_DIM, (hh + 1) * HEAD_DIM)
            kc = jnp.concatenate([kprev[:, cols], kself[:, cols]], axis=0)
            vc = jnp.concatenate([vprev[:, cols], vself[:, cols]], axis=0)
            s = lax.dot_general(qb[:, cols], kc, (((1,), (1,)), ((), ())), preferred_element_type=F32)
            s = s + bias_ref[first, hh]
            m = jnp.max(s, axis=-1, keepdims=True)
            p = jnp.exp(s - m)
            l = jnp.sum(p, axis=-1, keepdims=True)
            acc = jnp.dot(p.astype(BF16), vc, preferred_element_type=F32)
            o_ref[hh, rows, :] = acc / l
            lse_ref[hh, rows, :] = jnp.broadcast_to(m + jnp.log(l), (KEYS_PER_BLOCK, HEAD_DIM))
        return carry

    lax.fori_loop(0, BLOCKS_PER_ATTN_STEP, body, 0)


def _attn_call(group, q, k, v):
    dil = GROUPS[group][1]
    tail = KEYS_PER_BLOCK * dil
    steps = SEQ // ATTN_STEP_ROWS
    tails_per_step = ATTN_STEP_ROWS // tail
    cur = pl.BlockSpec((ATTN_STEP_ROWS, GROUP_COLS), lambda t: (t, 0))
    prev = pl.BlockSpec((tail, GROUP_COLS), lambda t: (jnp.maximum(t * tails_per_step - 1, 0), 0))
    out = pl.BlockSpec((HEADS_PER_GROUP, ATTN_STEP_ROWS, HEAD_DIM), lambda t: (0, t, 0))
    out_shape = jax.ShapeDtypeStruct((HEADS_PER_GROUP, SEQ, HEAD_DIM), F32)
    return pl.pallas_call(
        functools.partial(_attn_kernel, group=group),
        grid=(steps,),
        in_specs=[cur, cur, cur, prev, prev],
        out_specs=[out, out],
        out_shape=[out_shape, out_shape],
        scratch_shapes=[
            pltpu.VMEM((tail + ATTN_STEP_ROWS, GROUP_COLS), BF16),
            pltpu.VMEM((tail + ATTN_STEP_ROWS, GROUP_COLS), BF16),
            pltpu.VMEM((2, HEADS_PER_GROUP, KEYS_PER_BLOCK, 2 * KEYS_PER_BLOCK), F32),
        ],
        compiler_params=_params(("arbitrary",)),
        name=f"prompt_attn_g{group}",
    )(q, k, v, k, v)


_GA = (0, GROUP_COLS)
_GLU_A = (_GA[1], _GA[1] + CONV_CH)
_GLU_G = (_GLU_A[1], _GLU_A[1] + CONV_CH)
_GB = (_GLU_G[1], _GLU_G[1] + CONV_CH)
_MA = (_GB[1], _GB[1] + D_MODEL)
_MB = (_MA[1], _MA[1] + D_MODEL)


def _merge_out(x, gate, o_attn, conv_out, ga, gb, ma, mb, wpa_ref, wpb_ref, wo_ref, lng_ref, lnb_ref):
    a = jnp.dot((o_attn * _silu(ga)).astype(BF16), wpa_ref[...], preferred_element_type=F32)
    b = jnp.dot((conv_out * _silu(gb)).astype(BF16), wpb_ref[...], preferred_element_type=F32)
    pre = (_sigmoid(ma) * a + _sigmoid(mb) * b).astype(BF16)
    y = jnp.dot(pre, wo_ref[...], preferred_element_type=F32)
    return _layer_norm(ALPHA * x + gate * y, lng_ref[...], lnb_ref[...])


def _tail_kernel(x_ref, mod_ref, o0_ref, o1_ref, o2_ref, l0_ref, l1_ref, l2_ref,
                 w_ref, b_ref, cw_ref, cb_ref, cg_ref, cbeta_ref, wpa_ref, wpb_ref, wo_ref, lng_ref, lnb_ref,
                 y_ref, convp_ref, uext_ref):
    i = pl.program_id(0)
    shift = mod_ref[:, 0:D_MODEL]
    scale = mod_ref[:, D_MODEL:2 * D_MODEL]
    gate = mod_ref[:, 2 * D_MODEL:3 * D_MODEL]
    x = x_ref[...]
    hb = (x * (1.0 + scale) + shift).astype(BF16)

    def proj(rng):
        return jnp.dot(hb, w_ref[:, rng[0]:rng[1]], preferred_element_type=F32) + b_ref[:, rng[0]:rng[1]]

    heads = []
    for hh in range(HEADS_PER_GROUP):
        l0, l1, l2 = l0_ref[hh], l1_ref[hh], l2_ref[hh]
        m = jnp.maximum(jnp.maximum(l0, l1), l2)
        e0, e1, e2 = jnp.exp(l0 - m), jnp.exp(l1 - m), jnp.exp(l2 - m)
        heads.append((e0 * o0_ref[hh] + e1 * o1_ref[hh] + e2 * o2_ref[hh]) / (e0 + e1 + e2))
    o_attn = jnp.concatenate(heads, axis=-1)

    u = proj(_GLU_A) * _sigmoid(proj(_GLU_G))

    @pl.when(i == 0)
    def _():
        uext_ref[0:HIST_ROWS] = jnp.zeros((HIST_ROWS, CONV_CH), F32)

    uext_ref[HIST_ROWS:] = u
    lead = HIST_ROWS - (CONV_WIDTH - 1)
    acc = jnp.broadcast_to(cb_ref[...], (ROW_TILE, CONV_CH))
    for j in range(CONV_WIDTH):
        acc = acc + uext_ref[pl.ds(lead + j, ROW_TILE), :] * cw_ref[j:j + 1, :]
    conv_out = _silu(_layer_norm(acc, cg_ref[...], cbeta_ref[...]))
    uext_ref[0:HIST_ROWS] = uext_ref[ROW_TILE:ROW_TILE + HIST_ROWS]
    convp_ref[...] = u[ROW_TILE - (CONV_WIDTH - 1):, :]

    y_ref[...] = _merge_out(x, gate, o_attn, conv_out, proj(_GA), proj(_GB), proj(_MA), proj(_MB),
                            wpa_ref, wpb_ref, wo_ref, lng_ref, lnb_ref)


def _tail_call(x, mod_p, outs, lses, w_rest, b_rest, conv_w, conv_b, cn_g, cn_b, w_pa, w_pb, w_o, ln_g, ln_b):
    head_blk = pl.BlockSpec((HEADS_PER_GROUP, ROW_TILE, HEAD_DIM), lambda i: (0, i, 0))
    row = lambda n: pl.BlockSpec((1, n), lambda i: (0, 0))
    return pl.pallas_call(
        _tail_kernel,
        grid=(N_ROW_TILES,),
        in_specs=[
            pl.BlockSpec((ROW_TILE, D_MODEL), lambda i: (i, 0)),
            row(3 * D_MODEL),
            head_blk, head_blk, head_blk, head_blk, head_blk, head_blk,
            _resident((D_MODEL, REST_COLS)),
            row(REST_COLS),
            pl.BlockSpec((CONV_WIDTH, CONV_CH), lambda i: (0, 0)),
            row(CONV_CH), row(CONV_CH), row(CONV_CH),
            _resident((GROUP_COLS, D_MODEL)),
            _resident((CONV_CH, D_MODEL)),
            _resident((D_MODEL, D_MODEL)),
            row(D_MODEL), row(D_MODEL),
        ],
        out_specs=[
            pl.BlockSpec((ROW_TILE, D_MODEL), lambda i: (i, 0)),
            pl.BlockSpec((CONV_WIDTH - 1, CONV_CH), lambda i: (0, 0)),
        ],
        out_shape=[
            jax.ShapeDtypeStruct((SEQ, D_MODEL), F32),
            jax.ShapeDtypeStruct((CONV_WIDTH - 1, CONV_CH), F32),
        ],
        scratch_shapes=[pltpu.VMEM((HIST_ROWS + ROW_TILE, CONV_CH), F32)],
        compiler_params=_params(("arbitrary",)),
        name="prompt_tail",
    )(x, mod_p, *outs, *lses, w_rest, b_rest, conv_w, conv_b, cn_g, cn_b, w_pa, w_pb, w_o, ln_g, ln_b)


def _sample_proj_kernel(x_ref, mod_ref, wq_ref, wr_ref, bq_ref, br_ref,
                        qt0_ref, qt1_ref, qt2_ref, kv0_ref, kv1_ref, kv2_ref, zr_ref):
    shift = mod_ref[:, 0:D_MODEL]
    scale = mod_ref[:, D_MODEL:2 * D_MODEL]
    hb = (x_ref[...] * (1.0 + scale) + shift).astype(BF16)

    def proj(blk):
        cols = slice(blk * GROUP_COLS, (blk + 1) * GROUP_COLS)
        return jnp.dot(hb, wq_ref[:, cols], preferred_element_type=F32) + bq_ref[:, cols]

    zeros = jnp.zeros((DEC_BATCH, GROUP_COLS), F32)
    for g, (qt_ref, kv_ref) in enumerate(((qt0_ref, kv0_ref), (qt1_ref, kv1_ref), (qt2_ref, kv2_ref))):
        _store_kv_tiles(qt_ref, proj(g) * Q_SCALE, zeros)
        _store_kv_tiles(kv_ref, proj(N_GROUPS + g), proj(2 * N_GROUPS + g))
    for blk in range(REST_COLS // GROUP_COLS):
        cols = slice(blk * GROUP_COLS, (blk + 1) * GROUP_COLS)
        zr_ref[:, cols] = jnp.dot(hb, wr_ref[:, cols], preferred_element_type=F32) + br_ref[:, cols]


def _sample_proj_call(x_s, mod_s, w_qkv, w_rest, b_qkv, b_rest):
    full = lambda shape: pl.BlockSpec(shape, lambda i: (0,) * len(shape))
    tile_rows = DEC_BATCH * KV_TILE_ROWS
    tiles = jax.ShapeDtypeStruct((tile_rows, HEAD_DIM), F32)
    return pl.pallas_call(
        _sample_proj_kernel,
        grid=(1,),
        in_specs=[full((DEC_BATCH, D_MODEL)), full((DEC_BATCH, 3 * D_MODEL)),
                  _resident((D_MODEL, QKV_COLS)), _resident((D_MODEL, REST_COLS)),
                  full((1, QKV_COLS)), full((1, REST_COLS))],
        out_specs=[full((tile_rows, HEAD_DIM))] * 6 + [full((DEC_BATCH, REST_COLS))],
        out_shape=[tiles] * 6 + [jax.ShapeDtypeStruct((DEC_BATCH, REST_COLS), F32)],
        compiler_params=_params(("arbitrary",)),
        name="sample_proj",
    )(x_s, mod_s, w_qkv, w_rest, b_qkv, b_rest)


def _sample_attn_kernel(qt0_ref, qt1_ref, qt2_ref, kv0_ref, kv1_ref, kv2_ref, c0_ref, c1_ref, c2_ref, o_ref,
                        bias_ref):
    qt_refs = (qt0_ref, qt1_ref, qt2_ref)
    kv_refs = (kv0_ref, kv1_ref, kv2_ref)
    c_refs = (c0_ref, c1_ref, c2_ref)
    half = HEADS_PER_GROUP
    tile = (KV_TILE_ROWS, HEAD_DIM)
    keys_tile = (KEYS_PER_BLOCK,) + tile

    @pl.when(pl.program_id(0) == 0)
    def _():
        key = lax.broadcasted_iota(jnp.int32, keys_tile, 0)
        sub = lax.broadcasted_iota(jnp.int32, keys_tile, 1)
        steps_back = (KEYS_PER_BLOCK - key).astype(F32)
        for g in range(N_GROUPS):
            slope_rows = jnp.zeros(keys_tile, F32)
            for hh in range(HEADS_PER_GROUP):
                slope_rows = jnp.where(sub == half + hh, _slope(g, hh), slope_rows)
            bias_ref[g] = -(slope_rows * (GROUPS[g][1] * steps_back))

    def body(b, carry):
        outs, lses = [], []
        for g in range(N_GROUPS):
            x = c_refs[g][b]
            qt = qt_refs[g][b]
            kvn = kv_refs[g][b]
            s = jnp.broadcast_to(jnp.sum(x * qt[None], axis=-1, keepdims=True), keys_tile)
            s = pltpu.roll(s, half, axis=1) + bias_ref[g]
            s_self = jnp.broadcast_to(jnp.sum(qt * kvn, axis=-1, keepdims=True), tile)
            s_self = pltpu.roll(s_self, half, axis=0)
            m = jnp.maximum(jnp.max(s, axis=0), s_self)
            p = jnp.exp(s - m[None])
            p_self = jnp.exp(s_self - m)
            l = jnp.sum(p, axis=0) + p_self
            acc = jnp.sum(x * p, axis=0) + p_self * kvn
            outs.append(acc / l)
            lses.append(m + jnp.log(l))
        mx = jnp.maximum(jnp.maximum(lses[0], lses[1]), lses[2])
        es = [jnp.exp(ls - mx) for ls in lses]
        o_ref[b] = (es[0] * outs[0] + es[1] * outs[1] + es[2] * outs[2]) / (es[0] + es[1] + es[2])
        return carry

    lax.fori_loop(0, SAMPLE_BLOCK, body, 0)


def _sample_attn_call(qts, kvs, caches):
    views = [c.reshape(DEC_BATCH, KEYS_PER_BLOCK, GROUPS[g][1], KV_TILE_ROWS, HEAD_DIM) for g, c in enumerate(caches)]
    tile3 = lambda a: a.reshape(DEC_BATCH, KV_TILE_ROWS, HEAD_DIM)
    tile_blk = pl.BlockSpec((SAMPLE_BLOCK, KV_TILE_ROWS, HEAD_DIM), lambda i: (i, 0, 0))
    cache_blk = pl.BlockSpec((SAMPLE_BLOCK, KEYS_PER_BLOCK, None, KV_TILE_ROWS, HEAD_DIM), lambda i: (i, 0, 0, 0, 0))
    return pl.pallas_call(
        _sample_attn_kernel,
        grid=(DEC_BATCH // SAMPLE_BLOCK,),
        in_specs=[tile_blk] * 6 + [cache_blk] * 3,
        out_specs=tile_blk,
        out_shape=jax.ShapeDtypeStruct((DEC_BATCH, KV_TILE_ROWS, HEAD_DIM), F32),
        scratch_shapes=[pltpu.VMEM((N_GROUPS, KEYS_PER_BLOCK, KV_TILE_ROWS, HEAD_DIM), F32)],
        compiler_params=_params(("arbitrary",)),
        name="sample_attn",
    )(*[tile3(a) for a in qts], *[tile3(a) for a in kvs], *views)


def _sample_tail_kernel(x_ref, mod_ref, o_ref, zr_ref, st_ref, cw_ref, cb_ref, cg_ref, cbeta_ref,
                        wpa_ref, wpb_ref, wo_ref, lng_ref, lnb_ref, y_ref, convs_ref):
    gate = mod_ref[:, 2 * D_MODEL:3 * D_MODEL]
    z = lambda rng: zr_ref[:, rng[0]:rng[1]]
    u = z(_GLU_A) * _sigmoid(z(_GLU_G))
    hist = CONV_WIDTH - 1
    acc = cb_ref[...] + u * cw_ref[hist:hist + 1, :]
    for j in range(hist):
        acc = acc + st_ref[j] * cw_ref[j:j + 1, :]
    conv_out = _silu(_layer_norm(acc, cg_ref[...], cbeta_ref[...]))
    for j in range(hist - 1):
        convs_ref[j] = st_ref[j + 1]
    convs_ref[hist - 1] = u
    o_attn = jnp.concatenate(
        [o_ref[pl.ds(HEADS_PER_GROUP + hh, DEC_BATCH, stride=KV_TILE_ROWS), :] for hh in range(HEADS_PER_GROUP)],
        axis=-1)
    y_ref[...] = _merge_out(x_ref[...], gate, o_attn, conv_out, z(_GA), z(_GB), z(_MA), z(_MB),
                            wpa_ref, wpb_ref, wo_ref, lng_ref, lnb_ref)


def _sample_tail_call(x_s, mod_s, o_s, zr, state_t, conv_w, conv_b, cn_g, cn_b, w_pa, w_pb, w_o, ln_g, ln_b):
    full = lambda shape: pl.BlockSpec(shape, lambda i: (0,) * len(shape))
    args = (x_s, mod_s, o_s, zr, state_t, conv_w, conv_b, cn_g, cn_b, w_pa, w_pb, w_o, ln_g, ln_b)
    return pl.pallas_call(
        _sample_tail_kernel,
        grid=(1,),
        in_specs=[full(a.shape) for a in args],
        out_specs=[full((DEC_BATCH, D_MODEL)), full(state_t.shape)],
        out_shape=[jax.ShapeDtypeStruct((DEC_BATCH, D_MODEL), F32),
                   jax.ShapeDtypeStruct(state_t.shape, F32)],
        compiler_params=_params(("arbitrary",)),
        name="sample_tail",
    )(*args)


def kernel(x_prompt, x_sample, c_prompt, c_sample, cache_kv_w128, cache_kv_w512, cache_kv_w2048, state_conv,
           w_c, b_c, w_in, b_in, conv_w, conv_b, conv_norm_g, conv_norm_b, w_pa, w_pb, w_o, ln_g, ln_b):
    assert x_prompt.shape == (1, SEQ, D_MODEL) and x_sample.shape == (DEC_BATCH, 1, D_MODEL)
    assert w_in.shape == (D_MODEL, QKV_COLS + REST_COLS)
    caches = (cache_kv_w128, cache_kv_w512, cache_kv_w2048)
    for (window, _), c in zip(GROUPS, caches):
        assert c.shape == (DEC_BATCH, window, 2, HEADS_PER_GROUP, HEAD_DIM)

    w_qkv = w_in[:, :QKV_COLS].astype(BF16)
    w_rest = w_in[:, QKV_COLS:].astype(BF16)
    w_pa_b, w_pb_b, w_o_b = w_pa.astype(BF16), w_pb.astype(BF16), w_o.astype(BF16)
    b_qkv = b_in[:QKV_COLS].reshape(1, QKV_COLS)
    b_rest = b_in[QKV_COLS:].reshape(1, REST_COLS)
    vec = lambda a: a.reshape(1, -1)
    conv_args = (conv_w, vec(conv_b), vec(conv_norm_g), vec(conv_norm_b))
    out_args = (w_pa_b, w_pb_b, w_o_b, vec(ln_g), vec(ln_b))

    mod_p, mod_s = _mod_call(c_prompt, c_sample, w_c, b_c)

    x_p = x_prompt.reshape(SEQ, D_MODEL)
    q0, k0, v0, q1, k1, v1, q2, k2, v2, kv0, kv1, kv2 = _qkv_call(x_p, mod_p, w_qkv, b_qkv)
    flat = lambda a: a.reshape(SEQ, GROUP_COLS)
    attn = [_attn_call(0, q0, k0, v0), _attn_call(1, q1, k1, v1), _attn_call(2, flat(q2), flat(k2), flat(v2))]
    y_p, conv_p = _tail_call(x_p, mod_p, [a[0] for a in attn], [a[1] for a in attn],
                             w_rest, b_rest, *conv_args, *out_args)

    x_s = x_sample.reshape(DEC_BATCH, D_MODEL)
    qt0, qt1, qt2, kvs0, kvs1, kvs2, zr = _sample_proj_call(x_s, mod_s, w_qkv, w_rest, b_qkv, b_rest)
    o_s = _sample_attn_call((qt0, qt1, qt2), (kvs0, kvs1, kvs2), caches)
    o_s = o_s.reshape(DEC_BATCH * KV_TILE_ROWS, HEAD_DIM)
    state_t = jnp.transpose(state_conv, (1, 0, 2))
    y_s, conv_s_t = _sample_tail_call(x_s, mod_s, o_s, zr, state_t, *conv_args, *out_args)

    kv_shape_p = lambda keep: (1, keep, 2, HEADS_PER_GROUP, HEAD_DIM)
    kv_shape_s = (DEC_BATCH, 1, 2, HEADS_PER_GROUP, HEAD_DIM)
    return (y_p.reshape(1, SEQ, D_MODEL),
            y_s.reshape(DEC_BATCH, 1, D_MODEL),
            kv0.reshape(kv_shape_p(GROUPS[0][0])),
            kv1.reshape(kv_shape_p(GROUPS[1][0])),
            kv2.reshape(kv_shape_p(GROUPS[2][0])),
            conv_p.reshape(1, CONV_WIDTH - 1, CONV_CH),
            kvs0.reshape(kv_shape_s), kvs1.reshape(kv_shape_s), kvs2.reshape(kv_shape_s),
            jnp.transpose(conv_s_t, (1, 0, 2)))
```

```python
import functools

import jax
import jax.numpy as jnp
from jax import lax
from jax.experimental import pallas as pl
from jax.experimental.pallas import tpu as pltpu

F32 = jnp.float32
BF16 = jnp.bfloat16

D_MODEL = 1024
SEQ = 16384
DEC_BATCH = 128
DEPTH = 1
HEAD_DIM = 128
HEADS_PER_GROUP = 4
GROUPS = ((128, 1), (512, 4), (2048, 16))
N_GROUPS = len(GROUPS)
N_HEADS = N_GROUPS * HEADS_PER_GROUP
GROUP_COLS = HEADS_PER_GROUP * HEAD_DIM
ATTN_QKV = N_HEADS * HEAD_DIM
QKV_COLS = 3 * ATTN_QKV
CONV_CH = D_MODEL // 2
CONV_WIDTH = 31
COL_GA = QKV_COLS
COL_GLU_A = COL_GA + GROUP_COLS
COL_GLU_G = COL_GLU_A + CONV_CH
COL_GB = COL_GLU_G + CONV_CH
COL_MA = COL_GB + CONV_CH
COL_MB = COL_MA + D_MODEL
IN_COLS = COL_MB + D_MODEL
ALPHA = (2.0 * DEPTH) ** 0.25
LN_EPS = 1e-5
NEG = -1e30
Q_SCALE = HEAD_DIM ** -0.5
KEYS_PER_BLOCK = 128
KV_TILE_ROWS = 2 * HEADS_PER_GROUP

LANES = 128
MXU_COLS = 256
W_BLOCK = 512
ROW_TILE = 512
ATTN_STEP_ROWS = 2048
BLOCKS_PER_ATTN_STEP = ATTN_STEP_ROWS // KEYS_PER_BLOCK
CONV_ROW_CHUNK = 128
HIST_ROWS = 32
SAMPLE_BLOCK = 8
VMEM_LIMIT_BYTES = 56 * 1024 * 1024


def _slope(group, head):
    return 2.0 ** (-8.0 * (group * HEADS_PER_GROUP + head + 1) / N_HEADS)


def _sigmoid(x):
    return 1.0 / (1.0 + jnp.exp(-x))


def _silu(x):
    return x * _sigmoid(x)


def _layer_norm(x, g, b):
    mu = jnp.mean(x, axis=-1, keepdims=True)
    xc = x - mu
    var = jnp.mean(xc * xc, axis=-1, keepdims=True)
    return xc * lax.rsqrt(var + LN_EPS) * g + b


def _col_blocks(width):
    return [slice(b * MXU_COLS, (b + 1) * MXU_COLS) for b in range(width // MXU_COLS)]


def _row_stats(ref, width):
    total = sum(jnp.sum(ref[:, cols], axis=-1, keepdims=True) for cols in _col_blocks(width))
    mu = total / width
    sq = sum(jnp.sum(jnp.square(ref[:, cols] - mu), axis=-1, keepdims=True) for cols in _col_blocks(width))
    return mu, lax.rsqrt(sq / width + LN_EPS)


def _resident(shape):
    return pl.BlockSpec(shape, lambda *_: (0,) * len(shape), pipeline_mode=pl.Buffered(1))


def _w_cols(col0, width):
    assert col0 % width == 0
    return pl.BlockSpec((D_MODEL, width), lambda *_: (0, col0 // width), pipeline_mode=pl.Buffered(1))


def _row_vec(n):
    return pl.BlockSpec((1, n), lambda *_: (0, 0))


def _params(semantics):
    return pltpu.CompilerParams(dimension_semantics=semantics, vmem_limit_bytes=VMEM_LIMIT_BYTES)


def _store_kv_tiles(ref, k, v):
    rows = k.shape[0]
    for hh in range(HEADS_PER_GROUP):
        cols = slice(hh * HEAD_DIM, (hh + 1) * HEAD_DIM)
        ref[pl.ds(hh, rows, stride=KV_TILE_ROWS), :] = k[:, cols]
        ref[pl.ds(HEADS_PER_GROUP + hh, rows, stride=KV_TILE_ROWS), :] = v[:, cols]


def _mod_kernel(cp_ref, cs_ref, w_ref, b_ref, mp_ref, ms_ref):
    w = w_ref[...].astype(BF16)
    cp = jnp.broadcast_to(cp_ref[...], (8, D_MODEL)).astype(BF16)
    mp_ref[...] = jnp.dot(cp, w, preferred_element_type=F32)[0:1] + b_ref[...]
    ms_ref[...] = jnp.dot(cs_ref[...].astype(BF16), w, preferred_element_type=F32) + b_ref[...]


def _mod_call(c_prompt, c_sample, w_c, b_c):
    n_blk = 3
    return pl.pallas_call(
        _mod_kernel,
        grid=(n_blk,),
        in_specs=[
            pl.BlockSpec((1, D_MODEL), lambda n: (0, 0)),
            pl.BlockSpec((DEC_BATCH, D_MODEL), lambda n: (0, 0)),
            pl.BlockSpec((D_MODEL, D_MODEL), lambda n: (0, n)),
            pl.BlockSpec((1, D_MODEL), lambda n: (0, n)),
        ],
        out_specs=[
            pl.BlockSpec((1, D_MODEL), lambda n: (0, n)),
            pl.BlockSpec((DEC_BATCH, D_MODEL), lambda n: (0, n)),
        ],
        out_shape=[
            jax.ShapeDtypeStruct((1, 3 * D_MODEL), F32),
            jax.ShapeDtypeStruct((DEC_BATCH, 3 * D_MODEL), F32),
        ],
        compiler_params=_params(("arbitrary",)),
        name="adaln_mod",
    )(c_prompt, c_sample, w_c, b_c.reshape(1, 3 * D_MODEL))


N_ROW_TILES = SEQ // ROW_TILE
KV2_TILES = GROUPS[2][0] // ROW_TILE


def _front_kernel(x_ref, mod_ref, w_ref, wa_ref, wg_ref, wgb_ref, b_ref, cw_ref, cb_ref, cg_ref, cbeta_ref,
                  q0_ref, k0_ref, v0_ref, q1_ref, k1_ref, v1_ref, q2_ref, k2_ref, v2_ref,
                  kv0_ref, kv1_ref, kv2_ref, bin_ref, convp_ref,
                  hs_ref, hb_ref, hp4_ref, hp16_ref, uext_ref, cacc_ref):
    i = pl.program_id(0)
    shift = mod_ref[:, 0:D_MODEL]
    scale = mod_ref[:, D_MODEL:2 * D_MODEL]
    h = x_ref[...] * (1.0 + scale) + shift
    hb_ref[...] = h.astype(BF16)

    n_slab = D_MODEL // LANES
    for c in range(n_slab):
        hs_ref[c] = h[:, c * LANES:(c + 1) * LANES]
    d1, d2 = GROUPS[1][1], GROUPS[2][1]
    r1, r2 = ROW_TILE // d1, ROW_TILE // d2
    for c in range(n_slab):
        cols = slice(c * LANES, (c + 1) * LANES)
        for r in range(d1):
            hp4_ref[r * r1:(r + 1) * r1, cols] = hs_ref[c, pl.ds(r, r1, stride=d1), :].astype(BF16)
        for r in range(d2):
            hp16_ref[r * r2:(r + 1) * r2, cols] = hs_ref[c, pl.ds(r, r2, stride=d2), :].astype(BF16)

    def proj(lhs_ref, blk):
        cols = slice(blk * GROUP_COLS, (blk + 1) * GROUP_COLS)
        return jnp.dot(lhs_ref[...], w_ref[:, cols], preferred_element_type=F32) + b_ref[:, cols]

    def gate_proj(wx_ref, col0, cols):
        bcols = slice(col0 + cols.start, col0 + cols.stop)
        return jnp.dot(hb_ref[...], wx_ref[:, cols], preferred_element_type=F32) + b_ref[:, bcols]

    @pl.when(i == 0)
    def _():
        uext_ref[:, 0:HIST_ROWS, :] = jnp.zeros((CONV_CH // LANES, HIST_ROWS, LANES), F32)

    slabs_per_blk = MXU_COLS // LANES
    for b, cols in enumerate(_col_blocks(CONV_CH)):
        u = gate_proj(wa_ref, COL_GLU_A, cols) * _sigmoid(gate_proj(wg_ref, COL_GLU_G, cols))
        convp_ref[:, cols] = u[ROW_TILE - (CONV_WIDTH - 1):, :]
        for s in range(slabs_per_blk):
            uext_ref[b * slabs_per_blk + s, HIST_ROWS:, :] = u[:, s * LANES:(s + 1) * LANES]

    lead = HIST_ROWS - (CONV_WIDTH - 1)
    for c in range(CONV_CH // LANES):
        lanes = slice(c * LANES, (c + 1) * LANES)
        for r0 in range(0, ROW_TILE, CONV_ROW_CHUNK):
            acc = jnp.broadcast_to(cb_ref[:, lanes], (CONV_ROW_CHUNK, LANES))
            for j in range(CONV_WIDTH):
                acc = acc + uext_ref[c, pl.ds(r0 + lead + j, CONV_ROW_CHUNK), :] * cw_ref[j:j + 1, lanes]
            cacc_ref[r0:r0 + CONV_ROW_CHUNK, lanes] = acc
        uext_ref[c, 0:HIST_ROWS, :] = uext_ref[c, ROW_TILE:ROW_TILE + HIST_ROWS, :]

    mu, rstd = _row_stats(cacc_ref, CONV_CH)
    for cols in _col_blocks(CONV_CH):
        normed = (cacc_ref[:, cols] - mu) * rstd * cg_ref[:, cols] + cbeta_ref[:, cols]
        bin_ref[:, cols] = (_silu(normed) * _silu(gate_proj(wgb_ref, COL_GB, cols))).astype(BF16)

    def store_perm16(ref, val):
        val = val.astype(BF16)
        for r in range(d2):
            ref[0, r] = val[r * r2:(r + 1) * r2, :]

    q0_ref[...] = (proj(hb_ref, 0) * Q_SCALE).astype(BF16)
    k0 = proj(hb_ref, 3)
    v0 = proj(hb_ref, 6)
    k0_ref[...] = k0.astype(BF16)
    v0_ref[...] = v0.astype(BF16)
    keep0 = GROUPS[0][0]
    _store_kv_tiles(kv0_ref, k0[ROW_TILE - keep0:, :], v0[ROW_TILE - keep0:, :])
    q1_ref[...] = (proj(hp4_ref, 1) * Q_SCALE).astype(BF16)
    k1_ref[...] = proj(hp4_ref, 4).astype(BF16)
    v1_ref[...] = proj(hp4_ref, 7).astype(BF16)
    store_perm16(q2_ref, proj(hp16_ref, 2) * Q_SCALE)
    store_perm16(k2_ref, proj(hp16_ref, 5))
    store_perm16(v2_ref, proj(hp16_ref, 8))

    @pl.when(i == N_ROW_TILES - 1)
    def _():
        _store_kv_tiles(kv1_ref, proj(hb_ref, 4), proj(hb_ref, 7))

    @pl.when(i >= N_ROW_TILES - KV2_TILES)
    def _():
        _store_kv_tiles(kv2_ref, proj(hb_ref, 5), proj(hb_ref, 8))


def _front_call(x, mod_p, w_bf, b_in, conv_w, conv_b, cn_g, cn_b):
    d2 = GROUPS[2][1]
    spans2 = SEQ // GROUPS[2][0]
    tiles_per_span2 = GROUPS[2][0] // ROW_TILE
    nat = pl.BlockSpec((ROW_TILE, GROUP_COLS), lambda i: (i, 0))
    perm16 = pl.BlockSpec((1, d2, ROW_TILE // d2, GROUP_COLS),
                          lambda i: (i // tiles_per_span2, 0, i % tiles_per_span2, 0))
    nat_shape = jax.ShapeDtypeStruct((SEQ, GROUP_COLS), BF16)
    p16_shape = jax.ShapeDtypeStruct((spans2, d2, KEYS_PER_BLOCK, GROUP_COLS), BF16)
    first_kv2_tile = N_ROW_TILES - KV2_TILES
    kv_shape = lambda g: jax.ShapeDtypeStruct((GROUPS[g][0] * KV_TILE_ROWS, HEAD_DIM), F32)
    return pl.pallas_call(
        _front_kernel,
        grid=(N_ROW_TILES,),
        in_specs=[
            pl.BlockSpec((ROW_TILE, D_MODEL), lambda i: (i, 0)),
            _row_vec(3 * D_MODEL),
            _w_cols(0, QKV_COLS), _w_cols(COL_GLU_A, W_BLOCK), _w_cols(COL_GLU_G, W_BLOCK), _w_cols(COL_GB, W_BLOCK),
            _row_vec(IN_COLS),
            pl.BlockSpec((CONV_WIDTH, CONV_CH), lambda i: (0, 0)),
            _row_vec(CONV_CH), _row_vec(CONV_CH), _row_vec(CONV_CH),
        ],
        out_specs=[nat, nat, nat, nat, nat, nat, perm16, perm16, perm16,
                   pl.BlockSpec((GROUPS[0][0] * KV_TILE_ROWS, HEAD_DIM), lambda i: (0, 0)),
                   pl.BlockSpec((GROUPS[1][0] * KV_TILE_ROWS, HEAD_DIM), lambda i: (0, 0)),
                   pl.BlockSpec((ROW_TILE * KV_TILE_ROWS, HEAD_DIM),
                                lambda i: (jnp.maximum(i - first_kv2_tile, 0), 0)),
                   nat,
                   pl.BlockSpec((CONV_WIDTH - 1, CONV_CH), lambda i: (0, 0))],
        out_shape=[nat_shape] * 6 + [p16_shape] * 3 + [kv_shape(0), kv_shape(1), kv_shape(2), nat_shape,
                                                        jax.ShapeDtypeStruct((CONV_WIDTH - 1, CONV_CH), F32)],
        scratch_shapes=[
            pltpu.VMEM((D_MODEL // LANES, ROW_TILE, LANES), F32),
            pltpu.VMEM((ROW_TILE, D_MODEL), BF16),
            pltpu.VMEM((ROW_TILE, D_MODEL), BF16),
            pltpu.VMEM((ROW_TILE, D_MODEL), BF16),
            pltpu.VMEM((CONV_CH // LANES, HIST_ROWS + ROW_TILE, LANES), F32),
            pltpu.VMEM((ROW_TILE, CONV_CH), F32),
        ],
        compiler_params=_params(("arbitrary",)),
        name="prompt_front",
    )(x, mod_p, w_bf, w_bf, w_bf, w_bf, b_in, conv_w, conv_b, cn_g, cn_b)


def _attn_kernel(q_ref, k_ref, v_ref, kp_ref, vp_ref, o_ref, lse_ref, bias_ref, *, group):
    dil = GROUPS[group][1]
    tail = KEYS_PER_BLOCK * dil
    t = pl.program_id(0)

    @pl.when(t == 0)
    def _():
        row = lax.broadcasted_iota(jnp.int32, (KEYS_PER_BLOCK, 2 * KEYS_PER_BLOCK), 0)
        col = lax.broadcasted_iota(jnp.int32, (KEYS_PER_BLOCK, 2 * KEYS_PER_BLOCK), 1)
        dist = row - col + KEYS_PER_BLOCK
        valid = (dist >= 0) & (dist <= KEYS_PER_BLOCK)
        valid_first = valid & (col >= KEYS_PER_BLOCK)
        distf = (dil * dist).astype(F32)
        for hh in range(HEADS_PER_GROUP):
            b = -(_slope(group, hh) * distf)
            bias_ref[0, hh] = jnp.where(valid, b, NEG)
            bias_ref[1, hh] = jnp.where(valid_first, b, NEG)

    first_step = (t == 0).astype(jnp.int32)
    for j in range(BLOCKS_PER_ATTN_STEP):
        blk = slice(j * KEYS_PER_BLOCK, (j + 1) * KEYS_PER_BLOCK)
        qb = q_ref[blk, :]
        kself, vself = k_ref[blk, :], v_ref[blk, :]
        if j < dil:
            kprev, vprev = kp_ref[blk, :], vp_ref[blk, :]
        else:
            pblk = slice((j - dil) * KEYS_PER_BLOCK, (j - dil + 1) * KEYS_PER_BLOCK)
            kprev, vprev = k_ref[pblk, :], v_ref[pblk, :]
        first = first_step if j < dil else 0
        start = (j // dil) * tail + (j % dil)
        rows = pl.ds(start, KEYS_PER_BLOCK, stride=dil) if dil > 1 else pl.ds(start, KEYS_PER_BLOCK)
        for hh in range(HEADS_PER_GROUP):
            cols = slice(hh * HEAD_DIM, (hh + 1) * HEAD_DIM)
            kc = jnp.concatenate([kprev[:, cols], kself[:, cols]], axis=0)
            vc = jnp.concatenate([vprev[:, cols], vself[:, cols]], axis=0)
            s = lax.dot_general(qb[:, cols], kc, (((1,), (1,)), ((), ())), preferred_element_type=F32)
            s = s + bias_ref[first, hh]
            m = jnp.max(s, axis=-1, keepdims=True)
            p = jnp.exp(s - m)
            l = jnp.sum(p, axis=-1, keepdims=True)
            acc = jnp.dot(p.astype(BF16), vc, preferred_element_type=F32)
            o_ref[hh, rows, :] = acc / l
            lse_ref[hh, rows, :] = jnp.broadcast_to(m + jnp.log(l), (KEYS_PER_BLOCK, HEAD_DIM))


def _attn_call(group, q, k, v):
    dil = GROUPS[group][1]
    tail = KEYS_PER_BLOCK * dil
    steps = SEQ // ATTN_STEP_ROWS
    tails_per_step = ATTN_STEP_ROWS // tail
    cur = pl.BlockSpec((ATTN_STEP_ROWS, GROUP_COLS), lambda t: (t, 0))
    prev = pl.BlockSpec((tail, GROUP_COLS), lambda t: (jnp.maximum(t * tails_per_step - 1, 0), 0))
    out = pl.BlockSpec((HEADS_PER_GROUP, ATTN_STEP_ROWS, HEAD_DIM), lambda t: (0, t, 0))
    out_shape = jax.ShapeDtypeStruct((HEADS_PER_GROUP, SEQ, HEAD_DIM), F32)
    return pl.pallas_call(
        functools.partial(_attn_kernel, group=group),
        grid=(steps,),
        in_specs=[cur, cur, cur, prev, prev],
        out_specs=[out, out],
        out_shape=[out_shape, out_shape],
        scratch_shapes=[pltpu.VMEM((2, HEADS_PER_GROUP, KEYS_PER_BLOCK, 2 * KEYS_PER_BLOCK), F32)],
        compiler_params=_params(("arbitrary",)),
        name=f"prompt_attn_g{group}",
    )(q, k, v, k, v)


def _tail_kernel(x_ref, mod_ref, o0_ref, o1_ref, o2_ref, l0_ref, l1_ref, l2_ref, bin_ref,
                 wga_ref, wma0_ref, wma1_ref, wmb0_ref, wmb1_ref, b_ref, wpa_ref, wpb_ref, wo_ref, lng_ref, lnb_ref,
                 y_ref,
                 hb_ref, ain_ref, pre_ref, res_ref):
    shift = mod_ref[:, 0:D_MODEL]
    scale = mod_ref[:, D_MODEL:2 * D_MODEL]
    hb_ref[...] = (x_ref[...] * (1.0 + scale) + shift).astype(BF16)

    def gate_proj(wx_refs, col0, cols):
        wx_ref = wx_refs[cols.start // W_BLOCK]
        wcols = slice(cols.start % W_BLOCK, cols.start % W_BLOCK + MXU_COLS)
        bcols = slice(col0 + cols.start, col0 + cols.stop)
        return jnp.dot(hb_ref[...], wx_ref[:, wcols], preferred_element_type=F32) + b_ref[:, bcols]

    for cols in _col_blocks(GROUP_COLS):
        heads = []
        for hh in range(cols.start // HEAD_DIM, cols.stop // HEAD_DIM):
            l0, l1, l2 = l0_ref[hh], l1_ref[hh], l2_ref[hh]
            m = jnp.maximum(jnp.maximum(l0, l1), l2)
            e0, e1, e2 = jnp.exp(l0 - m), jnp.exp(l1 - m), jnp.exp(l2 - m)
            heads.append((e0 * o0_ref[hh] + e1 * o1_ref[hh] + e2 * o2_ref[hh]) / (e0 + e1 + e2))
        ga = gate_proj((wga_ref,), COL_GA, cols)
        ain_ref[:, cols] = (jnp.concatenate(heads, axis=-1) * _silu(ga)).astype(BF16)

    for cols in _col_blocks(D_MODEL):
        a = jnp.dot(ain_ref[...], wpa_ref[:, cols], preferred_element_type=F32)
        bb = jnp.dot(bin_ref[...], wpb_ref[:, cols], preferred_element_type=F32)
        ma = gate_proj((wma0_ref, wma1_ref), COL_MA, cols)
        mb = gate_proj((wmb0_ref, wmb1_ref), COL_MB, cols)
        pre_ref[:, cols] = (_sigmoid(ma) * a + _sigmoid(mb) * bb).astype(BF16)
    for cols in _col_blocks(D_MODEL):
        gate = mod_ref[:, 2 * D_MODEL + cols.start:2 * D_MODEL + cols.stop]
        y = jnp.dot(pre_ref[...], wo_ref[:, cols], preferred_element_type=F32)
        res_ref[:, cols] = ALPHA * x_ref[:, cols] + gate * y
    mu, rstd = _row_stats(res_ref, D_MODEL)
    for cols in _col_blocks(D_MODEL):
        y_ref[:, cols] = (res_ref[:, cols] - mu) * rstd * lng_ref[:, cols] + lnb_ref[:, cols]


def _tail_call(x, mod_p, outs, lses, b_conv, w_bf, b_in, w_pa, w_pb, w_o, ln_g, ln_b):
    head_blk = pl.BlockSpec((HEADS_PER_GROUP, ROW_TILE, HEAD_DIM), lambda i: (0, i, 0))
    return pl.pallas_call(
        _tail_kernel,
        grid=(N_ROW_TILES,),
        in_specs=[
            pl.BlockSpec((ROW_TILE, D_MODEL), lambda i: (i, 0)),
            _row_vec(3 * D_MODEL),
            head_blk, head_blk, head_blk, head_blk, head_blk, head_blk,
            pl.BlockSpec((ROW_TILE, CONV_CH), lambda i: (i, 0)),
            _w_cols(COL_GA, W_BLOCK),
            _w_cols(COL_MA, W_BLOCK), _w_cols(COL_MA + W_BLOCK, W_BLOCK),
            _w_cols(COL_MB, W_BLOCK), _w_cols(COL_MB + W_BLOCK, W_BLOCK),
            _row_vec(IN_COLS),
            _resident((GROUP_COLS, D_MODEL)),
            _resident((CONV_CH, D_MODEL)),
            _resident((D_MODEL, D_MODEL)),
            _row_vec(D_MODEL), _row_vec(D_MODEL),
        ],
        out_specs=pl.BlockSpec((ROW_TILE, D_MODEL), lambda i: (i, 0)),
        out_shape=jax.ShapeDtypeStruct((SEQ, D_MODEL), F32),
        scratch_shapes=[
            pltpu.VMEM((ROW_TILE, D_MODEL), BF16),
            pltpu.VMEM((ROW_TILE, GROUP_COLS), BF16),
            pltpu.VMEM((ROW_TILE, D_MODEL), BF16),
            pltpu.VMEM((ROW_TILE, D_MODEL), F32),
        ],
        compiler_params=_params(("arbitrary",)),
        name="prompt_tail",
    )(x, mod_p, *outs, *lses, b_conv, w_bf, w_bf, w_bf, w_bf, w_bf, b_in, w_pa, w_pb, w_o, ln_g, ln_b)


N_REST_BLOCKS = (IN_COLS - QKV_COLS) // W_BLOCK


def _sample_proj_kernel(x_ref, mod_ref, wq_ref, *rest):
    wr_refs = rest[:N_REST_BLOCKS]
    b_ref, qt0_ref, qt1_ref, qt2_ref, kv0_ref, kv1_ref, kv2_ref, zr_ref = rest[N_REST_BLOCKS:]
    shift = mod_ref[:, 0:D_MODEL]
    scale = mod_ref[:, D_MODEL:2 * D_MODEL]
    hb = (x_ref[...] * (1.0 + scale) + shift).astype(BF16)

    def proj(blk):
        cols = slice(blk * GROUP_COLS, (blk + 1) * GROUP_COLS)
        return jnp.dot(hb, wq_ref[:, cols], preferred_element_type=F32) + b_ref[:, cols]

    zeros = jnp.zeros((DEC_BATCH, GROUP_COLS), F32)
    for g, (qt_ref, kv_ref) in enumerate(((qt0_ref, kv0_ref), (qt1_ref, kv1_ref), (qt2_ref, kv2_ref))):
        _store_kv_tiles(qt_ref, proj(g) * Q_SCALE, zeros)
        _store_kv_tiles(kv_ref, proj(N_GROUPS + g), proj(2 * N_GROUPS + g))
    for blk, wr_ref in enumerate(wr_refs):
        cols = slice(blk * W_BLOCK, (blk + 1) * W_BLOCK)
        bcols = slice(QKV_COLS + cols.start, QKV_COLS + cols.stop)
        zr_ref[:, cols] = jnp.dot(hb, wr_ref[...], preferred_element_type=F32) + b_ref[:, bcols]


def _sample_proj_call(x_s, mod_s, w_bf, b_in):
    full = lambda shape: pl.BlockSpec(shape, lambda i: (0,) * len(shape))
    tile_rows = DEC_BATCH * KV_TILE_ROWS
    tiles = jax.ShapeDtypeStruct((tile_rows, HEAD_DIM), F32)
    rest_cols = IN_COLS - QKV_COLS
    return pl.pallas_call(
        _sample_proj_kernel,
        grid=(1,),
        in_specs=[full((DEC_BATCH, D_MODEL)), full((DEC_BATCH, 3 * D_MODEL)), _w_cols(0, QKV_COLS)]
                 + [_w_cols(QKV_COLS + blk * W_BLOCK, W_BLOCK) for blk in range(N_REST_BLOCKS)]
                 + [_row_vec(IN_COLS)],
        out_specs=[full((tile_rows, HEAD_DIM))] * 6 + [full((DEC_BATCH, rest_cols))],
        out_shape=[tiles] * 6 + [jax.ShapeDtypeStruct((DEC_BATCH, rest_cols), F32)],
        compiler_params=_params(("arbitrary",)),
        name="sample_proj",
    )(x_s, mod_s, w_bf, *([w_bf] * N_REST_BLOCKS), b_in)


def _sample_attn_kernel(qt0_ref, qt1_ref, qt2_ref, kv0_ref, kv1_ref, kv2_ref, c0_ref, c1_ref, c2_ref, o_ref,
                        bias_ref):
    qt_refs = (qt0_ref, qt1_ref, qt2_ref)
    kv_refs = (kv0_ref, kv1_ref, kv2_ref)
    c_refs = (c0_ref, c1_ref, c2_ref)
    half = HEADS_PER_GROUP
    tile = (KV_TILE_ROWS, HEAD_DIM)
    keys_tile = (KEYS_PER_BLOCK,) + tile

    @pl.when(pl.program_id(0) == 0)
    def _():
        key = lax.broadcasted_iota(jnp.int32, keys_tile, 0)
        sub = lax.broadcasted_iota(jnp.int32, keys_tile, 1)
        steps_back = (KEYS_PER_BLOCK - key).astype(F32)
        for g in range(N_GROUPS):
            slope_rows = jnp.zeros(keys_tile, F32)
            for hh in range(HEADS_PER_GROUP):
                slope_rows = jnp.where(sub == half + hh, _slope(g, hh), slope_rows)
            bias_ref[g] = -(slope_rows * (GROUPS[g][1] * steps_back))

    def body(b, carry):
        outs, lses = [], []
        for g in range(N_GROUPS):
            x = c_refs[g][b]
            qt = qt_refs[g][b]
            kvn = kv_refs[g][b]
            s = jnp.broadcast_to(jnp.sum(x * qt[None], axis=-1, keepdims=True), keys_tile)
            s = pltpu.roll(s, half, axis=1) + bias_ref[g]
            s_self = jnp.broadcast_to(jnp.sum(qt * kvn, axis=-1, keepdims=True), tile)
            s_self = pltpu.roll(s_self, half, axis=0)
            m = jnp.maximum(jnp.max(s, axis=0), s_self)
            p = jnp.exp(s - m[None])
            p_self = jnp.exp(s_self - m)
            l = jnp.sum(p, axis=0) + p_self
            acc = jnp.sum(x * p, axis=0) + p_self * kvn
            outs.append(acc / l)
            lses.append(m + jnp.log(l))
        mx = jnp.maximum(jnp.maximum(lses[0], lses[1]), lses[2])
        es = [jnp.exp(ls - mx) for ls in lses]
        o_ref[b] = (es[0] * outs[0] + es[1] * outs[1] + es[2] * outs[2]) / (es[0] + es[1] + es[2])
        return carry

    lax.fori_loop(0, SAMPLE_BLOCK, body, 0)


def _sample_attn_call(qts, kvs, caches):
    views = [c.reshape(DEC_BATCH, KEYS_PER_BLOCK, GROUPS[g][1], KV_TILE_ROWS, HEAD_DIM) for g, c in enumerate(caches)]
    tile3 = lambda a: a.reshape(DEC_BATCH, KV_TILE_ROWS, HEAD_DIM)
    tile_blk = pl.BlockSpec((SAMPLE_BLOCK, KV_TILE_ROWS, HEAD_DIM), lambda i: (i, 0, 0))
    cache_blk = pl.BlockSpec((SAMPLE_BLOCK, KEYS_PER_BLOCK, None, KV_TILE_ROWS, HEAD_DIM), lambda i: (i, 0, 0, 0, 0))
    return pl.pallas_call(
        _sample_attn_kernel,
        grid=(DEC_BATCH // SAMPLE_BLOCK,),
        in_specs=[tile_blk] * 6 + [cache_blk] * 3,
        out_specs=tile_blk,
        out_shape=jax.ShapeDtypeStruct((DEC_BATCH, KV_TILE_ROWS, HEAD_DIM), F32),
        scratch_shapes=[pltpu.VMEM((N_GROUPS, KEYS_PER_BLOCK, KV_TILE_ROWS, HEAD_DIM), F32)],
        compiler_params=_params(("arbitrary",)),
        name="sample_attn",
    )(*[tile3(a) for a in qts], *[tile3(a) for a in kvs], *views)


def _sample_tail_kernel(x_ref, mod_ref, o_ref, zr_ref, st_ref, cw_ref, cb_ref, cg_ref, cbeta_ref,
                        wpa_ref, wpb_ref, wo_ref, lng_ref, lnb_ref, y_ref, convs_ref):
    gate = mod_ref[:, 2 * D_MODEL:3 * D_MODEL]
    z = lambda col0, width: zr_ref[:, col0 - QKV_COLS:col0 - QKV_COLS + width]
    u = z(COL_GLU_A, CONV_CH) * _sigmoid(z(COL_GLU_G, CONV_CH))
    hist = CONV_WIDTH - 1
    acc = cb_ref[...] + u * cw_ref[hist:hist + 1, :]
    for j in range(hist):
        acc = acc + st_ref[j] * cw_ref[j:j + 1, :]
    conv_out = _silu(_layer_norm(acc, cg_ref[...], cbeta_ref[...]))
    for j in range(hist - 1):
        convs_ref[j] = st_ref[j + 1]
    convs_ref[hist - 1] = u
    o_attn = jnp.concatenate(
        [o_ref[pl.ds(HEADS_PER_GROUP + hh, DEC_BATCH, stride=KV_TILE_ROWS), :] for hh in range(HEADS_PER_GROUP)],
        axis=-1)
    a = jnp.dot((o_attn * _silu(z(COL_GA, GROUP_COLS))).astype(BF16), wpa_ref[...], preferred_element_type=F32)
    b = jnp.dot((conv_out * _silu(z(COL_GB, CONV_CH))).astype(BF16), wpb_ref[...], preferred_element_type=F32)
    pre = (_sigmoid(z(COL_MA, D_MODEL)) * a + _sigmoid(z(COL_MB, D_MODEL)) * b).astype(BF16)
    y = jnp.dot(pre, wo_ref[...], preferred_element_type=F32)
    y_ref[...] = _layer_norm(ALPHA * x_ref[...] + gate * y, lng_ref[...], lnb_ref[...])


def _sample_tail_call(x_s, mod_s, o_s, zr, state_t, conv_w, conv_b, cn_g, cn_b, w_pa, w_pb, w_o, ln_g, ln_b):
    full = lambda shape: pl.BlockSpec(shape, lambda i: (0,) * len(shape))
    args = (x_s, mod_s, o_s, zr, state_t, conv_w, conv_b, cn_g, cn_b, w_pa, w_pb, w_o, ln_g, ln_b)
    return pl.pallas_call(
        _sample_tail_kernel,
        grid=(1,),
        in_specs=[full(a.shape) for a in args],
        out_specs=[full((DEC_BATCH, D_MODEL)), full(state_t.shape)],
        out_shape=[jax.ShapeDtypeStruct((DEC_BATCH, D_MODEL), F32),
                   jax.ShapeDtypeStruct(state_t.shape, F32)],
        compiler_params=_params(("arbitrary",)),
        name="sample_tail",
    )(*args)


def kernel(x_prompt, x_sample, c_prompt, c_sample, cache_kv_w128, cache_kv_w512, cache_kv_w2048, state_conv,
           w_c, b_c, w_in, b_in, conv_w, conv_b, conv_norm_g, conv_norm_b, w_pa, w_pb, w_o, ln_g, ln_b):
    assert x_prompt.shape == (1, SEQ, D_MODEL) and x_sample.shape == (DEC_BATCH, 1, D_MODEL)
    assert w_in.shape == (D_MODEL, IN_COLS)
    caches = (cache_kv_w128, cache_kv_w512, cache_kv_w2048)
    for (window, _), c in zip(GROUPS, caches):
        assert c.shape == (DEC_BATCH, window, 2, HEADS_PER_GROUP, HEAD_DIM)

    w_bf = w_in.astype(BF16)
    w_pa_b, w_pb_b, w_o_b = w_pa.astype(BF16), w_pb.astype(BF16), w_o.astype(BF16)
    vec = lambda a: a.reshape(1, -1)
    b_row = vec(b_in)
    conv_args = (conv_w, vec(conv_b), vec(conv_norm_g), vec(conv_norm_b))
    out_args = (w_pa_b, w_pb_b, w_o_b, vec(ln_g), vec(ln_b))

    mod_p, mod_s = _mod_call(c_prompt, c_sample, w_c, b_c)

    x_p = x_prompt.reshape(SEQ, D_MODEL)
    (q0, k0, v0, q1, k1, v1, q2, k2, v2, kv0, kv1, kv2, b_conv, conv_p) = _front_call(
        x_p, mod_p, w_bf, b_row, *conv_args)
    flat = lambda a: a.reshape(SEQ, GROUP_COLS)
    attn = [_attn_call(0, q0, k0, v0), _attn_call(1, q1, k1, v1), _attn_call(2, flat(q2), flat(k2), flat(v2))]
    y_p = _tail_call(x_p, mod_p, [a[0] for a in attn], [a[1] for a in attn], b_conv, w_bf, b_row, *out_args)

    x_s = x_sample.reshape(DEC_BATCH, D_MODEL)
    qt0, qt1, qt2, kvs0, kvs1, kvs2, zr = _sample_proj_call(x_s, mod_s, w_bf, b_row)
    o_s = _sample_attn_call((qt0, qt1, qt2), (kvs0, kvs1, kvs2), caches)
    o_s = o_s.reshape(DEC_BATCH * KV_TILE_ROWS, HEAD_DIM)
    state_t = jnp.transpose(state_conv, (1, 0, 2))
    y_s, conv_s_t = _sample_tail_call(x_s, mod_s, o_s, zr, state_t, *conv_args, *out_args)

    kv_shape_p = lambda keep: (1, keep, 2, HEADS_PER_GROUP, HEAD_DIM)
    kv_shape_s = (DEC_BATCH, 1, 2, HEADS_PER_GROUP, HEAD_DIM)
    return (y_p.reshape(1, SEQ, D_MODEL),
            y_s.reshape(DEC_BATCH, 1, D_MODEL),
            kv0.reshape(kv_shape_p(GROUPS[0][0])),
            kv1.reshape(kv_shape_p(GROUPS[1][0])),
            kv2.reshape(kv_shape_p(GROUPS[2][0])),
            conv_p.reshape(1, CONV_WIDTH - 1, CONV_CH),
            kvs0.reshape(kv_shape_s), kvs1.reshape(kv_shape_s), kvs2.reshape(kv_shape_s),
            jnp.transpose(conv_s_t, (1, 0, 2)))
```

```python
import functools

import jax
import jax.numpy as jnp
from jax import lax
from jax.experimental import pallas as pl
from jax.experimental.pallas import tpu as pltpu

F32 = jnp.float32
BF16 = jnp.bfloat16

D_MODEL = 1024
SEQ = 16384
DEC_BATCH = 128
DEPTH = 1
HEAD_DIM = 128
HEADS_PER_GROUP = 4
GROUPS = ((128, 1), (512, 4), (2048, 16))
N_GROUPS = len(GROUPS)
N_HEADS = N_GROUPS * HEADS_PER_GROUP
GROUP_COLS = HEADS_PER_GROUP * HEAD_DIM
ATTN_QKV = N_HEADS * HEAD_DIM
QKV_COLS = 3 * ATTN_QKV
CONV_CH = D_MODEL // 2
CONV_WIDTH = 31
COL_GA = QKV_COLS
COL_GLU_A = COL_GA + GROUP_COLS
COL_GLU_G = COL_GLU_A + CONV_CH
COL_GB = COL_GLU_G + CONV_CH
COL_MA = COL_GB + CONV_CH
COL_MB = COL_MA + D_MODEL
IN_COLS = COL_MB + D_MODEL
ALPHA = (2.0 * DEPTH) ** 0.25
LN_EPS = 1e-5
NEG = -1e30
Q_SCALE = HEAD_DIM ** -0.5
LOG2_E = 1.4426950408889634
KEYS_PER_BLOCK = 128
KV_TILE_ROWS = 2 * HEADS_PER_GROUP

LANES = 128
MXU_COLS = 256
W_BLOCK = 512
ROW_TILE = 512
ATTN_STEP_ROWS = 2048
BLOCKS_PER_ATTN_STEP = ATTN_STEP_ROWS // KEYS_PER_BLOCK
CONV_ROW_CHUNK = 128
HIST_ROWS = 32
SAMPLE_BLOCK = 8
SAMPLE_UNROLL = 4
VMEM_LIMIT_BYTES = 56 * 1024 * 1024


def _slope(group, head):
    return 2.0 ** (-8.0 * (group * HEADS_PER_GROUP + head + 1) / N_HEADS)


def _sigmoid(x):
    return 1.0 / (1.0 + jnp.exp(-x))


def _silu(x):
    return x * _sigmoid(x)


def _layer_norm(x, g, b):
    mu = jnp.mean(x, axis=-1, keepdims=True)
    xc = x - mu
    var = jnp.mean(xc * xc, axis=-1, keepdims=True)
    return xc * lax.rsqrt(var + LN_EPS) * g + b


def _col_blocks(width):
    return [slice(b * MXU_COLS, (b + 1) * MXU_COLS) for b in range(width // MXU_COLS)]


def _row_stats(ref, width):
    total = sum(jnp.sum(ref[:, cols], axis=-1, keepdims=True) for cols in _col_blocks(width))
    mu = total / width
    sq = sum(jnp.sum(jnp.square(ref[:, cols] - mu), axis=-1, keepdims=True) for cols in _col_blocks(width))
    return mu, lax.rsqrt(sq / width + LN_EPS)


def _resident(shape):
    return pl.BlockSpec(shape, lambda *_: (0,) * len(shape), pipeline_mode=pl.Buffered(1))


def _w_cols(col0, width):
    assert col0 % width == 0
    return pl.BlockSpec((D_MODEL, width), lambda *_: (0, col0 // width), pipeline_mode=pl.Buffered(1))


def _row_vec(n):
    return pl.BlockSpec((1, n), lambda *_: (0, 0))


def _params(semantics):
    return pltpu.CompilerParams(dimension_semantics=semantics, vmem_limit_bytes=VMEM_LIMIT_BYTES)


def _store_kv_tiles(ref, k, v):
    rows = k.shape[0]
    for hh in range(HEADS_PER_GROUP):
        cols = slice(hh * HEAD_DIM, (hh + 1) * HEAD_DIM)
        ref[pl.ds(hh, rows, stride=KV_TILE_ROWS), :] = k[:, cols]
        ref[pl.ds(HEADS_PER_GROUP + hh, rows, stride=KV_TILE_ROWS), :] = v[:, cols]


def _mod_kernel(cp_ref, cs_ref, w_ref, b_ref, mp_ref, ms_ref):
    w = w_ref[...].astype(BF16)
    cp = jnp.broadcast_to(cp_ref[...], (8, D_MODEL)).astype(BF16)
    mp_ref[...] = jnp.dot(cp, w, preferred_element_type=F32)[0:1] + b_ref[...]
    ms_ref[...] = jnp.dot(cs_ref[...].astype(BF16), w, preferred_element_type=F32) + b_ref[...]


def _mod_call(c_prompt, c_sample, w_c, b_c):
    n_blk = 3
    return pl.pallas_call(
        _mod_kernel,
        grid=(n_blk,),
        in_specs=[
            pl.BlockSpec((1, D_MODEL), lambda n: (0, 0)),
            pl.BlockSpec((DEC_BATCH, D_MODEL), lambda n: (0, 0)),
            pl.BlockSpec((D_MODEL, D_MODEL), lambda n: (0, n)),
            pl.BlockSpec((1, D_MODEL), lambda n: (0, n)),
        ],
        out_specs=[
            pl.BlockSpec((1, D_MODEL), lambda n: (0, n)),
            pl.BlockSpec((DEC_BATCH, D_MODEL), lambda n: (0, n)),
        ],
        out_shape=[
            jax.ShapeDtypeStruct((1, 3 * D_MODEL), F32),
            jax.ShapeDtypeStruct((DEC_BATCH, 3 * D_MODEL), F32),
        ],
        compiler_params=_params(("arbitrary",)),
        name="adaln_mod",
    )(c_prompt, c_sample, w_c, b_c.reshape(1, 3 * D_MODEL))


N_ROW_TILES = SEQ // ROW_TILE
KV2_TILES = GROUPS[2][0] // ROW_TILE


def _front_kernel(x_ref, mod_ref, w_ref, wa_ref, wg_ref, wgb_ref, b_ref, cw_ref, cb_ref, cg_ref, cbeta_ref,
                  q0_ref, k0_ref, v0_ref, q1_ref, k1_ref, v1_ref, q2_ref, k2_ref, v2_ref,
                  kv0_ref, kv1_ref, kv2_ref, bin_ref, convp_ref,
                  hs_ref, hb_ref, hp4_ref, hp16_ref, uext_ref, cacc_ref):
    i = pl.program_id(0)
    shift = mod_ref[:, 0:D_MODEL]
    scale = mod_ref[:, D_MODEL:2 * D_MODEL]
    h = x_ref[...] * (1.0 + scale) + shift
    hb_ref[...] = h.astype(BF16)

    n_slab = D_MODEL // LANES
    for c in range(n_slab):
        hs_ref[c] = h[:, c * LANES:(c + 1) * LANES]
    d1, d2 = GROUPS[1][1], GROUPS[2][1]
    r1, r2 = ROW_TILE // d1, ROW_TILE // d2
    for c in range(n_slab):
        cols = slice(c * LANES, (c + 1) * LANES)
        for r in range(d1):
            hp4_ref[r * r1:(r + 1) * r1, cols] = hs_ref[c, pl.ds(r, r1, stride=d1), :].astype(BF16)
        for r in range(d2):
            hp16_ref[r * r2:(r + 1) * r2, cols] = hs_ref[c, pl.ds(r, r2, stride=d2), :].astype(BF16)

    def proj(lhs_ref, blk):
        cols = slice(blk * GROUP_COLS, (blk + 1) * GROUP_COLS)
        return jnp.dot(lhs_ref[...], w_ref[:, cols], preferred_element_type=F32) + b_ref[:, cols]

    def gate_proj(wx_ref, col0, cols):
        bcols = slice(col0 + cols.start, col0 + cols.stop)
        return jnp.dot(hb_ref[...], wx_ref[:, cols], preferred_element_type=F32) + b_ref[:, bcols]

    @pl.when(i == 0)
    def _():
        uext_ref[:, 0:HIST_ROWS, :] = jnp.zeros((CONV_CH // LANES, HIST_ROWS, LANES), F32)

    slabs_per_blk = MXU_COLS // LANES
    for b, cols in enumerate(_col_blocks(CONV_CH)):
        u = gate_proj(wa_ref, COL_GLU_A, cols) * _sigmoid(gate_proj(wg_ref, COL_GLU_G, cols))
        convp_ref[:, cols] = u[ROW_TILE - (CONV_WIDTH - 1):, :]
        for s in range(slabs_per_blk):
            uext_ref[b * slabs_per_blk + s, HIST_ROWS:, :] = u[:, s * LANES:(s + 1) * LANES]

    lead = HIST_ROWS - (CONV_WIDTH - 1)
    for c in range(CONV_CH // LANES):
        lanes = slice(c * LANES, (c + 1) * LANES)
        for r0 in range(0, ROW_TILE, CONV_ROW_CHUNK):
            acc = jnp.broadcast_to(cb_ref[:, lanes], (CONV_ROW_CHUNK, LANES))
            for j in range(CONV_WIDTH):
                acc = acc + uext_ref[c, pl.ds(r0 + lead + j, CONV_ROW_CHUNK), :] * cw_ref[j:j + 1, lanes]
            cacc_ref[r0:r0 + CONV_ROW_CHUNK, lanes] = acc
        uext_ref[c, 0:HIST_ROWS, :] = uext_ref[c, ROW_TILE:ROW_TILE + HIST_ROWS, :]

    mu, rstd = _row_stats(cacc_ref, CONV_CH)
    for cols in _col_blocks(CONV_CH):
        normed = (cacc_ref[:, cols] - mu) * rstd * cg_ref[:, cols] + cbeta_ref[:, cols]
        bin_ref[:, cols] = (_silu(normed) * _silu(gate_proj(wgb_ref, COL_GB, cols))).astype(BF16)

    def store_perm16(ref, val):
        val = val.astype(BF16)
        for r in range(d2):
            ref[0, r] = val[r * r2:(r + 1) * r2, :]

    q0_ref[...] = (proj(hb_ref, 0) * Q_SCALE).astype(BF16)
    k0 = proj(hb_ref, 3)
    v0 = proj(hb_ref, 6)
    k0_ref[...] = k0.astype(BF16)
    v0_ref[...] = v0.astype(BF16)
    keep0 = GROUPS[0][0]
    _store_kv_tiles(kv0_ref, k0[ROW_TILE - keep0:, :], v0[ROW_TILE - keep0:, :])
    q1_ref[...] = (proj(hp4_ref, 1) * Q_SCALE).astype(BF16)
    k1_ref[...] = proj(hp4_ref, 4).astype(BF16)
    v1_ref[...] = proj(hp4_ref, 7).astype(BF16)
    store_perm16(q2_ref, proj(hp16_ref, 2) * Q_SCALE)
    store_perm16(k2_ref, proj(hp16_ref, 5))
    store_perm16(v2_ref, proj(hp16_ref, 8))

    @pl.when(i == N_ROW_TILES - 1)
    def _():
        _store_kv_tiles(kv1_ref, proj(hb_ref, 4), proj(hb_ref, 7))

    @pl.when(i >= N_ROW_TILES - KV2_TILES)
    def _():
        _store_kv_tiles(kv2_ref, proj(hb_ref, 5), proj(hb_ref, 8))


def _front_call(x, mod_p, w_bf, b_in, conv_w, conv_b, cn_g, cn_b):
    d2 = GROUPS[2][1]
    spans2 = SEQ // GROUPS[2][0]
    tiles_per_span2 = GROUPS[2][0] // ROW_TILE
    nat = pl.BlockSpec((ROW_TILE, GROUP_COLS), lambda i: (i, 0))
    perm16 = pl.BlockSpec((1, d2, ROW_TILE // d2, GROUP_COLS),
                          lambda i: (i // tiles_per_span2, 0, i % tiles_per_span2, 0))
    nat_shape = jax.ShapeDtypeStruct((SEQ, GROUP_COLS), BF16)
    p16_shape = jax.ShapeDtypeStruct((spans2, d2, KEYS_PER_BLOCK, GROUP_COLS), BF16)
    first_kv2_tile = N_ROW_TILES - KV2_TILES
    kv_shape = lambda g: jax.ShapeDtypeStruct((GROUPS[g][0] * KV_TILE_ROWS, HEAD_DIM), F32)
    return pl.pallas_call(
        _front_kernel,
        grid=(N_ROW_TILES,),
        in_specs=[
            pl.BlockSpec((ROW_TILE, D_MODEL), lambda i: (i, 0)),
            _row_vec(3 * D_MODEL),
            _w_cols(0, QKV_COLS), _w_cols(COL_GLU_A, W_BLOCK), _w_cols(COL_GLU_G, W_BLOCK), _w_cols(COL_GB, W_BLOCK),
            _row_vec(IN_COLS),
            pl.BlockSpec((CONV_WIDTH, CONV_CH), lambda i: (0, 0)),
            _row_vec(CONV_CH), _row_vec(CONV_CH), _row_vec(CONV_CH),
        ],
        out_specs=[nat, nat, nat, nat, nat, nat, perm16, perm16, perm16,
                   pl.BlockSpec((GROUPS[0][0] * KV_TILE_ROWS, HEAD_DIM), lambda i: (0, 0)),
                   pl.BlockSpec((GROUPS[1][0] * KV_TILE_ROWS, HEAD_DIM), lambda i: (0, 0)),
                   pl.BlockSpec((ROW_TILE * KV_TILE_ROWS, HEAD_DIM),
                                lambda i: (jnp.maximum(i - first_kv2_tile, 0), 0)),
                   nat,
                   pl.BlockSpec((CONV_WIDTH - 1, CONV_CH), lambda i: (0, 0))],
        out_shape=[nat_shape] * 6 + [p16_shape] * 3 + [kv_shape(0), kv_shape(1), kv_shape(2), nat_shape,
                                                        jax.ShapeDtypeStruct((CONV_WIDTH - 1, CONV_CH), F32)],
        scratch_shapes=[
            pltpu.VMEM((D_MODEL // LANES, ROW_TILE, LANES), F32),
            pltpu.VMEM((ROW_TILE, D_MODEL), BF16),
            pltpu.VMEM((ROW_TILE, D_MODEL), BF16),
            pltpu.VMEM((ROW_TILE, D_MODEL), BF16),
            pltpu.VMEM((CONV_CH // LANES, HIST_ROWS + ROW_TILE, LANES), F32),
            pltpu.VMEM((ROW_TILE, CONV_CH), F32),
        ],
        compiler_params=_params(("arbitrary",)),
        name="prompt_front",
    )(x, mod_p, w_bf, w_bf, w_bf, w_bf, b_in, conv_w, conv_b, cn_g, cn_b)


def _attn_kernel(q_ref, k_ref, v_ref, kp_ref, vp_ref, o_ref, lse_ref, bias_ref, *scr, group):
    dil = GROUPS[group][1]
    tail = KEYS_PER_BLOCK * dil
    t = pl.program_id(0)

    @pl.when(t == 0)
    def _():
        row = lax.broadcasted_iota(jnp.int32, (KEYS_PER_BLOCK, 2 * KEYS_PER_BLOCK), 0)
        col = lax.broadcasted_iota(jnp.int32, (KEYS_PER_BLOCK, 2 * KEYS_PER_BLOCK), 1)
        dist = row - col + KEYS_PER_BLOCK
        valid = (dist >= 0) & (dist <= KEYS_PER_BLOCK)
        valid_first = valid & (col >= KEYS_PER_BLOCK)
        distf = (dil * dist).astype(F32)
        for hh in range(HEADS_PER_GROUP):
            b = -(_slope(group, hh) * distf)
            bias_ref[0, hh] = jnp.where(valid, b, NEG)
            bias_ref[1, hh] = jnp.where(valid_first, b, NEG)

    first_step = (t == 0).astype(jnp.int32)
    head_lane = lax.broadcasted_iota(jnp.int32, (KEYS_PER_BLOCK, LANES), 1)
    if dil > 1:
        oscr_ref, lscr_ref = scr
    for j in range(BLOCKS_PER_ATTN_STEP):
        blk = slice(j * KEYS_PER_BLOCK, (j + 1) * KEYS_PER_BLOCK)
        qb = q_ref[blk, :]
        kself, vself = k_ref[blk, :], v_ref[blk, :]
        if j < dil:
            kprev, vprev = kp_ref[blk, :], vp_ref[blk, :]
        else:
            pblk = slice((j - dil) * KEYS_PER_BLOCK, (j - dil + 1) * KEYS_PER_BLOCK)
            kprev, vprev = k_ref[pblk, :], v_ref[pblk, :]
        first = first_step if j < dil else 0
        start = (j // dil) * tail + (j % dil)
        rows = pl.ds(start, KEYS_PER_BLOCK, stride=dil) if dil > 1 else pl.ds(start, KEYS_PER_BLOCK)
        lse_tile = jnp.zeros((KEYS_PER_BLOCK, LANES), F32)
        for hh in range(HEADS_PER_GROUP):
            cols = slice(hh * HEAD_DIM, (hh + 1) * HEAD_DIM)
            kc = jnp.concatenate([kprev[:, cols], kself[:, cols]], axis=0)
            vc = jnp.concatenate([vprev[:, cols], vself[:, cols]], axis=0)
            s = lax.dot_general(qb[:, cols], kc, (((1,), (1,)), ((), ())), preferred_element_type=F32)
            s = s + bias_ref[first, hh]
            m = jnp.max(s, axis=-1, keepdims=True)
            p = jnp.exp(s - m)
            l = jnp.sum(p, axis=-1, keepdims=True)
            acc = jnp.dot(p.astype(BF16), vc, preferred_element_type=F32)
            if dil > 1:
                oscr_ref[hh, rows, :] = acc / l
            else:
                o_ref[rows, cols] = (acc / l).astype(BF16)
            lse_tile = jnp.where(head_lane == hh, m + jnp.log(l), lse_tile)
        if dil > 1:
            lscr_ref[rows, :] = lse_tile
        else:
            lse_ref[rows, :] = lse_tile

    if dil > 1:
        for hh in range(HEADS_PER_GROUP):
            o_ref[:, hh * HEAD_DIM:(hh + 1) * HEAD_DIM] = oscr_ref[hh].astype(BF16)
        lse_ref[...] = lscr_ref[...]


def _attn_call(group, q, k, v):
    dil = GROUPS[group][1]
    tail = KEYS_PER_BLOCK * dil
    steps = SEQ // ATTN_STEP_ROWS
    tails_per_step = ATTN_STEP_ROWS // tail
    cur = pl.BlockSpec((ATTN_STEP_ROWS, GROUP_COLS), lambda t: (t, 0))
    prev = pl.BlockSpec((tail, GROUP_COLS), lambda t: (jnp.maximum(t * tails_per_step - 1, 0), 0))
    lse_blk = pl.BlockSpec((ATTN_STEP_ROWS, LANES), lambda t: (t, 0))
    scratch = [pltpu.VMEM((2, HEADS_PER_GROUP, KEYS_PER_BLOCK, 2 * KEYS_PER_BLOCK), F32)]
    if dil > 1:
        scratch += [pltpu.VMEM((HEADS_PER_GROUP, ATTN_STEP_ROWS, HEAD_DIM), F32), pltpu.VMEM((ATTN_STEP_ROWS, LANES), F32)]
    return pl.pallas_call(
        functools.partial(_attn_kernel, group=group),
        grid=(steps,),
        in_specs=[cur, cur, cur, prev, prev],
        out_specs=[cur, lse_blk],
        out_shape=[jax.ShapeDtypeStruct((SEQ, GROUP_COLS), BF16), jax.ShapeDtypeStruct((SEQ, LANES), F32)],
        scratch_shapes=scratch,
        compiler_params=_params(("arbitrary",)),
        name=f"prompt_attn_g{group}",
    )(q, k, v, k, v)


def _tail_kernel(x_ref, mod_ref, o0_ref, o1_ref, o2_ref, l0_ref, l1_ref, l2_ref, bin_ref,
                 wga_ref, wma0_ref, wma1_ref, wmb0_ref, wmb1_ref, b_ref, wpa_ref, wpb_ref, wo_ref, lng_ref, lnb_ref,
                 y_ref,
                 hb_ref, ain_ref, pre_ref, res_ref):
    shift = mod_ref[:, 0:D_MODEL]
    scale = mod_ref[:, D_MODEL:2 * D_MODEL]
    hb_ref[...] = (x_ref[...] * (1.0 + scale) + shift).astype(BF16)

    def gate_proj(wx_refs, col0, cols):
        wx_ref = wx_refs[cols.start // W_BLOCK]
        wcols = slice(cols.start % W_BLOCK, cols.start % W_BLOCK + MXU_COLS)
        bcols = slice(col0 + cols.start, col0 + cols.stop)
        return jnp.dot(hb_ref[...], wx_ref[:, wcols], preferred_element_type=F32) + b_ref[:, bcols]

    l0, l1, l2 = l0_ref[...], l1_ref[...], l2_ref[...]
    m = jnp.maximum(jnp.maximum(l0, l1), l2)
    e0, e1, e2 = jnp.exp(l0 - m), jnp.exp(l1 - m), jnp.exp(l2 - m)
    inv = 1.0 / (e0 + e1 + e2)
    weights = (e0 * inv, e1 * inv, e2 * inv)
    o_refs = (o0_ref, o1_ref, o2_ref)
    for cols in _col_blocks(GROUP_COLS):
        heads = []
        for hh in range(cols.start // HEAD_DIM, cols.stop // HEAD_DIM):
            hcols = slice(hh * HEAD_DIM, (hh + 1) * HEAD_DIM)
            heads.append(sum(w[:, hh:hh + 1] * o_ref[:, hcols].astype(F32) for w, o_ref in zip(weights, o_refs)))
        ga = gate_proj((wga_ref,), COL_GA, cols)
        ain_ref[:, cols] = (jnp.concatenate(heads, axis=-1) * _silu(ga)).astype(BF16)

    for cols in _col_blocks(D_MODEL):
        a = jnp.dot(ain_ref[...], wpa_ref[:, cols], preferred_element_type=F32)
        bb = jnp.dot(bin_ref[...], wpb_ref[:, cols], preferred_element_type=F32)
        ma = gate_proj((wma0_ref, wma1_ref), COL_MA, cols)
        mb = gate_proj((wmb0_ref, wmb1_ref), COL_MB, cols)
        pre_ref[:, cols] = (_sigmoid(ma) * a + _sigmoid(mb) * bb).astype(BF16)
    for cols in _col_blocks(D_MODEL):
        gate = mod_ref[:, 2 * D_MODEL + cols.start:2 * D_MODEL + cols.stop]
        y = jnp.dot(pre_ref[...], wo_ref[:, cols], preferred_element_type=F32)
        res_ref[:, cols] = ALPHA * x_ref[:, cols] + gate * y
    mu, rstd = _row_stats(res_ref, D_MODEL)
    for cols in _col_blocks(D_MODEL):
        y_ref[:, cols] = (res_ref[:, cols] - mu) * rstd * lng_ref[:, cols] + lnb_ref[:, cols]


def _tail_call(x, mod_p, outs, lses, b_conv, w_bf, b_in, w_pa, w_pb, w_o, ln_g, ln_b):
    o_blk = pl.BlockSpec((ROW_TILE, GROUP_COLS), lambda i: (i, 0))
    lse_blk = pl.BlockSpec((ROW_TILE, LANES), lambda i: (i, 0))
    return pl.pallas_call(
        _tail_kernel,
        grid=(N_ROW_TILES,),
        in_specs=[
            pl.BlockSpec((ROW_TILE, D_MODEL), lambda i: (i, 0)),
            _row_vec(3 * D_MODEL),
            o_blk, o_blk, o_blk, lse_blk, lse_blk, lse_blk,
            pl.BlockSpec((ROW_TILE, CONV_CH), lambda i: (i, 0)),
            _w_cols(COL_GA, W_BLOCK),
            _w_cols(COL_MA, W_BLOCK), _w_cols(COL_MA + W_BLOCK, W_BLOCK),
            _w_cols(COL_MB, W_BLOCK), _w_cols(COL_MB + W_BLOCK, W_BLOCK),
            _row_vec(IN_COLS),
            _resident((GROUP_COLS, D_MODEL)),
            _resident((CONV_CH, D_MODEL)),
            _resident((D_MODEL, D_MODEL)),
            _row_vec(D_MODEL), _row_vec(D_MODEL),
        ],
        out_specs=pl.BlockSpec((ROW_TILE, D_MODEL), lambda i: (i, 0)),
        out_shape=jax.ShapeDtypeStruct((SEQ, D_MODEL), F32),
        scratch_shapes=[
            pltpu.VMEM((ROW_TILE, D_MODEL), BF16),
            pltpu.VMEM((ROW_TILE, GROUP_COLS), BF16),
            pltpu.VMEM((ROW_TILE, D_MODEL), BF16),
            pltpu.VMEM((ROW_TILE, D_MODEL), F32),
        ],
        compiler_params=_params(("arbitrary",)),
        name="prompt_tail",
    )(x, mod_p, *outs, *lses, b_conv, w_bf, w_bf, w_bf, w_bf, w_bf, b_in, w_pa, w_pb, w_o, ln_g, ln_b)


N_REST_BLOCKS = (IN_COLS - QKV_COLS) // W_BLOCK


def _sample_proj_kernel(x_ref, mod_ref, wq_ref, *rest):
    wr_refs = rest[:N_REST_BLOCKS]
    b_ref, qt0_ref, qt1_ref, qt2_ref, kv0_ref, kv1_ref, kv2_ref, zr_ref = rest[N_REST_BLOCKS:]
    shift = mod_ref[:, 0:D_MODEL]
    scale = mod_ref[:, D_MODEL:2 * D_MODEL]
    hb = (x_ref[...] * (1.0 + scale) + shift).astype(BF16)

    def proj(blk):
        cols = slice(blk * GROUP_COLS, (blk + 1) * GROUP_COLS)
        return jnp.dot(hb, wq_ref[:, cols], preferred_element_type=F32) + b_ref[:, cols]

    zeros = jnp.zeros((DEC_BATCH, GROUP_COLS), F32)
    for g, (qt_ref, kv_ref) in enumerate(((qt0_ref, kv0_ref), (qt1_ref, kv1_ref), (qt2_ref, kv2_ref))):
        _store_kv_tiles(qt_ref, proj(g) * Q_SCALE, zeros)
        _store_kv_tiles(kv_ref, proj(N_GROUPS + g), proj(2 * N_GROUPS + g))
    for blk, wr_ref in enumerate(wr_refs):
        cols = slice(blk * W_BLOCK, (blk + 1) * W_BLOCK)
        bcols = slice(QKV_COLS + cols.start, QKV_COLS + cols.stop)
        zr_ref[:, cols] = jnp.dot(hb, wr_ref[...], preferred_element_type=F32) + b_ref[:, bcols]


def _sample_proj_call(x_s, mod_s, w_bf, b_in):
    full = lambda shape: pl.BlockSpec(shape, lambda i: (0,) * len(shape))
    tile_rows = DEC_BATCH * KV_TILE_ROWS
    tiles = jax.ShapeDtypeStruct((tile_rows, HEAD_DIM), F32)
    rest_cols = IN_COLS - QKV_COLS
    return pl.pallas_call(
        _sample_proj_kernel,
        grid=(1,),
        in_specs=[full((DEC_BATCH, D_MODEL)), full((DEC_BATCH, 3 * D_MODEL)), _w_cols(0, QKV_COLS)]
                 + [_w_cols(QKV_COLS + blk * W_BLOCK, W_BLOCK) for blk in range(N_REST_BLOCKS)]
                 + [_row_vec(IN_COLS)],
        out_specs=[full((tile_rows, HEAD_DIM))] * 6 + [full((DEC_BATCH, rest_cols))],
        out_shape=[tiles] * 6 + [jax.ShapeDtypeStruct((DEC_BATCH, rest_cols), F32)],
        compiler_params=_params(("arbitrary",)),
        name="sample_proj",
    )(x_s, mod_s, w_bf, *([w_bf] * N_REST_BLOCKS), b_in)


def _sample_attn_kernel(qt0_ref, qt1_ref, qt2_ref, kv0_ref, kv1_ref, kv2_ref, c0_ref, c1_ref, c2_ref, o_ref,
                        bias_ref):
    qt_refs = (qt0_ref, qt1_ref, qt2_ref)
    kv_refs = (kv0_ref, kv1_ref, kv2_ref)
    c_refs = (c0_ref, c1_ref, c2_ref)
    half = HEADS_PER_GROUP
    tile = (KV_TILE_ROWS, HEAD_DIM)
    keys_tile = (KEYS_PER_BLOCK,) + tile

    @pl.when(pl.program_id(0) == 0)
    def _():
        key = lax.broadcasted_iota(jnp.int32, keys_tile, 0)
        sub = lax.broadcasted_iota(jnp.int32, keys_tile, 1)
        steps_back = (KEYS_PER_BLOCK - key).astype(F32)
        for g in range(N_GROUPS):
            slope_rows = jnp.zeros(keys_tile, F32)
            for hh in range(HEADS_PER_GROUP):
                slope_rows = jnp.where(sub == half + hh, _slope(g, hh), slope_rows)
            bias_ref[g] = -(slope_rows * (GROUPS[g][1] * steps_back)) * LOG2_E

    def body(b, carry):
        outs, lses = [], []
        for g in range(N_GROUPS):
            x = c_refs[g][b]
            qt = qt_refs[g][b] * LOG2_E
            kvn = kv_refs[g][b]
            s = jnp.broadcast_to(jnp.sum(x * qt[None], axis=-1, keepdims=True), keys_tile)
            s = pltpu.roll(s, half, axis=1) + bias_ref[g]
            s_self = jnp.broadcast_to(jnp.sum(qt * kvn, axis=-1, keepdims=True), tile)
            s_self = pltpu.roll(s_self, half, axis=0)
            m = jnp.maximum(jnp.max(s, axis=0), s_self)
            p = jnp.exp2(s - m[None])
            p_self = jnp.exp2(s_self - m)
            l = jnp.sum(p, axis=0) + p_self
            acc = jnp.sum(x * p, axis=0) + p_self * kvn
            outs.append(acc / l)
            lses.append(m + jnp.log2(l))
        mx = jnp.maximum(jnp.maximum(lses[0], lses[1]), lses[2])
        es = [jnp.exp2(ls - mx) for ls in lses]
        o_ref[b] = (es[0] * outs[0] + es[1] * outs[1] + es[2] * outs[2]) / (es[0] + es[1] + es[2])
        return carry

    lax.fori_loop(0, SAMPLE_BLOCK, body, 0, unroll=SAMPLE_UNROLL)


def _sample_attn_call(qts, kvs, caches):
    views = [c.reshape(DEC_BATCH, KEYS_PER_BLOCK, GROUPS[g][1], KV_TILE_ROWS, HEAD_DIM) for g, c in enumerate(caches)]
    tile3 = lambda a: a.reshape(DEC_BATCH, KV_TILE_ROWS, HEAD_DIM)
    tile_blk = pl.BlockSpec((SAMPLE_BLOCK, KV_TILE_ROWS, HEAD_DIM), lambda i: (i, 0, 0))
    cache_blk = pl.BlockSpec((SAMPLE_BLOCK, KEYS_PER_BLOCK, None, KV_TILE_ROWS, HEAD_DIM), lambda i: (i, 0, 0, 0, 0))
    return pl.pallas_call(
        _sample_attn_kernel,
        grid=(DEC_BATCH // SAMPLE_BLOCK,),
        in_specs=[tile_blk] * 6 + [cache_blk] * 3,
        out_specs=tile_blk,
        out_shape=jax.ShapeDtypeStruct((DEC_BATCH, KV_TILE_ROWS, HEAD_DIM), F32),
        scratch_shapes=[pltpu.VMEM((N_GROUPS, KEYS_PER_BLOCK, KV_TILE_ROWS, HEAD_DIM), F32)],
        compiler_params=_params(("arbitrary",)),
        name="sample_attn",
    )(*[tile3(a) for a in qts], *[tile3(a) for a in kvs], *views)


def _sample_tail_kernel(x_ref, mod_ref, o_ref, zr_ref, st_ref, cw_ref, cb_ref, cg_ref, cbeta_ref,
                        wpa_ref, wpb_ref, wo_ref, lng_ref, lnb_ref, y_ref, convs_ref):
    gate = mod_ref[:, 2 * D_MODEL:3 * D_MODEL]
    z = lambda col0, width: zr_ref[:, col0 - QKV_COLS:col0 - QKV_COLS + width]
    u = z(COL_GLU_A, CONV_CH) * _sigmoid(z(COL_GLU_G, CONV_CH))
    hist = CONV_WIDTH - 1
    acc = cb_ref[...] + u * cw_ref[hist:hist + 1, :]
    for j in range(hist):
        acc = acc + st_ref[j] * cw_ref[j:j + 1, :]
    conv_out = _silu(_layer_norm(acc, cg_ref[...], cbeta_ref[...]))
    for j in range(hist - 1):
        convs_ref[j] = st_ref[j + 1]
    convs_ref[hist - 1] = u
    o_attn = jnp.concatenate(
        [o_ref[pl.ds(HEADS_PER_GROUP + hh, DEC_BATCH, stride=KV_TILE_ROWS), :] for hh in range(HEADS_PER_GROUP)],
        axis=-1)
    a = jnp.dot((o_attn * _silu(z(COL_GA, GROUP_COLS))).astype(BF16), wpa_ref[...], preferred_element_type=F32)
    b = jnp.dot((conv_out * _silu(z(COL_GB, CONV_CH))).astype(BF16), wpb_ref[...], preferred_element_type=F32)
    pre = (_sigmoid(z(COL_MA, D_MODEL)) * a + _sigmoid(z(COL_MB, D_MODEL)) * b).astype(BF16)
    y = jnp.dot(pre, wo_ref[...], preferred_element_type=F32)
    y_ref[...] = _layer_norm(ALPHA * x_ref[...] + gate * y, lng_ref[...], lnb_ref[...])


def _sample_tail_call(x_s, mod_s, o_s, zr, state_t, conv_w, conv_b, cn_g, cn_b, w_pa, w_pb, w_o, ln_g, ln_b):
    full = lambda shape: pl.BlockSpec(shape, lambda i: (0,) * len(shape))
    args = (x_s, mod_s, o_s, zr, state_t, conv_w, conv_b, cn_g, cn_b, w_pa, w_pb, w_o, ln_g, ln_b)
    return pl.pallas_call(
        _sample_tail_kernel,
        grid=(1,),
        in_specs=[full(a.shape) for a in args],
        out_specs=[full((DEC_BATCH, D_MODEL)), full(state_t.shape)],
        out_shape=[jax.ShapeDtypeStruct((DEC_BATCH, D_MODEL), F32),
                   jax.ShapeDtypeStruct(state_t.shape, F32)],
        compiler_params=_params(("arbitrary",)),
        name="sample_tail",
    )(*args)


def kernel(x_prompt, x_sample, c_prompt, c_sample, cache_kv_w128, cache_kv_w512, cache_kv_w2048, state_conv,
           w_c, b_c, w_in, b_in, conv_w, conv_b, conv_norm_g, conv_norm_b, w_pa, w_pb, w_o, ln_g, ln_b):
    assert x_prompt.shape == (1, SEQ, D_MODEL) and x_sample.shape == (DEC_BATCH, 1, D_MODEL)
    assert w_in.shape == (D_MODEL, IN_COLS)
    caches = (cache_kv_w128, cache_kv_w512, cache_kv_w2048)
    for (window, _), c in zip(GROUPS, caches):
        assert c.shape == (DEC_BATCH, window, 2, HEADS_PER_GROUP, HEAD_DIM)

    w_bf = w_in.astype(BF16)
    w_pa_b, w_pb_b, w_o_b = w_pa.astype(BF16), w_pb.astype(BF16), w_o.astype(BF16)
    vec = lambda a: a.reshape(1, -1)
    b_row = vec(b_in)
    conv_args = (conv_w, vec(conv_b), vec(conv_norm_g), vec(conv_norm_b))
    out_args = (w_pa_b, w_pb_b, w_o_b, vec(ln_g), vec(ln_b))

    mod_p, mod_s = _mod_call(c_prompt, c_sample, w_c, b_c)

    x_p = x_prompt.reshape(SEQ, D_MODEL)
    (q0, k0, v0, q1, k1, v1, q2, k2, v2, kv0, kv1, kv2, b_conv, conv_p) = _front_call(
        x_p, mod_p, w_bf, b_row, *conv_args)
    flat = lambda a: a.reshape(SEQ, GROUP_COLS)
    attn = [_attn_call(0, q0, k0, v0), _attn_call(1, q1, k1, v1), _attn_call(2, flat(q2), flat(k2), flat(v2))]
    y_p = _tail_call(x_p, mod_p, [a[0] for a in attn], [a[1] for a in attn], b_conv, w_bf, b_row, *out_args)

    x_s = x_sample.reshape(DEC_BATCH, D_MODEL)
    qt0, qt1, qt2, kvs0, kvs1, kvs2, zr = _sample_proj_call(x_s, mod_s, w_bf, b_row)
    o_s = _sample_attn_call((qt0, qt1, qt2), (kvs0, kvs1, kvs2), caches)
    o_s = o_s.reshape(DEC_BATCH * KV_TILE_ROWS, HEAD_DIM)
    state_t = jnp.transpose(state_conv, (1, 0, 2))
    y_s, conv_s_t = _sample_tail_call(x_s, mod_s, o_s, zr, state_t, *conv_args, *out_args)

    kv_shape_p = lambda keep: (1, keep, 2, HEADS_PER_GROUP, HEAD_DIM)
    kv_shape_s = (DEC_BATCH, 1, 2, HEADS_PER_GROUP, HEAD_DIM)
    return (y_p.reshape(1, SEQ, D_MODEL),
            y_s.reshape(DEC_BATCH, 1, D_MODEL),
            kv0.reshape(kv_shape_p(GROUPS[0][0])),
            kv1.reshape(kv_shape_p(GROUPS[1][0])),
            kv2.reshape(kv_shape_p(GROUPS[2][0])),
            conv_p.reshape(1, CONV_WIDTH - 1, CONV_CH),
            kvs0.reshape(kv_shape_s), kvs1.reshape(kv_shape_s), kvs2.reshape(kv_shape_s),
            jnp.transpose(conv_s_t, (1, 0, 2)))
```

```python
import functools

import jax
import jax.numpy as jnp
from jax import lax
from jax.experimental import pallas as pl
from jax.experimental.pallas import tpu as pltpu

F32 = jnp.float32
BF16 = jnp.bfloat16

D_MODEL = 1024
SEQ = 16384
DEC_BATCH = 128
DEPTH = 1
HEAD_DIM = 128
HEADS_PER_GROUP = 4
GROUPS = ((128, 1), (512, 4), (2048, 16))
N_GROUPS = len(GROUPS)
N_HEADS = N_GROUPS * HEADS_PER_GROUP
GROUP_COLS = HEADS_PER_GROUP * HEAD_DIM
ATTN_QKV = N_HEADS * HEAD_DIM
QKV_COLS = 3 * ATTN_QKV
CONV_CH = D_MODEL // 2
CONV_WIDTH = 31
COL_GA = QKV_COLS
COL_GLU_A = COL_GA + GROUP_COLS
COL_GLU_G = COL_GLU_A + CONV_CH
COL_GB = COL_GLU_G + CONV_CH
COL_MA = COL_GB + CONV_CH
COL_MB = COL_MA + D_MODEL
IN_COLS = COL_MB + D_MODEL
ALPHA = (2.0 * DEPTH) ** 0.25
LN_EPS = 1e-5
NEG = -1e30
Q_SCALE = HEAD_DIM ** -0.5
LOG2_E = 1.4426950408889634
Q_SCALE_LOG2 = Q_SCALE * LOG2_E
KEYS_PER_BLOCK = 128
KV_TILE_ROWS = 2 * HEADS_PER_GROUP

LANES = 128
MXU_COLS = 256
W_BLOCK = 512
ROW_TILE = 512
ATTN_STEP_ROWS = 2048
BLOCKS_PER_ATTN_STEP = ATTN_STEP_ROWS // KEYS_PER_BLOCK
CONV_ROW_CHUNK = 128
HIST_ROWS = 32
SAMPLE_BLOCK = 8
SAMPLE_UNROLL = 4
VMEM_LIMIT_BYTES = 56 * 1024 * 1024


def _slope(group, head):
    return 2.0 ** (-8.0 * (group * HEADS_PER_GROUP + head + 1) / N_HEADS)


def _sigmoid(x):
    return 1.0 / (1.0 + jnp.exp(-x))


def _silu(x):
    return x * _sigmoid(x)


def _layer_norm(x, g, b):
    mu = jnp.mean(x, axis=-1, keepdims=True)
    xc = x - mu
    var = jnp.mean(xc * xc, axis=-1, keepdims=True)
    return xc * lax.rsqrt(var + LN_EPS) * g + b


def _col_blocks(width):
    return [slice(b * MXU_COLS, (b + 1) * MXU_COLS) for b in range(width // MXU_COLS)]


def _row_stats(ref, width):
    total = sum(jnp.sum(ref[:, cols], axis=-1, keepdims=True) for cols in _col_blocks(width))
    mu = total / width
    sq = sum(jnp.sum(jnp.square(ref[:, cols] - mu), axis=-1, keepdims=True) for cols in _col_blocks(width))
    return mu, lax.rsqrt(sq / width + LN_EPS)


def _resident(shape):
    return pl.BlockSpec(shape, lambda *_: (0,) * len(shape), pipeline_mode=pl.Buffered(1))


def _w_cols(col0, width):
    assert col0 % width == 0
    return pl.BlockSpec((D_MODEL, width), lambda *_: (0, col0 // width), pipeline_mode=pl.Buffered(1))


def _row_vec(n):
    return pl.BlockSpec((1, n), lambda *_: (0, 0))


def _params(semantics):
    return pltpu.CompilerParams(dimension_semantics=semantics, vmem_limit_bytes=VMEM_LIMIT_BYTES)


def _store_kv_tiles(ref, k, v):
    rows = k.shape[0]
    for hh in range(HEADS_PER_GROUP):
        cols = slice(hh * HEAD_DIM, (hh + 1) * HEAD_DIM)
        ref[pl.ds(hh, rows, stride=KV_TILE_ROWS), :] = k[:, cols]
        ref[pl.ds(HEADS_PER_GROUP + hh, rows, stride=KV_TILE_ROWS), :] = v[:, cols]


def _mod_kernel(cp_ref, cs_ref, w_ref, b_ref, mp_ref, ms_ref):
    w = w_ref[...].astype(BF16)
    cp = jnp.broadcast_to(cp_ref[...], (8, D_MODEL)).astype(BF16)
    mp_ref[...] = jnp.dot(cp, w, preferred_element_type=F32)[0:1] + b_ref[...]
    ms_ref[...] = jnp.dot(cs_ref[...].astype(BF16), w, preferred_element_type=F32) + b_ref[...]


def _mod_call(c_prompt, c_sample, w_c, b_c):
    n_blk = 3
    return pl.pallas_call(
        _mod_kernel,
        grid=(n_blk,),
        in_specs=[
            pl.BlockSpec((1, D_MODEL), lambda n: (0, 0)),
            pl.BlockSpec((DEC_BATCH, D_MODEL), lambda n: (0, 0)),
            pl.BlockSpec((D_MODEL, D_MODEL), lambda n: (0, n)),
            pl.BlockSpec((1, D_MODEL), lambda n: (0, n)),
        ],
        out_specs=[
            pl.BlockSpec((1, D_MODEL), lambda n: (0, n)),
            pl.BlockSpec((DEC_BATCH, D_MODEL), lambda n: (0, n)),
        ],
        out_shape=[
            jax.ShapeDtypeStruct((1, 3 * D_MODEL), F32),
            jax.ShapeDtypeStruct((DEC_BATCH, 3 * D_MODEL), F32),
        ],
        compiler_params=_params(("arbitrary",)),
        name="adaln_mod",
    )(c_prompt, c_sample, w_c, b_c.reshape(1, 3 * D_MODEL))


N_ROW_TILES = SEQ // ROW_TILE
KV2_TILES = GROUPS[2][0] // ROW_TILE


def _front_kernel(x_ref, mod_ref, w_ref, wa_ref, wg_ref, wgb_ref, b_ref, cw_ref, cb_ref, cg_ref, cbeta_ref,
                  q0_ref, k0_ref, v0_ref, q1_ref, k1_ref, v1_ref, q2_ref, k2_ref, v2_ref,
                  kv0_ref, kv1_ref, kv2_ref, bin_ref, convp_ref,
                  hs_ref, hq_ref, hb_ref, hp4_ref, hp16_ref, uext_ref, cacc_ref):
    i = pl.program_id(0)
    shift = mod_ref[:, 0:D_MODEL]
    scale = mod_ref[:, D_MODEL:2 * D_MODEL]
    h = x_ref[...] * (1.0 + scale) + shift
    hb_ref[...] = h.astype(BF16)

    n_slab = D_MODEL // LANES
    d1, d2 = GROUPS[1][1], GROUPS[2][1]
    assert d2 == d1 * d1
    r1, r2 = ROW_TILE // d1, ROW_TILE // d2
    for c in range(n_slab):
        cols = slice(c * LANES, (c + 1) * LANES)
        hs_ref[c] = h[:, cols]
        for b in range(d1):
            rows_b = hs_ref[c, pl.ds(b, r1, stride=d1), :]
            hq_ref[c, b * r1:(b + 1) * r1, :] = rows_b
            hp4_ref[b * r1:(b + 1) * r1, cols] = rows_b.astype(BF16)
        for r in range(d2):
            a, b = r // d1, r % d1
            hp16_ref[r * r2:(r + 1) * r2, cols] = hq_ref[c, pl.ds(b * r1 + a, r2, stride=d1), :].astype(BF16)

    def proj(lhs_ref, blk):
        cols = slice(blk * GROUP_COLS, (blk + 1) * GROUP_COLS)
        return jnp.dot(lhs_ref[...], w_ref[:, cols], preferred_element_type=F32) + b_ref[:, cols]

    def gate_proj(wx_ref, col0, cols):
        bcols = slice(col0 + cols.start, col0 + cols.stop)
        return jnp.dot(hb_ref[...], wx_ref[:, cols], preferred_element_type=F32) + b_ref[:, bcols]

    @pl.when(i == 0)
    def _():
        uext_ref[:, 0:HIST_ROWS, :] = jnp.zeros((CONV_CH // LANES, HIST_ROWS, LANES), F32)

    slabs_per_blk = MXU_COLS // LANES
    for b, cols in enumerate(_col_blocks(CONV_CH)):
        u = gate_proj(wa_ref, COL_GLU_A, cols) * _sigmoid(gate_proj(wg_ref, COL_GLU_G, cols))
        convp_ref[:, cols] = u[ROW_TILE - (CONV_WIDTH - 1):, :]
        for s in range(slabs_per_blk):
            uext_ref[b * slabs_per_blk + s, HIST_ROWS:, :] = u[:, s * LANES:(s + 1) * LANES]

    lead = HIST_ROWS - (CONV_WIDTH - 1)
    for c in range(CONV_CH // LANES):
        lanes = slice(c * LANES, (c + 1) * LANES)
        for r0 in range(0, ROW_TILE, CONV_ROW_CHUNK):
            acc = jnp.broadcast_to(cb_ref[:, lanes], (CONV_ROW_CHUNK, LANES))
            for j in range(CONV_WIDTH):
                acc = acc + uext_ref[c, pl.ds(r0 + lead + j, CONV_ROW_CHUNK), :] * cw_ref[j:j + 1, lanes]
            cacc_ref[r0:r0 + CONV_ROW_CHUNK, lanes] = acc
        uext_ref[c, 0:HIST_ROWS, :] = uext_ref[c, ROW_TILE:ROW_TILE + HIST_ROWS, :]

    mu, rstd = _row_stats(cacc_ref, CONV_CH)
    for cols in _col_blocks(CONV_CH):
        normed = (cacc_ref[:, cols] - mu) * rstd * cg_ref[:, cols] + cbeta_ref[:, cols]
        bin_ref[:, cols] = (_silu(normed) * _silu(gate_proj(wgb_ref, COL_GB, cols))).astype(BF16)

    def store_perm16(ref, val):
        val = val.astype(BF16)
        for r in range(d2):
            ref[0, r] = val[r * r2:(r + 1) * r2, :]

    q0_ref[...] = (proj(hb_ref, 0) * Q_SCALE_LOG2).astype(BF16)
    k0 = proj(hb_ref, 3)
    v0 = proj(hb_ref, 6)
    k0_ref[...] = k0.astype(BF16)
    v0_ref[...] = v0.astype(BF16)
    keep0 = GROUPS[0][0]
    _store_kv_tiles(kv0_ref, k0[ROW_TILE - keep0:, :], v0[ROW_TILE - keep0:, :])
    q1_ref[...] = (proj(hp4_ref, 1) * Q_SCALE_LOG2).astype(BF16)
    k1_ref[...] = proj(hp4_ref, 4).astype(BF16)
    v1_ref[...] = proj(hp4_ref, 7).astype(BF16)
    store_perm16(q2_ref, proj(hp16_ref, 2) * Q_SCALE_LOG2)
    store_perm16(k2_ref, proj(hp16_ref, 5))
    store_perm16(v2_ref, proj(hp16_ref, 8))

    @pl.when(i == N_ROW_TILES - 1)
    def _():
        _store_kv_tiles(kv1_ref, proj(hb_ref, 4), proj(hb_ref, 7))

    @pl.when(i >= N_ROW_TILES - KV2_TILES)
    def _():
        _store_kv_tiles(kv2_ref, proj(hb_ref, 5), proj(hb_ref, 8))


def _front_call(x, mod_p, w_bf, b_in, conv_w, conv_b, cn_g, cn_b):
    d2 = GROUPS[2][1]
    spans2 = SEQ // GROUPS[2][0]
    tiles_per_span2 = GROUPS[2][0] // ROW_TILE
    nat = pl.BlockSpec((ROW_TILE, GROUP_COLS), lambda i: (i, 0))
    perm16 = pl.BlockSpec((1, d2, ROW_TILE // d2, GROUP_COLS),
                          lambda i: (i // tiles_per_span2, 0, i % tiles_per_span2, 0))
    nat_shape = jax.ShapeDtypeStruct((SEQ, GROUP_COLS), BF16)
    p16_shape = jax.ShapeDtypeStruct((spans2, d2, KEYS_PER_BLOCK, GROUP_COLS), BF16)
    first_kv2_tile = N_ROW_TILES - KV2_TILES
    kv_shape = lambda g: jax.ShapeDtypeStruct((GROUPS[g][0] * KV_TILE_ROWS, HEAD_DIM), F32)
    return pl.pallas_call(
        _front_kernel,
        grid=(N_ROW_TILES,),
        in_specs=[
            pl.BlockSpec((ROW_TILE, D_MODEL), lambda i: (i, 0)),
            _row_vec(3 * D_MODEL),
            _w_cols(0, QKV_COLS), _w_cols(COL_GLU_A, W_BLOCK), _w_cols(COL_GLU_G, W_BLOCK), _w_cols(COL_GB, W_BLOCK),
            _row_vec(IN_COLS),
            pl.BlockSpec((CONV_WIDTH, CONV_CH), lambda i: (0, 0)),
            _row_vec(CONV_CH), _row_vec(CONV_CH), _row_vec(CONV_CH),
        ],
        out_specs=[nat, nat, nat, nat, nat, nat, perm16, perm16, perm16,
                   pl.BlockSpec((GROUPS[0][0] * KV_TILE_ROWS, HEAD_DIM), lambda i: (0, 0)),
                   pl.BlockSpec((GROUPS[1][0] * KV_TILE_ROWS, HEAD_DIM), lambda i: (0, 0)),
                   pl.BlockSpec((ROW_TILE * KV_TILE_ROWS, HEAD_DIM),
                                lambda i: (jnp.maximum(i - first_kv2_tile, 0), 0)),
                   nat,
                   pl.BlockSpec((CONV_WIDTH - 1, CONV_CH), lambda i: (0, 0))],
        out_shape=[nat_shape] * 6 + [p16_shape] * 3 + [kv_shape(0), kv_shape(1), kv_shape(2), nat_shape,
                                                        jax.ShapeDtypeStruct((CONV_WIDTH - 1, CONV_CH), F32)],
        scratch_shapes=[
            pltpu.VMEM((D_MODEL // LANES, ROW_TILE, LANES), F32),
            pltpu.VMEM((D_MODEL // LANES, ROW_TILE, LANES), F32),
            pltpu.VMEM((ROW_TILE, D_MODEL), BF16),
            pltpu.VMEM((ROW_TILE, D_MODEL), BF16),
            pltpu.VMEM((ROW_TILE, D_MODEL), BF16),
            pltpu.VMEM((CONV_CH // LANES, HIST_ROWS + ROW_TILE, LANES), F32),
            pltpu.VMEM((ROW_TILE, CONV_CH), F32),
        ],
        compiler_params=_params(("arbitrary",)),
        name="prompt_front",
    )(x, mod_p, w_bf, w_bf, w_bf, w_bf, b_in, conv_w, conv_b, cn_g, cn_b)


def _attn_kernel(q_ref, k_ref, v_ref, kp_ref, vp_ref, o_ref, lse_ref, bias_ref, *scr, group):
    dil = GROUPS[group][1]
    tail = KEYS_PER_BLOCK * dil
    t = pl.program_id(0)

    @pl.when(t == 0)
    def _():
        row = lax.broadcasted_iota(jnp.int32, (KEYS_PER_BLOCK, 2 * KEYS_PER_BLOCK), 0)
        col = lax.broadcasted_iota(jnp.int32, (KEYS_PER_BLOCK, 2 * KEYS_PER_BLOCK), 1)
        dist = row - col + KEYS_PER_BLOCK
        valid = (dist >= 0) & (dist <= KEYS_PER_BLOCK)
        valid_first = valid & (col >= KEYS_PER_BLOCK)
        distf = (dil * dist).astype(F32)
        for hh in range(HEADS_PER_GROUP):
            b = -(_slope(group, hh) * distf) * LOG2_E
            bias_ref[0, hh] = jnp.where(valid, b, NEG)
            bias_ref[1, hh] = jnp.where(valid_first, b, NEG)

    first_step = (t == 0).astype(jnp.int32)
    head_lane = lax.broadcasted_iota(jnp.int32, (KEYS_PER_BLOCK, LANES), 1)
    if dil > 1:
        oscr_ref, lscr_ref = scr
    for j in range(BLOCKS_PER_ATTN_STEP):
        blk = slice(j * KEYS_PER_BLOCK, (j + 1) * KEYS_PER_BLOCK)
        qb = q_ref[blk, :]
        kself, vself = k_ref[blk, :], v_ref[blk, :]
        if j < dil:
            kprev, vprev = kp_ref[blk, :], vp_ref[blk, :]
        else:
            pblk = slice((j - dil) * KEYS_PER_BLOCK, (j - dil + 1) * KEYS_PER_BLOCK)
            kprev, vprev = k_ref[pblk, :], v_ref[pblk, :]
        first = first_step if j < dil else 0
        start = (j // dil) * tail + (j % dil)
        rows = pl.ds(start, KEYS_PER_BLOCK, stride=dil) if dil > 1 else pl.ds(start, KEYS_PER_BLOCK)
        m_tile = jnp.zeros((KEYS_PER_BLOCK, LANES), F32)
        l_tile = jnp.ones((KEYS_PER_BLOCK, LANES), F32)
        for hh in range(HEADS_PER_GROUP):
            cols = slice(hh * HEAD_DIM, (hh + 1) * HEAD_DIM)
            kc = jnp.concatenate([kprev[:, cols], kself[:, cols]], axis=0)
            vc = jnp.concatenate([vprev[:, cols], vself[:, cols]], axis=0)
            s = lax.dot_general(qb[:, cols], kc, (((1,), (1,)), ((), ())), preferred_element_type=F32)
            s = s + bias_ref[first, hh]
            m = jnp.max(s, axis=-1, keepdims=True)
            p = jnp.exp2(s - m)
            l = jnp.sum(p, axis=-1, keepdims=True)
            acc = jnp.dot(p.astype(BF16), vc, preferred_element_type=F32)
            if dil > 1:
                oscr_ref[hh, rows, :] = acc
            else:
                o_ref[rows, cols] = acc.astype(BF16)
            m_tile = jnp.where(head_lane == hh, m, m_tile)
            l_tile = jnp.where(head_lane == hh, l, l_tile)
        if dil > 1:
            lscr_ref[0, rows, :] = m_tile
            lscr_ref[1, rows, :] = l_tile
        else:
            lse_ref[rows, 0:LANES] = m_tile
            lse_ref[rows, LANES:] = l_tile

    if dil > 1:
        for hh in range(HEADS_PER_GROUP):
            o_ref[:, hh * HEAD_DIM:(hh + 1) * HEAD_DIM] = oscr_ref[hh].astype(BF16)
        lse_ref[:, 0:LANES] = lscr_ref[0]
        lse_ref[:, LANES:] = lscr_ref[1]


def _attn_call(group, q, k, v):
    dil = GROUPS[group][1]
    tail = KEYS_PER_BLOCK * dil
    steps = SEQ // ATTN_STEP_ROWS
    tails_per_step = ATTN_STEP_ROWS // tail
    cur = pl.BlockSpec((ATTN_STEP_ROWS, GROUP_COLS), lambda t: (t, 0))
    prev = pl.BlockSpec((tail, GROUP_COLS), lambda t: (jnp.maximum(t * tails_per_step - 1, 0), 0))
    lse_blk = pl.BlockSpec((ATTN_STEP_ROWS, 2 * LANES), lambda t: (t, 0))
    scratch = [pltpu.VMEM((2, HEADS_PER_GROUP, KEYS_PER_BLOCK, 2 * KEYS_PER_BLOCK), F32)]
    if dil > 1:
        scratch += [pltpu.VMEM((HEADS_PER_GROUP, ATTN_STEP_ROWS, HEAD_DIM), F32),
                    pltpu.VMEM((2, ATTN_STEP_ROWS, LANES), F32)]
    return pl.pallas_call(
        functools.partial(_attn_kernel, group=group),
        grid=(steps,),
        in_specs=[cur, cur, cur, prev, prev],
        out_specs=[cur, lse_blk],
        out_shape=[jax.ShapeDtypeStruct((SEQ, GROUP_COLS), BF16), jax.ShapeDtypeStruct((SEQ, 2 * LANES), F32)],
        scratch_shapes=scratch,
        compiler_params=_params(("arbitrary",)),
        name=f"prompt_attn_g{group}",
    )(q, k, v, k, v)


def _tail_kernel(x_ref, mod_ref, o0_ref, o1_ref, o2_ref, l0_ref, l1_ref, l2_ref, bin_ref,
                 wga_ref, wma0_ref, wma1_ref, wmb0_ref, wmb1_ref, b_ref, wpa_ref, wpb_ref, wo_ref, lng_ref, lnb_ref,
                 y_ref,
                 hb_ref, ain_ref, pre_ref, res_ref):
    shift = mod_ref[:, 0:D_MODEL]
    scale = mod_ref[:, D_MODEL:2 * D_MODEL]
    hb_ref[...] = (x_ref[...] * (1.0 + scale) + shift).astype(BF16)

    def gate_proj(wx_refs, col0, cols):
        wx_ref = wx_refs[cols.start // W_BLOCK]
        wcols = slice(cols.start % W_BLOCK, cols.start % W_BLOCK + MXU_COLS)
        bcols = slice(col0 + cols.start, col0 + cols.stop)
        return jnp.dot(hb_ref[...], wx_ref[:, wcols], preferred_element_type=F32) + b_ref[:, bcols]

    stats = (l0_ref, l1_ref, l2_ref)
    maxes = [st[:, 0:LANES] for st in stats]
    m = jnp.maximum(jnp.maximum(maxes[0], maxes[1]), maxes[2])
    es = [jnp.exp2(mg - m) for mg in maxes]
    inv = 1.0 / sum(e * st[:, LANES:] for e, st in zip(es, stats))
    weights = tuple(e * inv for e in es)
    o_refs = (o0_ref, o1_ref, o2_ref)
    for cols in _col_blocks(GROUP_COLS):
        heads = []
        for hh in range(cols.start // HEAD_DIM, cols.stop // HEAD_DIM):
            hcols = slice(hh * HEAD_DIM, (hh + 1) * HEAD_DIM)
            heads.append(sum(w[:, hh:hh + 1] * o_ref[:, hcols].astype(F32) for w, o_ref in zip(weights, o_refs)))
        ga = gate_proj((wga_ref,), COL_GA, cols)
        ain_ref[:, cols] = (jnp.concatenate(heads, axis=-1) * _silu(ga)).astype(BF16)

    for cols in _col_blocks(D_MODEL):
        a = jnp.dot(ain_ref[...], wpa_ref[:, cols], preferred_element_type=F32)
        bb = jnp.dot(bin_ref[...], wpb_ref[:, cols], preferred_element_type=F32)
        ma = gate_proj((wma0_ref, wma1_ref), COL_MA, cols)
        mb = gate_proj((wmb0_ref, wmb1_ref), COL_MB, cols)
        pre_ref[:, cols] = (_sigmoid(ma) * a + _sigmoid(mb) * bb).astype(BF16)
    for cols in _col_blocks(D_MODEL):
        gate = mod_ref[:, 2 * D_MODEL + cols.start:2 * D_MODEL + cols.stop]
        y = jnp.dot(pre_ref[...], wo_ref[:, cols], preferred_element_type=F32)
        res_ref[:, cols] = ALPHA * x_ref[:, cols] + gate * y
    mu, rstd = _row_stats(res_ref, D_MODEL)
    for cols in _col_blocks(D_MODEL):
        y_ref[:, cols] = (res_ref[:, cols] - mu) * rstd * lng_ref[:, cols] + lnb_ref[:, cols]


def _tail_call(x, mod_p, outs, lses, b_conv, w_bf, b_in, w_pa, w_pb, w_o, ln_g, ln_b):
    o_blk = pl.BlockSpec((ROW_TILE, GROUP_COLS), lambda i: (i, 0))
    lse_blk = pl.BlockSpec((ROW_TILE, 2 * LANES), lambda i: (i, 0))
    return pl.pallas_call(
        _tail_kernel,
        grid=(N_ROW_TILES,),
        in_specs=[
            pl.BlockSpec((ROW_TILE, D_MODEL), lambda i: (i, 0)),
            _row_vec(3 * D_MODEL),
            o_blk, o_blk, o_blk, lse_blk, lse_blk, lse_blk,
            pl.BlockSpec((ROW_TILE, CONV_CH), lambda i: (i, 0)),
            _w_cols(COL_GA, W_BLOCK),
            _w_cols(COL_MA, W_BLOCK), _w_cols(COL_MA + W_BLOCK, W_BLOCK),
            _w_cols(COL_MB, W_BLOCK), _w_cols(COL_MB + W_BLOCK, W_BLOCK),
            _row_vec(IN_COLS),
            _resident((GROUP_COLS, D_MODEL)),
            _resident((CONV_CH, D_MODEL)),
            _resident((D_MODEL, D_MODEL)),
            _row_vec(D_MODEL), _row_vec(D_MODEL),
        ],
        out_specs=pl.BlockSpec((ROW_TILE, D_MODEL), lambda i: (i, 0)),
        out_shape=jax.ShapeDtypeStruct((SEQ, D_MODEL), F32),
        scratch_shapes=[
            pltpu.VMEM((ROW_TILE, D_MODEL), BF16),
            pltpu.VMEM((ROW_TILE, GROUP_COLS), BF16),
            pltpu.VMEM((ROW_TILE, D_MODEL), BF16),
            pltpu.VMEM((ROW_TILE, D_MODEL), F32),
        ],
        compiler_params=_params(("arbitrary",)),
        name="prompt_tail",
    )(x, mod_p, *outs, *lses, b_conv, w_bf, w_bf, w_bf, w_bf, w_bf, b_in, w_pa, w_pb, w_o, ln_g, ln_b)


N_REST_BLOCKS = (IN_COLS - QKV_COLS) // W_BLOCK


def _sample_proj_kernel(x_ref, mod_ref, wq_ref, *rest):
    wr_refs = rest[:N_REST_BLOCKS]
    b_ref, qt0_ref, qt1_ref, qt2_ref, kv0_ref, kv1_ref, kv2_ref, zr_ref = rest[N_REST_BLOCKS:]
    shift = mod_ref[:, 0:D_MODEL]
    scale = mod_ref[:, D_MODEL:2 * D_MODEL]
    hb = (x_ref[...] * (1.0 + scale) + shift).astype(BF16)

    def proj(blk):
        cols = slice(blk * GROUP_COLS, (blk + 1) * GROUP_COLS)
        return jnp.dot(hb, wq_ref[:, cols], preferred_element_type=F32) + b_ref[:, cols]

    zeros = jnp.zeros((DEC_BATCH, GROUP_COLS), F32)
    for g, (qt_ref, kv_ref) in enumerate(((qt0_ref, kv0_ref), (qt1_ref, kv1_ref), (qt2_ref, kv2_ref))):
        _store_kv_tiles(qt_ref, proj(g) * Q_SCALE, zeros)
        _store_kv_tiles(kv_ref, proj(N_GROUPS + g), proj(2 * N_GROUPS + g))
    for blk, wr_ref in enumerate(wr_refs):
        cols = slice(blk * W_BLOCK, (blk + 1) * W_BLOCK)
        bcols = slice(QKV_COLS + cols.start, QKV_COLS + cols.stop)
        zr_ref[:, cols] = jnp.dot(hb, wr_ref[...], preferred_element_type=F32) + b_ref[:, bcols]


def _sample_proj_call(x_s, mod_s, w_bf, b_in):
    full = lambda shape: pl.BlockSpec(shape, lambda i: (0,) * len(shape))
    tile_rows = DEC_BATCH * KV_TILE_ROWS
    tiles = jax.ShapeDtypeStruct((tile_rows, HEAD_DIM), F32)
    rest_cols = IN_COLS - QKV_COLS
    return pl.pallas_call(
        _sample_proj_kernel,
        grid=(1,),
        in_specs=[full((DEC_BATCH, D_MODEL)), full((DEC_BATCH, 3 * D_MODEL)), _w_cols(0, QKV_COLS)]
                 + [_w_cols(QKV_COLS + blk * W_BLOCK, W_BLOCK) for blk in range(N_REST_BLOCKS)]
                 + [_row_vec(IN_COLS)],
        out_specs=[full((tile_rows, HEAD_DIM))] * 6 + [full((DEC_BATCH, rest_cols))],
        out_shape=[tiles] * 6 + [jax.ShapeDtypeStruct((DEC_BATCH, rest_cols), F32)],
        compiler_params=_params(("arbitrary",)),
        name="sample_proj",
    )(x_s, mod_s, w_bf, *([w_bf] * N_REST_BLOCKS), b_in)


def _sample_attn_kernel(qt0_ref, qt1_ref, qt2_ref, kv0_ref, kv1_ref, kv2_ref, c0_ref, c1_ref, c2_ref, o_ref,
                        bias_ref):
    qt_refs = (qt0_ref, qt1_ref, qt2_ref)
    kv_refs = (kv0_ref, kv1_ref, kv2_ref)
    c_refs = (c0_ref, c1_ref, c2_ref)
    half = HEADS_PER_GROUP
    tile = (KV_TILE_ROWS, HEAD_DIM)
    keys_tile = (KEYS_PER_BLOCK,) + tile

    @pl.when(pl.program_id(0) == 0)
    def _():
        key = lax.broadcasted_iota(jnp.int32, keys_tile, 0)
        sub = lax.broadcasted_iota(jnp.int32, keys_tile, 1)
        steps_back = (KEYS_PER_BLOCK - key).astype(F32)
        for g in range(N_GROUPS):
            slope_rows = jnp.zeros(keys_tile, F32)
            for hh in range(HEADS_PER_GROUP):
                slope_rows = jnp.where(sub == half + hh, _slope(g, hh), slope_rows)
            bias_ref[g] = -(slope_rows * (GROUPS[g][1] * steps_back)) * LOG2_E

    def body(b, carry):
        outs, lses = [], []
        for g in range(N_GROUPS):
            x = c_refs[g][b]
            qt = qt_refs[g][b] * LOG2_E
            kvn = kv_refs[g][b]
            s = jnp.broadcast_to(jnp.sum(x * qt[None], axis=-1, keepdims=True), keys_tile)
            s = pltpu.roll(s, half, axis=1) + bias_ref[g]
            s_self = jnp.broadcast_to(jnp.sum(qt * kvn, axis=-1, keepdims=True), tile)
            s_self = pltpu.roll(s_self, half, axis=0)
            m = jnp.maximum(jnp.max(s, axis=0), s_self)
            p = jnp.exp2(s - m[None])
            p_self = jnp.exp2(s_self - m)
            l = jnp.sum(p, axis=0) + p_self
            acc = jnp.sum(x * p, axis=0) + p_self * kvn
            outs.append(acc / l)
            lses.append(m + jnp.log2(l))
        mx = jnp.maximum(jnp.maximum(lses[0], lses[1]), lses[2])
        es = [jnp.exp2(ls - mx) for ls in lses]
        o_ref[b] = (es[0] * outs[0] + es[1] * outs[1] + es[2] * outs[2]) / (es[0] + es[1] + es[2])
        return carry

    lax.fori_loop(0, SAMPLE_BLOCK, body, 0, unroll=SAMPLE_UNROLL)


def _sample_attn_call(qts, kvs, caches):
    views = [c.reshape(DEC_BATCH, KEYS_PER_BLOCK, GROUPS[g][1], KV_TILE_ROWS, HEAD_DIM) for g, c in enumerate(caches)]
    tile3 = lambda a: a.reshape(DEC_BATCH, KV_TILE_ROWS, HEAD_DIM)
    tile_blk = pl.BlockSpec((SAMPLE_BLOCK, KV_TILE_ROWS, HEAD_DIM), lambda i: (i, 0, 0))
    cache_blk = pl.BlockSpec((SAMPLE_BLOCK, KEYS_PER_BLOCK, None, KV_TILE_ROWS, HEAD_DIM), lambda i: (i, 0, 0, 0, 0))
    return pl.pallas_call(
        _sample_attn_kernel,
        grid=(DEC_BATCH // SAMPLE_BLOCK,),
        in_specs=[tile_blk] * 6 + [cache_blk] * 3,
        out_specs=tile_blk,
        out_shape=jax.ShapeDtypeStruct((DEC_BATCH, KV_TILE_ROWS, HEAD_DIM), F32),
        scratch_shapes=[pltpu.VMEM((N_GROUPS, KEYS_PER_BLOCK, KV_TILE_ROWS, HEAD_DIM), F32)],
        compiler_params=_params(("arbitrary",)),
        name="sample_attn",
    )(*[tile3(a) for a in qts], *[tile3(a) for a in kvs], *views)


def _sample_tail_kernel(x_ref, mod_ref, o_ref, zr_ref, st_ref, cw_ref, cb_ref, cg_ref, cbeta_ref,
                        wpa_ref, wpb_ref, wo_ref, lng_ref, lnb_ref, y_ref, convs_ref):
    gate = mod_ref[:, 2 * D_MODEL:3 * D_MODEL]
    z = lambda col0, width: zr_ref[:, col0 - QKV_COLS:col0 - QKV_COLS + width]
    u = z(COL_GLU_A, CONV_CH) * _sigmoid(z(COL_GLU_G, CONV_CH))
    hist = CONV_WIDTH - 1
    acc = cb_ref[...] + u * cw_ref[hist:hist + 1, :]
    for j in range(hist):
        acc = acc + st_ref[j] * cw_ref[j:j + 1, :]
    conv_out = _silu(_layer_norm(acc, cg_ref[...], cbeta_ref[...]))
    for j in range(hist - 1):
        convs_ref[j] = st_ref[j + 1]
    convs_ref[hist - 1] = u
    o_attn = jnp.concatenate(
        [o_ref[pl.ds(HEADS_PER_GROUP + hh, DEC_BATCH, stride=KV_TILE_ROWS), :] for hh in range(HEADS_PER_GROUP)],
        axis=-1)
    a = jnp.dot((o_attn * _silu(z(COL_GA, GROUP_COLS))).astype(BF16), wpa_ref[...], preferred_element_type=F32)
    b = jnp.dot((conv_out * _silu(z(COL_GB, CONV_CH))).astype(BF16), wpb_ref[...], preferred_element_type=F32)
    pre = (_sigmoid(z(COL_MA, D_MODEL)) * a + _sigmoid(z(COL_MB, D_MODEL)) * b).astype(BF16)
    y = jnp.dot(pre, wo_ref[...], preferred_element_type=F32)
    y_ref[...] = _layer_norm(ALPHA * x_ref[...] + gate * y, lng_ref[...], lnb_ref[...])


def _sample_tail_call(x_s, mod_s, o_s, zr, state_t, conv_w, conv_b, cn_g, cn_b, w_pa, w_pb, w_o, ln_g, ln_b):
    full = lambda shape: pl.BlockSpec(shape, lambda i: (0,) * len(shape))
    args = (x_s, mod_s, o_s, zr, state_t, conv_w, conv_b, cn_g, cn_b, w_pa, w_pb, w_o, ln_g, ln_b)
    return pl.pallas_call(
        _sample_tail_kernel,
        grid=(1,),
        in_specs=[full(a.shape) for a in args],
        out_specs=[full((DEC_BATCH, D_MODEL)), full(state_t.shape)],
        out_shape=[jax.ShapeDtypeStruct((DEC_BATCH, D_MODEL), F32),
                   jax.ShapeDtypeStruct(state_t.shape, F32)],
        compiler_params=_params(("arbitrary",)),
        name="sample_tail",
    )(*args)


def kernel(x_prompt, x_sample, c_prompt, c_sample, cache_kv_w128, cache_kv_w512, cache_kv_w2048, state_conv,
           w_c, b_c, w_in, b_in, conv_w, conv_b, conv_norm_g, conv_norm_b, w_pa, w_pb, w_o, ln_g, ln_b):
    assert x_prompt.shape == (1, SEQ, D_MODEL) and x_sample.shape == (DEC_BATCH, 1, D_MODEL)
    assert w_in.shape == (D_MODEL, IN_COLS)
    caches = (cache_kv_w128, cache_kv_w512, cache_kv_w2048)
    for (window, _), c in zip(GROUPS, caches):
        assert c.shape == (DEC_BATCH, window, 2, HEADS_PER_GROUP, HEAD_DIM)

    w_bf = w_in.astype(BF16)
    w_pa_b, w_pb_b, w_o_b = w_pa.astype(BF16), w_pb.astype(BF16), w_o.astype(BF16)
    vec = lambda a: a.reshape(1, -1)
    b_row = vec(b_in)
    conv_args = (conv_w, vec(conv_b), vec(conv_norm_g), vec(conv_norm_b))
    out_args = (w_pa_b, w_pb_b, w_o_b, vec(ln_g), vec(ln_b))

    mod_p, mod_s = _mod_call(c_prompt, c_sample, w_c, b_c)

    x_p = x_prompt.reshape(SEQ, D_MODEL)
    (q0, k0, v0, q1, k1, v1, q2, k2, v2, kv0, kv1, kv2, b_conv, conv_p) = _front_call(
        x_p, mod_p, w_bf, b_row, *conv_args)
    flat = lambda a: a.reshape(SEQ, GROUP_COLS)
    attn = [_attn_call(0, q0, k0, v0), _attn_call(1, q1, k1, v1), _attn_call(2, flat(q2), flat(k2), flat(v2))]
    y_p = _tail_call(x_p, mod_p, [a[0] for a in attn], [a[1] for a in attn], b_conv, w_bf, b_row, *out_args)

    x_s = x_sample.reshape(DEC_BATCH, D_MODEL)
    qt0, qt1, qt2, kvs0, kvs1, kvs2, zr = _sample_proj_call(x_s, mod_s, w_bf, b_row)
    o_s = _sample_attn_call((qt0, qt1, qt2), (kvs0, kvs1, kvs2), caches)
    o_s = o_s.reshape(DEC_BATCH * KV_TILE_ROWS, HEAD_DIM)
    state_t = jnp.transpose(state_conv, (1, 0, 2))
    y_s, conv_s_t = _sample_tail_call(x_s, mod_s, o_s, zr, state_t, *conv_args, *out_args)

    kv_shape_p = lambda keep: (1, keep, 2, HEADS_PER_GROUP, HEAD_DIM)
    kv_shape_s = (DEC_BATCH, 1, 2, HEADS_PER_GROUP, HEAD_DIM)
    return (y_p.reshape(1, SEQ, D_MODEL),
            y_s.reshape(DEC_BATCH, 1, D_MODEL),
            kv0.reshape(kv_shape_p(GROUPS[0][0])),
            kv1.reshape(kv_shape_p(GROUPS[1][0])),
            kv2.reshape(kv_shape_p(GROUPS[2][0])),
            conv_p.reshape(1, CONV_WIDTH - 1, CONV_CH),
            kvs0.reshape(kv_shape_s), kvs1.reshape(kv_shape_s), kvs2.reshape(kv_shape_s),
            jnp.transpose(conv_s_t, (1, 0, 2)))
```

```python
import functools

import jax
import jax.numpy as jnp
from jax import lax
from jax.experimental import pallas as pl
from jax.experimental.pallas import tpu as pltpu

F32 = jnp.float32
BF16 = jnp.bfloat16

D_MODEL = 1024
SEQ = 16384
DEC_BATCH = 128
DEPTH = 1
HEAD_DIM = 128
HEADS_PER_GROUP = 4
GROUPS = ((128, 1), (512, 4), (2048, 16))
N_GROUPS = len(GROUPS)
N_HEADS = N_GROUPS * HEADS_PER_GROUP
GROUP_COLS = HEADS_PER_GROUP * HEAD_DIM
ATTN_QKV = N_HEADS * HEAD_DIM
QKV_COLS = 3 * ATTN_QKV
CONV_CH = D_MODEL // 2
CONV_WIDTH = 31
COL_GA = QKV_COLS
COL_GLU_A = COL_GA + GROUP_COLS
COL_GLU_G = COL_GLU_A + CONV_CH
COL_GB = COL_GLU_G + CONV_CH
COL_MA = COL_GB + CONV_CH
COL_MB = COL_MA + D_MODEL
IN_COLS = COL_MB + D_MODEL
ALPHA = (2.0 * DEPTH) ** 0.25
LN_EPS = 1e-5
NEG = -1e30
Q_SCALE = HEAD_DIM ** -0.5
LOG2_E = 1.4426950408889634
Q_SCALE_LOG2 = Q_SCALE * LOG2_E
KEYS_PER_BLOCK = 128
KV_TILE_ROWS = 2 * HEADS_PER_GROUP

LANES = 128
MXU_COLS = 256
W_BLOCK = 512
ROW_TILE = 512
ATTN_STEP_ROWS = 2048
BLOCKS_PER_ATTN_STEP = ATTN_STEP_ROWS // KEYS_PER_BLOCK
CONV_ROW_CHUNK = 128
HIST_ROWS = 32
SAMPLE_BLOCK = 8
SAMPLE_UNROLL = 4
VMEM_LIMIT_BYTES = 56 * 1024 * 1024


def _slope(group, head):
    return 2.0 ** (-8.0 * (group * HEADS_PER_GROUP + head + 1) / N_HEADS)


def _sigmoid(x):
    return 1.0 / (1.0 + jnp.exp(-x))


def _silu(x):
    return x * _sigmoid(x)


def _layer_norm(x, g, b):
    mu = jnp.mean(x, axis=-1, keepdims=True)
    xc = x - mu
    var = jnp.mean(xc * xc, axis=-1, keepdims=True)
    return xc * lax.rsqrt(var + LN_EPS) * g + b


def _col_blocks(width):
    return [slice(b * MXU_COLS, (b + 1) * MXU_COLS) for b in range(width // MXU_COLS)]


def _row_stats(ref, width):
    total = sum(jnp.sum(ref[:, cols], axis=-1, keepdims=True) for cols in _col_blocks(width))
    mu = total / width
    sq = sum(jnp.sum(jnp.square(ref[:, cols] - mu), axis=-1, keepdims=True) for cols in _col_blocks(width))
    return mu, lax.rsqrt(sq / width + LN_EPS)


def _resident(shape):
    return pl.BlockSpec(shape, lambda *_: (0,) * len(shape), pipeline_mode=pl.Buffered(1))


def _w_cols(col0, width):
    assert col0 % width == 0
    return pl.BlockSpec((D_MODEL, width), lambda *_: (0, col0 // width), pipeline_mode=pl.Buffered(1))


def _row_vec(n):
    return pl.BlockSpec((1, n), lambda *_: (0, 0))


def _params(semantics):
    return pltpu.CompilerParams(dimension_semantics=semantics, vmem_limit_bytes=VMEM_LIMIT_BYTES)


def _store_kv_tiles(ref, k, v):
    rows = k.shape[0]
    for hh in range(HEADS_PER_GROUP):
        cols = slice(hh * HEAD_DIM, (hh + 1) * HEAD_DIM)
        ref[pl.ds(hh, rows, stride=KV_TILE_ROWS), :] = k[:, cols]
        ref[pl.ds(HEADS_PER_GROUP + hh, rows, stride=KV_TILE_ROWS), :] = v[:, cols]


def _mod_kernel(cp_ref, cs_ref, w_ref, b_ref, mp_ref, ms_ref):
    w = w_ref[...].astype(BF16)
    cp = jnp.broadcast_to(cp_ref[...], (8, D_MODEL)).astype(BF16)
    mp_ref[...] = jnp.dot(cp, w, preferred_element_type=F32)[0:1] + b_ref[...]
    ms_ref[...] = jnp.dot(cs_ref[...].astype(BF16), w, preferred_element_type=F32) + b_ref[...]


def _mod_call(c_prompt, c_sample, w_c, b_c):
    n_blk = 3
    return pl.pallas_call(
        _mod_kernel,
        grid=(n_blk,),
        in_specs=[
            pl.BlockSpec((1, D_MODEL), lambda n: (0, 0)),
            pl.BlockSpec((DEC_BATCH, D_MODEL), lambda n: (0, 0)),
            pl.BlockSpec((D_MODEL, D_MODEL), lambda n: (0, n)),
            pl.BlockSpec((1, D_MODEL), lambda n: (0, n)),
        ],
        out_specs=[
            pl.BlockSpec((1, D_MODEL), lambda n: (0, n)),
            pl.BlockSpec((DEC_BATCH, D_MODEL), lambda n: (0, n)),
        ],
        out_shape=[
            jax.ShapeDtypeStruct((1, 3 * D_MODEL), F32),
            jax.ShapeDtypeStruct((DEC_BATCH, 3 * D_MODEL), F32),
        ],
        compiler_params=_params(("arbitrary",)),
        name="adaln_mod",
    )(c_prompt, c_sample, w_c, b_c.reshape(1, 3 * D_MODEL))


N_ROW_TILES = SEQ // ROW_TILE
KV2_TILES = GROUPS[2][0] // ROW_TILE


def _front_kernel(x_ref, mod_ref, w_ref, wa_ref, wg_ref, wgb_ref, b_ref, cw_ref, cb_ref, cg_ref, cbeta_ref,
                  q0_ref, k0_ref, v0_ref, q1_ref, k1_ref, v1_ref, q2_ref, k2_ref, v2_ref,
                  kv0_ref, kv1_ref, kv2_ref, bin_ref, convp_ref,
                  hs_ref, hq_ref, hb_ref, hp4_ref, hp16_ref, uext_ref, cacc_ref):
    i = pl.program_id(0)
    shift = mod_ref[:, 0:D_MODEL]
    scale = mod_ref[:, D_MODEL:2 * D_MODEL]
    h = x_ref[...] * (1.0 + scale) + shift
    hb_ref[...] = h.astype(BF16)

    n_slab = D_MODEL // LANES
    d1, d2 = GROUPS[1][1], GROUPS[2][1]
    assert d2 == d1 * d1
    r1, r2 = ROW_TILE // d1, ROW_TILE // d2
    for c in range(n_slab):
        cols = slice(c * LANES, (c + 1) * LANES)
        hs_ref[c] = h[:, cols]
        for b in range(d1):
            rows_b = hs_ref[c, pl.ds(b, r1, stride=d1), :]
            hq_ref[c, b * r1:(b + 1) * r1, :] = rows_b
            hp4_ref[b * r1:(b + 1) * r1, cols] = rows_b.astype(BF16)
        for r in range(d2):
            a, b = r // d1, r % d1
            hp16_ref[r * r2:(r + 1) * r2, cols] = hq_ref[c, pl.ds(b * r1 + a, r2, stride=d1), :].astype(BF16)

    def proj(lhs_ref, blk):
        cols = slice(blk * GROUP_COLS, (blk + 1) * GROUP_COLS)
        return jnp.dot(lhs_ref[...], w_ref[:, cols], preferred_element_type=F32) + b_ref[:, cols]

    def gate_proj(wx_ref, col0, cols):
        bcols = slice(col0 + cols.start, col0 + cols.stop)
        return jnp.dot(hb_ref[...], wx_ref[:, cols], preferred_element_type=F32) + b_ref[:, bcols]

    @pl.when(i == 0)
    def _():
        uext_ref[:, 0:HIST_ROWS, :] = jnp.zeros((CONV_CH // LANES, HIST_ROWS, LANES), F32)

    slabs_per_blk = MXU_COLS // LANES
    for b, cols in enumerate(_col_blocks(CONV_CH)):
        u = gate_proj(wa_ref, COL_GLU_A, cols) * _sigmoid(gate_proj(wg_ref, COL_GLU_G, cols))
        convp_ref[:, cols] = u[ROW_TILE - (CONV_WIDTH - 1):, :]
        for s in range(slabs_per_blk):
            uext_ref[b * slabs_per_blk + s, HIST_ROWS:, :] = u[:, s * LANES:(s + 1) * LANES]

    lead = HIST_ROWS - (CONV_WIDTH - 1)
    for c in range(CONV_CH // LANES):
        lanes = slice(c * LANES, (c + 1) * LANES)
        for r0 in range(0, ROW_TILE, CONV_ROW_CHUNK):
            acc = jnp.broadcast_to(cb_ref[:, lanes], (CONV_ROW_CHUNK, LANES))
            for j in range(CONV_WIDTH):
                acc = acc + uext_ref[c, pl.ds(r0 + lead + j, CONV_ROW_CHUNK), :] * cw_ref[j:j + 1, lanes]
            cacc_ref[r0:r0 + CONV_ROW_CHUNK, lanes] = acc
        uext_ref[c, 0:HIST_ROWS, :] = uext_ref[c, ROW_TILE:ROW_TILE + HIST_ROWS, :]

    mu, rstd = _row_stats(cacc_ref, CONV_CH)
    for cols in _col_blocks(CONV_CH):
        normed = (cacc_ref[:, cols] - mu) * rstd * cg_ref[:, cols] + cbeta_ref[:, cols]
        bin_ref[:, cols] = (_silu(normed) * _silu(gate_proj(wgb_ref, COL_GB, cols))).astype(BF16)

    def store_perm16(ref, val):
        val = val.astype(BF16)
        for r in range(d2):
            ref[0, r] = val[r * r2:(r + 1) * r2, :]

    q0_ref[...] = (proj(hb_ref, 0) * Q_SCALE_LOG2).astype(BF16)
    k0 = proj(hb_ref, 3)
    v0 = proj(hb_ref, 6)
    k0_ref[...] = k0.astype(BF16)
    v0_ref[...] = v0.astype(BF16)
    keep0 = GROUPS[0][0]
    _store_kv_tiles(kv0_ref, k0[ROW_TILE - keep0:, :], v0[ROW_TILE - keep0:, :])
    q1_ref[...] = (proj(hp4_ref, 1) * Q_SCALE_LOG2).astype(BF16)
    k1_ref[...] = proj(hp4_ref, 4).astype(BF16)
    v1_ref[...] = proj(hp4_ref, 7).astype(BF16)
    store_perm16(q2_ref, proj(hp16_ref, 2) * Q_SCALE_LOG2)
    store_perm16(k2_ref, proj(hp16_ref, 5))
    store_perm16(v2_ref, proj(hp16_ref, 8))

    @pl.when(i == N_ROW_TILES - 1)
    def _():
        _store_kv_tiles(kv1_ref, proj(hb_ref, 4), proj(hb_ref, 7))

    @pl.when(i >= N_ROW_TILES - KV2_TILES)
    def _():
        _store_kv_tiles(kv2_ref, proj(hb_ref, 5), proj(hb_ref, 8))


def _front_call(x, mod_p, w_bf, b_in, conv_w, conv_b, cn_g, cn_b):
    d2 = GROUPS[2][1]
    spans2 = SEQ // GROUPS[2][0]
    tiles_per_span2 = GROUPS[2][0] // ROW_TILE
    nat = pl.BlockSpec((ROW_TILE, GROUP_COLS), lambda i: (i, 0))
    perm16 = pl.BlockSpec((1, d2, ROW_TILE // d2, GROUP_COLS),
                          lambda i: (i // tiles_per_span2, 0, i % tiles_per_span2, 0))
    nat_shape = jax.ShapeDtypeStruct((SEQ, GROUP_COLS), BF16)
    p16_shape = jax.ShapeDtypeStruct((spans2, d2, KEYS_PER_BLOCK, GROUP_COLS), BF16)
    first_kv2_tile = N_ROW_TILES - KV2_TILES
    kv_shape = lambda g: jax.ShapeDtypeStruct((GROUPS[g][0] * KV_TILE_ROWS, HEAD_DIM), F32)
    return pl.pallas_call(
        _front_kernel,
        grid=(N_ROW_TILES,),
        in_specs=[
            pl.BlockSpec((ROW_TILE, D_MODEL), lambda i: (i, 0)),
            _row_vec(3 * D_MODEL),
            _w_cols(0, QKV_COLS), _w_cols(COL_GLU_A, W_BLOCK), _w_cols(COL_GLU_G, W_BLOCK), _w_cols(COL_GB, W_BLOCK),
            _row_vec(IN_COLS),
            pl.BlockSpec((CONV_WIDTH, CONV_CH), lambda i: (0, 0)),
            _row_vec(CONV_CH), _row_vec(CONV_CH), _row_vec(CONV_CH),
        ],
        out_specs=[nat, nat, nat, nat, nat, nat, perm16, perm16, perm16,
                   pl.BlockSpec((GROUPS[0][0] * KV_TILE_ROWS, HEAD_DIM), lambda i: (0, 0)),
                   pl.BlockSpec((GROUPS[1][0] * KV_TILE_ROWS, HEAD_DIM), lambda i: (0, 0)),
                   pl.BlockSpec((ROW_TILE * KV_TILE_ROWS, HEAD_DIM),
                                lambda i: (jnp.maximum(i - first_kv2_tile, 0), 0)),
                   nat,
                   pl.BlockSpec((CONV_WIDTH - 1, CONV_CH), lambda i: (0, 0))],
        out_shape=[nat_shape] * 6 + [p16_shape] * 3 + [kv_shape(0), kv_shape(1), kv_shape(2), nat_shape,
                                                        jax.ShapeDtypeStruct((CONV_WIDTH - 1, CONV_CH), F32)],
        scratch_shapes=[
            pltpu.VMEM((D_MODEL // LANES, ROW_TILE, LANES), F32),
            pltpu.VMEM((D_MODEL // LANES, ROW_TILE, LANES), F32),
            pltpu.VMEM((ROW_TILE, D_MODEL), BF16),
            pltpu.VMEM((ROW_TILE, D_MODEL), BF16),
            pltpu.VMEM((ROW_TILE, D_MODEL), BF16),
            pltpu.VMEM((CONV_CH // LANES, HIST_ROWS + ROW_TILE, LANES), F32),
            pltpu.VMEM((ROW_TILE, CONV_CH), F32),
        ],
        compiler_params=_params(("arbitrary",)),
        name="prompt_front",
    )(x, mod_p, w_bf, w_bf, w_bf, w_bf, b_in, conv_w, conv_b, cn_g, cn_b)


def _attn_kernel(q_ref, k_ref, v_ref, kp_ref, vp_ref, o_ref, lse_ref, bias_ref, *scr, group):
    dil = GROUPS[group][1]
    tail = KEYS_PER_BLOCK * dil
    t = pl.program_id(0)

    @pl.when(t == 0)
    def _():
        row = lax.broadcasted_iota(jnp.int32, (KEYS_PER_BLOCK, 2 * KEYS_PER_BLOCK), 0)
        col = lax.broadcasted_iota(jnp.int32, (KEYS_PER_BLOCK, 2 * KEYS_PER_BLOCK), 1)
        dist = row - col + KEYS_PER_BLOCK
        valid = (dist >= 0) & (dist <= KEYS_PER_BLOCK)
        valid_first = valid & (col >= KEYS_PER_BLOCK)
        distf = (dil * dist).astype(F32)
        for hh in range(HEADS_PER_GROUP):
            b = -(_slope(group, hh) * distf) * LOG2_E
            bias_ref[0, hh] = jnp.where(valid, b, NEG)
            bias_ref[1, hh] = jnp.where(valid_first, b, NEG)

    first_step = (t == 0).astype(jnp.int32)
    head_lane = lax.broadcasted_iota(jnp.int32, (KEYS_PER_BLOCK, LANES), 1)
    if dil > 1:
        oscr_ref, lscr_ref = scr
    for j in range(BLOCKS_PER_ATTN_STEP):
        blk = slice(j * KEYS_PER_BLOCK, (j + 1) * KEYS_PER_BLOCK)
        qb = q_ref[blk, :]
        kself, vself = k_ref[blk, :], v_ref[blk, :]
        if j < dil:
            kprev, vprev = kp_ref[blk, :], vp_ref[blk, :]
        else:
            pblk = slice((j - dil) * KEYS_PER_BLOCK, (j - dil + 1) * KEYS_PER_BLOCK)
            kprev, vprev = k_ref[pblk, :], v_ref[pblk, :]
        first = first_step if j < dil else 0
        start = (j // dil) * tail + (j % dil)
        rows = pl.ds(start, KEYS_PER_BLOCK, stride=dil) if dil > 1 else pl.ds(start, KEYS_PER_BLOCK)
        m_tile = jnp.zeros((KEYS_PER_BLOCK, LANES), F32)
        l_tile = jnp.ones((KEYS_PER_BLOCK, LANES), F32)
        for hh in range(HEADS_PER_GROUP):
            cols = slice(hh * HEAD_DIM, (hh + 1) * HEAD_DIM)
            kc = jnp.concatenate([kprev[:, cols], kself[:, cols]], axis=0)
            vc = jnp.concatenate([vprev[:, cols], vself[:, cols]], axis=0)
            s = lax.dot_general(qb[:, cols], kc, (((1,), (1,)), ((), ())), preferred_element_type=F32)
            s = s + bias_ref[first, hh]
            m = jnp.max(s, axis=-1, keepdims=True)
            p = jnp.exp2(s - m)
            l = jnp.sum(p, axis=-1, keepdims=True)
            acc = jnp.dot(p.astype(BF16), vc, preferred_element_type=F32)
            if dil > 1:
                oscr_ref[hh, rows, :] = acc
            else:
                o_ref[rows, cols] = acc.astype(BF16)
            m_tile = jnp.where(head_lane == hh, m, m_tile)
            l_tile = jnp.where(head_lane == hh, l, l_tile)
        if dil > 1:
            lscr_ref[0, rows, :] = m_tile
            lscr_ref[1, rows, :] = l_tile
        else:
            lse_ref[rows, 0:LANES] = m_tile
            lse_ref[rows, LANES:] = l_tile

    if dil > 1:
        for hh in range(HEADS_PER_GROUP):
            o_ref[:, hh * HEAD_DIM:(hh + 1) * HEAD_DIM] = oscr_ref[hh].astype(BF16)
        lse_ref[:, 0:LANES] = lscr_ref[0]
        lse_ref[:, LANES:] = lscr_ref[1]


def _attn_call(group, q, k, v):
    dil = GROUPS[group][1]
    tail = KEYS_PER_BLOCK * dil
    steps = SEQ // ATTN_STEP_ROWS
    tails_per_step = ATTN_STEP_ROWS // tail
    cur = pl.BlockSpec((ATTN_STEP_ROWS, GROUP_COLS), lambda t: (t, 0))
    prev = pl.BlockSpec((tail, GROUP_COLS), lambda t: (jnp.maximum(t * tails_per_step - 1, 0), 0))
    lse_blk = pl.BlockSpec((ATTN_STEP_ROWS, 2 * LANES), lambda t: (t, 0))
    scratch = [pltpu.VMEM((2, HEADS_PER_GROUP, KEYS_PER_BLOCK, 2 * KEYS_PER_BLOCK), F32)]
    if dil > 1:
        scratch += [pltpu.VMEM((HEADS_PER_GROUP, ATTN_STEP_ROWS, HEAD_DIM), F32),
                    pltpu.VMEM((2, ATTN_STEP_ROWS, LANES), F32)]
    return pl.pallas_call(
        functools.partial(_attn_kernel, group=group),
        grid=(steps,),
        in_specs=[cur, cur, cur, prev, prev],
        out_specs=[cur, lse_blk],
        out_shape=[jax.ShapeDtypeStruct((SEQ, GROUP_COLS), BF16), jax.ShapeDtypeStruct((SEQ, 2 * LANES), F32)],
        scratch_shapes=scratch,
        compiler_params=_params(("arbitrary",)),
        name=f"prompt_attn_g{group}",
    )(q, k, v, k, v)


def _tail_kernel(x_ref, mod_ref, o0_ref, o1_ref, o2_ref, l0_ref, l1_ref, l2_ref, bin_ref,
                 wga_ref, wma0_ref, wma1_ref, wmb0_ref, wmb1_ref, b_ref, wpa_ref, wpb_ref, wo_ref, lng_ref, lnb_ref,
                 y_ref,
                 hb_ref, ain_ref, pre_ref, res_ref):
    shift = mod_ref[:, 0:D_MODEL]
    scale = mod_ref[:, D_MODEL:2 * D_MODEL]
    hb_ref[...] = (x_ref[...] * (1.0 + scale) + shift).astype(BF16)

    def gate_proj(wx_refs, col0, cols):
        wx_ref = wx_refs[cols.start // W_BLOCK]
        wcols = slice(cols.start % W_BLOCK, cols.start % W_BLOCK + MXU_COLS)
        bcols = slice(col0 + cols.start, col0 + cols.stop)
        return jnp.dot(hb_ref[...], wx_ref[:, wcols], preferred_element_type=F32) + b_ref[:, bcols]

    stats = (l0_ref, l1_ref, l2_ref)
    maxes = [st[:, 0:LANES] for st in stats]
    m = jnp.maximum(jnp.maximum(maxes[0], maxes[1]), maxes[2])
    es = [jnp.exp2(mg - m) for mg in maxes]
    inv = 1.0 / sum(e * st[:, LANES:] for e, st in zip(es, stats))
    weights = tuple(e * inv for e in es)
    o_refs = (o0_ref, o1_ref, o2_ref)
    for cols in _col_blocks(GROUP_COLS):
        heads = []
        for hh in range(cols.start // HEAD_DIM, cols.stop // HEAD_DIM):
            hcols = slice(hh * HEAD_DIM, (hh + 1) * HEAD_DIM)
            heads.append(sum(w[:, hh:hh + 1] * o_ref[:, hcols].astype(F32) for w, o_ref in zip(weights, o_refs)))
        ga = gate_proj((wga_ref,), COL_GA, cols)
        ain_ref[:, cols] = (jnp.concatenate(heads, axis=-1) * _silu(ga)).astype(BF16)

    for cols in _col_blocks(D_MODEL):
        a = jnp.dot(ain_ref[...], wpa_ref[:, cols], preferred_element_type=F32)
        bb = jnp.dot(bin_ref[...], wpb_ref[:, cols], preferred_element_type=F32)
        ma = gate_proj((wma0_ref, wma1_ref), COL_MA, cols)
        mb = gate_proj((wmb0_ref, wmb1_ref), COL_MB, cols)
        pre_ref[:, cols] = (_sigmoid(ma) * a + _sigmoid(mb) * bb).astype(BF16)
    for cols in _col_blocks(D_MODEL):
        gate = mod_ref[:, 2 * D_MODEL + cols.start:2 * D_MODEL + cols.stop]
        y = jnp.dot(pre_ref[...], wo_ref[:, cols], preferred_element_type=F32)
        res_ref[:, cols] = ALPHA * x_ref[:, cols] + gate * y
    mu, rstd = _row_stats(res_ref, D_MODEL)
    for cols in _col_blocks(D_MODEL):
        y_ref[:, cols] = (res_ref[:, cols] - mu) * rstd * lng_ref[:, cols] + lnb_ref[:, cols]


def _tail_call(x, mod_p, outs, lses, b_conv, w_bf, b_in, w_pa, w_pb, w_o, ln_g, ln_b):
    o_blk = pl.BlockSpec((ROW_TILE, GROUP_COLS), lambda i: (i, 0))
    lse_blk = pl.BlockSpec((ROW_TILE, 2 * LANES), lambda i: (i, 0))
    return pl.pallas_call(
        _tail_kernel,
        grid=(N_ROW_TILES,),
        in_specs=[
            pl.BlockSpec((ROW_TILE, D_MODEL), lambda i: (i, 0)),
            _row_vec(3 * D_MODEL),
            o_blk, o_blk, o_blk, lse_blk, lse_blk, lse_blk,
            pl.BlockSpec((ROW_TILE, CONV_CH), lambda i: (i, 0)),
            _w_cols(COL_GA, W_BLOCK),
            _w_cols(COL_MA, W_BLOCK), _w_cols(COL_MA + W_BLOCK, W_BLOCK),
            _w_cols(COL_MB, W_BLOCK), _w_cols(COL_MB + W_BLOCK, W_BLOCK),
            _row_vec(IN_COLS),
            _resident((GROUP_COLS, D_MODEL)),
            _resident((CONV_CH, D_MODEL)),
            _resident((D_MODEL, D_MODEL)),
            _row_vec(D_MODEL), _row_vec(D_MODEL),
        ],
        out_specs=pl.BlockSpec((ROW_TILE, D_MODEL), lambda i: (i, 0)),
        out_shape=jax.ShapeDtypeStruct((SEQ, D_MODEL), F32),
        scratch_shapes=[
            pltpu.VMEM((ROW_TILE, D_MODEL), BF16),
            pltpu.VMEM((ROW_TILE, GROUP_COLS), BF16),
            pltpu.VMEM((ROW_TILE, D_MODEL), BF16),
            pltpu.VMEM((ROW_TILE, D_MODEL), F32),
        ],
        compiler_params=_params(("arbitrary",)),
        name="prompt_tail",
    )(x, mod_p, *outs, *lses, b_conv, w_bf, w_bf, w_bf, w_bf, w_bf, b_in, w_pa, w_pb, w_o, ln_g, ln_b)


N_W_BLOCKS = IN_COLS // W_BLOCK


def _sample_proj_kernel(x_ref, mod_ref, w_ref, b_ref,
                        wbf_ref, z_ref, qt0_ref, qt1_ref, qt2_ref, kv0_ref, kv1_ref, kv2_ref):
    n = pl.program_id(0)
    shift = mod_ref[:, 0:D_MODEL]
    scale = mod_ref[:, D_MODEL:2 * D_MODEL]
    hb = (x_ref[...] * (1.0 + scale) + shift).astype(BF16)
    wb = w_ref[...].astype(BF16)
    wbf_ref[...] = wb
    col0 = pl.multiple_of(n * W_BLOCK, W_BLOCK)
    z_ref[:, pl.ds(col0, W_BLOCK)] = jnp.dot(hb, wb, preferred_element_type=F32) + b_ref[:, pl.ds(col0, W_BLOCK)]

    @pl.when(n == N_W_BLOCKS - 1)
    def _():
        zeros = jnp.zeros((DEC_BATCH, GROUP_COLS), F32)
        blk = lambda i: z_ref[:, i * GROUP_COLS:(i + 1) * GROUP_COLS]
        for g, (qt_ref, kv_ref) in enumerate(((qt0_ref, kv0_ref), (qt1_ref, kv1_ref), (qt2_ref, kv2_ref))):
            _store_kv_tiles(qt_ref, blk(g) * Q_SCALE, zeros)
            _store_kv_tiles(kv_ref, blk(N_GROUPS + g), blk(2 * N_GROUPS + g))


def _sample_proj_call(x_s, mod_s, w_in, b_in):
    full = lambda shape: pl.BlockSpec(shape, lambda n: (0,) * len(shape))
    tile_rows = DEC_BATCH * KV_TILE_ROWS
    tiles = jax.ShapeDtypeStruct((tile_rows, HEAD_DIM), F32)
    w_blk = pl.BlockSpec((D_MODEL, W_BLOCK), lambda n: (0, n))
    return pl.pallas_call(
        _sample_proj_kernel,
        grid=(N_W_BLOCKS,),
        in_specs=[full((DEC_BATCH, D_MODEL)), full((DEC_BATCH, 3 * D_MODEL)), w_blk, _row_vec(IN_COLS)],
        out_specs=[w_blk, full((DEC_BATCH, IN_COLS))] + [full((tile_rows, HEAD_DIM))] * 6,
        out_shape=[jax.ShapeDtypeStruct((D_MODEL, IN_COLS), BF16), jax.ShapeDtypeStruct((DEC_BATCH, IN_COLS), F32)]
                  + [tiles] * 6,
        compiler_params=_params(("arbitrary",)),
        name="sample_proj",
    )(x_s, mod_s, w_in, b_in)


def _sample_attn_kernel(qt0_ref, qt1_ref, qt2_ref, kv0_ref, kv1_ref, kv2_ref, c0_ref, c1_ref, c2_ref, o_ref,
                        bias_ref):
    qt_refs = (qt0_ref, qt1_ref, qt2_ref)
    kv_refs = (kv0_ref, kv1_ref, kv2_ref)
    c_refs = (c0_ref, c1_ref, c2_ref)
    half = HEADS_PER_GROUP
    tile = (KV_TILE_ROWS, HEAD_DIM)
    keys_tile = (KEYS_PER_BLOCK,) + tile

    @pl.when(pl.program_id(0) == 0)
    def _():
        key = lax.broadcasted_iota(jnp.int32, keys_tile, 0)
        sub = lax.broadcasted_iota(jnp.int32, keys_tile, 1)
        steps_back = (KEYS_PER_BLOCK - key).astype(F32)
        for g in range(N_GROUPS):
            slope_rows = jnp.zeros(keys_tile, F32)
            for hh in range(HEADS_PER_GROUP):
                slope_rows = jnp.where(sub == half + hh, _slope(g, hh), slope_rows)
            bias_ref[g] = -(slope_rows * (GROUPS[g][1] * steps_back)) * LOG2_E

    def body(b, carry):
        outs, lses = [], []
        for g in range(N_GROUPS):
            x = c_refs[g][b]
            qt = qt_refs[g][b] * LOG2_E
            kvn = kv_refs[g][b]
            s = jnp.broadcast_to(jnp.sum(x * qt[None], axis=-1, keepdims=True), keys_tile)
            s = pltpu.roll(s, half, axis=1) + bias_ref[g]
            s_self = jnp.broadcast_to(jnp.sum(qt * kvn, axis=-1, keepdims=True), tile)
            s_self = pltpu.roll(s_self, half, axis=0)
            m = jnp.maximum(jnp.max(s, axis=0), s_self)
            p = jnp.exp2(s - m[None])
            p_self = jnp.exp2(s_self - m)
            l = jnp.sum(p, axis=0) + p_self
            acc = jnp.sum(x * p, axis=0) + p_self * kvn
            outs.append(acc / l)
            lses.append(m + jnp.log2(l))
        mx = jnp.maximum(jnp.maximum(lses[0], lses[1]), lses[2])
        es = [jnp.exp2(ls - mx) for ls in lses]
        o_ref[b] = (es[0] * outs[0] + es[1] * outs[1] + es[2] * outs[2]) / (es[0] + es[1] + es[2])
        return carry

    lax.fori_loop(0, SAMPLE_BLOCK, body, 0, unroll=SAMPLE_UNROLL)


def _sample_attn_call(qts, kvs, caches):
    views = [c.reshape(DEC_BATCH, KEYS_PER_BLOCK, GROUPS[g][1], KV_TILE_ROWS, HEAD_DIM) for g, c in enumerate(caches)]
    tile3 = lambda a: a.reshape(DEC_BATCH, KV_TILE_ROWS, HEAD_DIM)
    tile_blk = pl.BlockSpec((SAMPLE_BLOCK, KV_TILE_ROWS, HEAD_DIM), lambda i: (i, 0, 0))
    cache_blk = pl.BlockSpec((SAMPLE_BLOCK, KEYS_PER_BLOCK, None, KV_TILE_ROWS, HEAD_DIM), lambda i: (i, 0, 0, 0, 0))
    return pl.pallas_call(
        _sample_attn_kernel,
        grid=(DEC_BATCH // SAMPLE_BLOCK,),
        in_specs=[tile_blk] * 6 + [cache_blk] * 3,
        out_specs=tile_blk,
        out_shape=jax.ShapeDtypeStruct((DEC_BATCH, KV_TILE_ROWS, HEAD_DIM), F32),
        scratch_shapes=[pltpu.VMEM((N_GROUPS, KEYS_PER_BLOCK, KV_TILE_ROWS, HEAD_DIM), F32)],
        compiler_params=_params(("arbitrary",)),
        name="sample_attn",
    )(*[tile3(a) for a in qts], *[tile3(a) for a in kvs], *views)


def _sample_tail_kernel(x_ref, mod_ref, o_ref, zr_ref, st_ref, cw_ref, cb_ref, cg_ref, cbeta_ref,
                        wpa_ref, wpb_ref, wo_ref, lng_ref, lnb_ref, y_ref, convs_ref):
    gate = mod_ref[:, 2 * D_MODEL:3 * D_MODEL]
    z = lambda col0, width: zr_ref[:, col0:col0 + width]
    u = z(COL_GLU_A, CONV_CH) * _sigmoid(z(COL_GLU_G, CONV_CH))
    hist = CONV_WIDTH - 1
    acc = cb_ref[...] + u * cw_ref[hist:hist + 1, :]
    for j in range(hist):
        acc = acc + st_ref[j] * cw_ref[j:j + 1, :]
    conv_out = _silu(_layer_norm(acc, cg_ref[...], cbeta_ref[...]))
    for j in range(hist - 1):
        convs_ref[j] = st_ref[j + 1]
    convs_ref[hist - 1] = u
    o_attn = jnp.concatenate(
        [o_ref[pl.ds(HEADS_PER_GROUP + hh, DEC_BATCH, stride=KV_TILE_ROWS), :] for hh in range(HEADS_PER_GROUP)],
        axis=-1)
    a = jnp.dot((o_attn * _silu(z(COL_GA, GROUP_COLS))).astype(BF16), wpa_ref[...], preferred_element_type=F32)
    b = jnp.dot((conv_out * _silu(z(COL_GB, CONV_CH))).astype(BF16), wpb_ref[...], preferred_element_type=F32)
    pre = (_sigmoid(z(COL_MA, D_MODEL)) * a + _sigmoid(z(COL_MB, D_MODEL)) * b).astype(BF16)
    y = jnp.dot(pre, wo_ref[...], preferred_element_type=F32)
    y_ref[...] = _layer_norm(ALPHA * x_ref[...] + gate * y, lng_ref[...], lnb_ref[...])


def _sample_tail_call(x_s, mod_s, o_s, zr, state_t, conv_w, conv_b, cn_g, cn_b, w_pa, w_pb, w_o, ln_g, ln_b):
    full = lambda shape: pl.BlockSpec(shape, lambda i: (0,) * len(shape))
    args = (x_s, mod_s, o_s, zr, state_t, conv_w, conv_b, cn_g, cn_b, w_pa, w_pb, w_o, ln_g, ln_b)
    return pl.pallas_call(
        _sample_tail_kernel,
        grid=(1,),
        in_specs=[full(a.shape) for a in args],
        out_specs=[full((DEC_BATCH, D_MODEL)), full(state_t.shape)],
        out_shape=[jax.ShapeDtypeStruct((DEC_BATCH, D_MODEL), F32),
                   jax.ShapeDtypeStruct(state_t.shape, F32)],
        compiler_params=_params(("arbitrary",)),
        name="sample_tail",
    )(*args)


def kernel(x_prompt, x_sample, c_prompt, c_sample, cache_kv_w128, cache_kv_w512, cache_kv_w2048, state_conv,
           w_c, b_c, w_in, b_in, conv_w, conv_b, conv_norm_g, conv_norm_b, w_pa, w_pb, w_o, ln_g, ln_b):
    assert x_prompt.shape == (1, SEQ, D_MODEL) and x_sample.shape == (DEC_BATCH, 1, D_MODEL)
    assert w_in.shape == (D_MODEL, IN_COLS)
    caches = (cache_kv_w128, cache_kv_w512, cache_kv_w2048)
    for (window, _), c in zip(GROUPS, caches):
        assert c.shape == (DEC_BATCH, window, 2, HEADS_PER_GROUP, HEAD_DIM)

    w_pa_b, w_pb_b, w_o_b = w_pa.astype(BF16), w_pb.astype(BF16), w_o.astype(BF16)
    vec = lambda a: a.reshape(1, -1)
    b_row = vec(b_in)
    conv_args = (conv_w, vec(conv_b), vec(conv_norm_g), vec(conv_norm_b))
    out_args = (w_pa_b, w_pb_b, w_o_b, vec(ln_g), vec(ln_b))

    mod_p, mod_s = _mod_call(c_prompt, c_sample, w_c, b_c)
    x_s = x_sample.reshape(DEC_BATCH, D_MODEL)
    w_bf, z_s, qt0, qt1, qt2, kvs0, kvs1, kvs2 = _sample_proj_call(x_s, mod_s, w_in, b_row)

    x_p = x_prompt.reshape(SEQ, D_MODEL)
    (q0, k0, v0, q1, k1, v1, q2, k2, v2, kv0, kv1, kv2, b_conv, conv_p) = _front_call(
        x_p, mod_p, w_bf, b_row, *conv_args)
    flat = lambda a: a.reshape(SEQ, GROUP_COLS)
    attn = [_attn_call(0, q0, k0, v0), _attn_call(1, q1, k1, v1), _attn_call(2, flat(q2), flat(k2), flat(v2))]
    y_p = _tail_call(x_p, mod_p, [a[0] for a in attn], [a[1] for a in attn], b_conv, w_bf, b_row, *out_args)

    o_s = _sample_attn_call((qt0, qt1, qt2), (kvs0, kvs1, kvs2), caches)
    o_s = o_s.reshape(DEC_BATCH * KV_TILE_ROWS, HEAD_DIM)
    state_t = jnp.transpose(state_conv, (1, 0, 2))
    y_s, conv_s_t = _sample_tail_call(x_s, mod_s, o_s, z_s, state_t, *conv_args, *out_args)

    kv_shape_p = lambda keep: (1, keep, 2, HEADS_PER_GROUP, HEAD_DIM)
    kv_shape_s = (DEC_BATCH, 1, 2, HEADS_PER_GROUP, HEAD_DIM)
    return (y_p.reshape(1, SEQ, D_MODEL),
            y_s.reshape(DEC_BATCH, 1, D_MODEL),
            kv0.reshape(kv_shape_p(GROUPS[0][0])),
            kv1.reshape(kv_shape_p(GROUPS[1][0])),
            kv2.reshape(kv_shape_p(GROUPS[2][0])),
            conv_p.reshape(1, CONV_WIDTH - 1, CONV_CH),
            kvs0.reshape(kv_shape_s), kvs1.reshape(kv_shape_s), kvs2.reshape(kv_shape_s),
            jnp.transpose(conv_s_t, (1, 0, 2)))
```

```python
import functools

import jax
import jax.numpy as jnp
from jax import lax
from jax.experimental import pallas as pl
from jax.experimental.pallas import tpu as pltpu

F32 = jnp.float32
BF16 = jnp.bfloat16

D_MODEL = 1024
SEQ = 16384
DEC_BATCH = 128
DEPTH = 1
HEAD_DIM = 128
HEADS_PER_GROUP = 4
GROUPS = ((128, 1), (512, 4), (2048, 16))
N_GROUPS = len(GROUPS)
N_HEADS = N_GROUPS * HEADS_PER_GROUP
GROUP_COLS = HEADS_PER_GROUP * HEAD_DIM
ATTN_QKV = N_HEADS * HEAD_DIM
QKV_COLS = 3 * ATTN_QKV
CONV_CH = D_MODEL // 2
CONV_WIDTH = 31
COL_GA = QKV_COLS
COL_GLU_A = COL_GA + GROUP_COLS
COL_GLU_G = COL_GLU_A + CONV_CH
COL_GB = COL_GLU_G + CONV_CH
COL_MA = COL_GB + CONV_CH
COL_MB = COL_MA + D_MODEL
IN_COLS = COL_MB + D_MODEL
ALPHA = (2.0 * DEPTH) ** 0.25
LN_EPS = 1e-5
NEG = -1e30
Q_SCALE = HEAD_DIM ** -0.5
LOG2_E = 1.4426950408889634
Q_SCALE_LOG2 = Q_SCALE * LOG2_E
KEYS_PER_BLOCK = 128
KV_TILE_ROWS = 2 * HEADS_PER_GROUP

LANES = 128
MXU_COLS = 256
W_BLOCK = 512
ROW_TILE = 512
ATTN_STEP_ROWS = 2048
BLOCKS_PER_ATTN_STEP = ATTN_STEP_ROWS // KEYS_PER_BLOCK
CONV_ROW_CHUNK = 128
HIST_ROWS = 32
SAMPLE_BLOCK = 8
SAMPLE_UNROLL = 4
VMEM_LIMIT_BYTES = 56 * 1024 * 1024


def _slope(group, head):
    return 2.0 ** (-8.0 * (group * HEADS_PER_GROUP + head + 1) / N_HEADS)


def _sigmoid(x):
    return 1.0 / (1.0 + jnp.exp(-x))


def _silu(x):
    return x * _sigmoid(x)


def _layer_norm(x, g, b):
    mu = jnp.mean(x, axis=-1, keepdims=True)
    xc = x - mu
    var = jnp.mean(xc * xc, axis=-1, keepdims=True)
    return xc * lax.rsqrt(var + LN_EPS) * g + b


def _col_blocks(width):
    return [slice(b * MXU_COLS, (b + 1) * MXU_COLS) for b in range(width // MXU_COLS)]


def _row_stats(ref, width):
    total = sum(jnp.sum(ref[:, cols], axis=-1, keepdims=True) for cols in _col_blocks(width))
    mu = total / width
    sq = sum(jnp.sum(jnp.square(ref[:, cols] - mu), axis=-1, keepdims=True) for cols in _col_blocks(width))
    return mu, lax.rsqrt(sq / width + LN_EPS)


def _resident(shape):
    return pl.BlockSpec(shape, lambda *_: (0,) * len(shape), pipeline_mode=pl.Buffered(1))


def _w_cols(col0, width):
    assert col0 % width == 0
    return pl.BlockSpec((D_MODEL, width), lambda *_: (0, col0 // width), pipeline_mode=pl.Buffered(1))


def _row_vec(n):
    return pl.BlockSpec((1, n), lambda *_: (0, 0))


def _params(semantics):
    return pltpu.CompilerParams(dimension_semantics=semantics, vmem_limit_bytes=VMEM_LIMIT_BYTES)


def _store_kv_tiles(ref, k, v):
    rows = k.shape[0]
    for hh in range(HEADS_PER_GROUP):
        cols = slice(hh * HEAD_DIM, (hh + 1) * HEAD_DIM)
        ref[pl.ds(hh, rows, stride=KV_TILE_ROWS), :] = k[:, cols]
        ref[pl.ds(HEADS_PER_GROUP + hh, rows, stride=KV_TILE_ROWS), :] = v[:, cols]


def _mod_kernel(cp_ref, cs_ref, w_ref, b_ref, mp_ref, ms_ref):
    w = w_ref[...].astype(BF16)
    cp = jnp.broadcast_to(cp_ref[...], (8, D_MODEL)).astype(BF16)
    mp_ref[...] = jnp.dot(cp, w, preferred_element_type=F32)[0:1] + b_ref[...]
    ms_ref[...] = jnp.dot(cs_ref[...].astype(BF16), w, preferred_element_type=F32) + b_ref[...]


def _mod_call(c_prompt, c_sample, w_c, b_c):
    n_blk = 3
    return pl.pallas_call(
        _mod_kernel,
        grid=(n_blk,),
        in_specs=[
            pl.BlockSpec((1, D_MODEL), lambda n: (0, 0)),
            pl.BlockSpec((DEC_BATCH, D_MODEL), lambda n: (0, 0)),
            pl.BlockSpec((D_MODEL, D_MODEL), lambda n: (0, n)),
            pl.BlockSpec((1, D_MODEL), lambda n: (0, n)),
        ],
        out_specs=[
            pl.BlockSpec((1, D_MODEL), lambda n: (0, n)),
            pl.BlockSpec((DEC_BATCH, D_MODEL), lambda n: (0, n)),
        ],
        out_shape=[
            jax.ShapeDtypeStruct((1, 3 * D_MODEL), F32),
            jax.ShapeDtypeStruct((DEC_BATCH, 3 * D_MODEL), F32),
        ],
        compiler_params=_params(("arbitrary",)),
        name="adaln_mod",
    )(c_prompt, c_sample, w_c, b_c.reshape(1, 3 * D_MODEL))


N_ROW_TILES = SEQ // ROW_TILE
KV2_TILES = GROUPS[2][0] // ROW_TILE


N_QKV_BLOCKS = QKV_COLS // GROUP_COLS
CONV_CHUNKS = (CONV_CH // LANES) * (ROW_TILE // CONV_ROW_CHUNK)
CONV_CHUNKS_PER_TRIP = 2
assert CONV_CHUNKS == CONV_CHUNKS_PER_TRIP * (N_QKV_BLOCKS - 1)


def _front_kernel(x_ref, mod_ref, w_ref, wa_ref, wg_ref, wgb_ref, b_ref, cw_ref, cb_ref, cg_ref, cbeta_ref,
                  qkv_ref, kv0_ref, kv1_ref, kv2_ref, bin_ref, convp_ref,
                  hs_ref, hq_ref, lhs_ref, uext_ref, cws_ref, cacc_ref, tail_ref):
    i = pl.program_id(0)
    shift = mod_ref[:, 0:D_MODEL]
    scale = mod_ref[:, D_MODEL:2 * D_MODEL]
    h = x_ref[...] * (1.0 + scale) + shift
    lhs_ref[0] = h.astype(BF16)

    n_slab = D_MODEL // LANES
    d1, d2 = GROUPS[1][1], GROUPS[2][1]
    assert d2 == d1 * d1
    r1, r2 = ROW_TILE // d1, ROW_TILE // d2
    for c in range(n_slab):
        cols = slice(c * LANES, (c + 1) * LANES)
        hs_ref[c] = h[:, cols]
        for b in range(d1):
            rows_b = hs_ref[c, pl.ds(b, r1, stride=d1), :]
            hq_ref[c, b * r1:(b + 1) * r1, :] = rows_b
            lhs_ref[1, b * r1:(b + 1) * r1, cols] = rows_b.astype(BF16)
        for r in range(d2):
            a, b = r // d1, r % d1
            lhs_ref[2, r * r2:(r + 1) * r2, cols] = hq_ref[c, pl.ds(b * r1 + a, r2, stride=d1), :].astype(BF16)

    def gate_proj(wx_ref, col0, cols):
        bcols = slice(col0 + cols.start, col0 + cols.stop)
        return jnp.dot(lhs_ref[0], wx_ref[:, cols], preferred_element_type=F32) + b_ref[:, bcols]

    n_cslab = CONV_CH // LANES

    @pl.when(i == 0)
    def _():
        uext_ref[:, 0:HIST_ROWS, :] = jnp.zeros((n_cslab, HIST_ROWS, LANES), F32)
        for c in range(n_cslab):
            cws_ref[c, 0:CONV_WIDTH, :] = cw_ref[:, c * LANES:(c + 1) * LANES]
            cws_ref[c, CONV_WIDTH:CONV_WIDTH + 1, :] = cb_ref[:, c * LANES:(c + 1) * LANES]

    slabs_per_blk = MXU_COLS // LANES
    for b, cols in enumerate(_col_blocks(CONV_CH)):
        u = gate_proj(wa_ref, COL_GLU_A, cols) * _sigmoid(gate_proj(wg_ref, COL_GLU_G, cols))
        convp_ref[:, cols] = u[ROW_TILE - (CONV_WIDTH - 1):, :]
        for s in range(slabs_per_blk):
            uext_ref[b * slabs_per_blk + s, HIST_ROWS:, :] = u[:, s * LANES:(s + 1) * LANES]

    lead = HIST_ROWS - (CONV_WIDTH - 1)
    chunks_per_slab = ROW_TILE // CONV_ROW_CHUNK
    chunk_shift = chunks_per_slab.bit_length() - 1
    assert chunks_per_slab == 1 << chunk_shift

    def conv_chunk(idx):
        c = lax.shift_right_logical(idx, chunk_shift)
        r0 = pl.multiple_of(jnp.bitwise_and(idx, chunks_per_slab - 1) * CONV_ROW_CHUNK, CONV_ROW_CHUNK)
        acc = jnp.broadcast_to(cws_ref[c, CONV_WIDTH:CONV_WIDTH + 1, :], (CONV_ROW_CHUNK, LANES))
        for j in range(CONV_WIDTH):
            acc = acc + uext_ref[c, pl.ds(r0 + lead + j, CONV_ROW_CHUNK), :] * cws_ref[c, j:j + 1, :]

        def store():
            cacc_ref[c, pl.ds(r0, CONV_ROW_CHUNK), :] = acc
        return store

    def qkv_block(n):
        col0 = pl.multiple_of(n * GROUP_COLS, GROUP_COLS)
        val = jnp.dot(lhs_ref[n % N_GROUPS], w_ref[:, pl.ds(col0, GROUP_COLS)], preferred_element_type=F32)
        val = val + b_ref[:, pl.ds(col0, GROUP_COLS)]

        def store():
            tail_ref[n] = val[ROW_TILE - GROUPS[0][0]:, :]
            qkv_ref[n] = (val * jnp.where(n < N_GROUPS, Q_SCALE_LOG2, 1.0)).astype(BF16)
        return store

    def trip(n, carry):
        stores = [conv_chunk(n * CONV_CHUNKS_PER_TRIP + k) for k in range(CONV_CHUNKS_PER_TRIP)]
        stores.append(qkv_block(n))
        for st in stores:
            st()
        return carry

    lax.fori_loop(0, N_QKV_BLOCKS - 1, trip, 0)
    qkv_block(N_QKV_BLOCKS - 1)()

    for c in range(n_cslab):
        uext_ref[c, 0:HIST_ROWS, :] = uext_ref[c, ROW_TILE:ROW_TILE + HIST_ROWS, :]
    _store_kv_tiles(kv0_ref, tail_ref[N_GROUPS], tail_ref[2 * N_GROUPS])

    total = sum(jnp.sum(cacc_ref[c], axis=-1, keepdims=True) for c in range(n_cslab))
    mu = total / CONV_CH
    sq = sum(jnp.sum(jnp.square(cacc_ref[c] - mu), axis=-1, keepdims=True) for c in range(n_cslab))
    rstd = lax.rsqrt(sq / CONV_CH + LN_EPS)
    for b, cols in enumerate(_col_blocks(CONV_CH)):
        conv = jnp.concatenate([cacc_ref[b * slabs_per_blk + s] for s in range(slabs_per_blk)], axis=-1)
        normed = (conv - mu) * rstd * cg_ref[:, cols] + cbeta_ref[:, cols]
        bin_ref[:, cols] = (_silu(normed) * _silu(gate_proj(wgb_ref, COL_GB, cols))).astype(BF16)

    def proj_nat(blk):
        cols = slice(blk * GROUP_COLS, (blk + 1) * GROUP_COLS)
        return jnp.dot(lhs_ref[0], w_ref[:, cols], preferred_element_type=F32) + b_ref[:, cols]

    @pl.when(i == N_ROW_TILES - 1)
    def _():
        _store_kv_tiles(kv1_ref, proj_nat(N_GROUPS + 1), proj_nat(2 * N_GROUPS + 1))

    @pl.when(i >= N_ROW_TILES - KV2_TILES)
    def _():
        _store_kv_tiles(kv2_ref, proj_nat(N_GROUPS + 2), proj_nat(2 * N_GROUPS + 2))


def _front_call(x, mod_p, w_bf, b_in, conv_w, conv_b, cn_g, cn_b):
    nat = pl.BlockSpec((ROW_TILE, GROUP_COLS), lambda i: (i, 0))
    first_kv2_tile = N_ROW_TILES - KV2_TILES
    kv_shape = lambda g: jax.ShapeDtypeStruct((GROUPS[g][0] * KV_TILE_ROWS, HEAD_DIM), F32)
    return pl.pallas_call(
        _front_kernel,
        grid=(N_ROW_TILES,),
        in_specs=[
            pl.BlockSpec((ROW_TILE, D_MODEL), lambda i: (i, 0)),
            _row_vec(3 * D_MODEL),
            _w_cols(0, QKV_COLS), _w_cols(COL_GLU_A, W_BLOCK), _w_cols(COL_GLU_G, W_BLOCK), _w_cols(COL_GB, W_BLOCK),
            _row_vec(IN_COLS),
            pl.BlockSpec((CONV_WIDTH, CONV_CH), lambda i: (0, 0)),
            _row_vec(CONV_CH), _row_vec(CONV_CH), _row_vec(CONV_CH),
        ],
        out_specs=[pl.BlockSpec((N_QKV_BLOCKS, ROW_TILE, GROUP_COLS), lambda i: (0, i, 0)),
                   pl.BlockSpec((GROUPS[0][0] * KV_TILE_ROWS, HEAD_DIM), lambda i: (0, 0)),
                   pl.BlockSpec((GROUPS[1][0] * KV_TILE_ROWS, HEAD_DIM), lambda i: (0, 0)),
                   pl.BlockSpec((ROW_TILE * KV_TILE_ROWS, HEAD_DIM),
                                lambda i: (jnp.maximum(i - first_kv2_tile, 0), 0)),
                   nat,
                   pl.BlockSpec((CONV_WIDTH - 1, CONV_CH), lambda i: (0, 0))],
        out_shape=[jax.ShapeDtypeStruct((N_QKV_BLOCKS, SEQ, GROUP_COLS), BF16),
                   kv_shape(0), kv_shape(1), kv_shape(2),
                   jax.ShapeDtypeStruct((SEQ, GROUP_COLS), BF16),
                   jax.ShapeDtypeStruct((CONV_WIDTH - 1, CONV_CH), F32)],
        scratch_shapes=[
            pltpu.VMEM((D_MODEL // LANES, ROW_TILE, LANES), F32),
            pltpu.VMEM((D_MODEL // LANES, ROW_TILE, LANES), F32),
            pltpu.VMEM((N_GROUPS, ROW_TILE, D_MODEL), BF16),
            pltpu.VMEM((CONV_CH // LANES, HIST_ROWS + ROW_TILE, LANES), F32),
            pltpu.VMEM((CONV_CH // LANES, CONV_WIDTH + 1, LANES), F32),
            pltpu.VMEM((CONV_CH // LANES, ROW_TILE, LANES), F32),
            pltpu.VMEM((N_QKV_BLOCKS, GROUPS[0][0], GROUP_COLS), F32),
        ],
        compiler_params=_params(("arbitrary",)),
        name="prompt_front",
    )(x, mod_p, w_bf, w_bf, w_bf, w_bf, b_in, conv_w, conv_b, cn_g, cn_b)


def _attn_kernel(q_ref, k_ref, v_ref, kp_ref, vp_ref, o_ref, lse_ref, bias_ref, *scr, group):
    dil = GROUPS[group][1]
    tail = KEYS_PER_BLOCK * dil
    t = pl.program_id(0)

    @pl.when(t == 0)
    def _():
        row = lax.broadcasted_iota(jnp.int32, (KEYS_PER_BLOCK, 2 * KEYS_PER_BLOCK), 0)
        col = lax.broadcasted_iota(jnp.int32, (KEYS_PER_BLOCK, 2 * KEYS_PER_BLOCK), 1)
        dist = row - col + KEYS_PER_BLOCK
        valid = (dist >= 0) & (dist <= KEYS_PER_BLOCK)
        valid_first = valid & (col >= KEYS_PER_BLOCK)
        distf = (dil * dist).astype(F32)
        for hh in range(HEADS_PER_GROUP):
            b = -(_slope(group, hh) * distf) * LOG2_E
            bias_ref[0, hh] = jnp.where(valid, b, NEG)
            bias_ref[1, hh] = jnp.where(valid_first, b, NEG)

    first_step = (t == 0).astype(jnp.int32)
    head_lane = lax.broadcasted_iota(jnp.int32, (KEYS_PER_BLOCK, LANES), 1)
    if dil > 1:
        oscr_ref, lscr_ref = scr
    def block(ref, j):
        if tail < ATTN_STEP_ROWS:
            return ref[j * KEYS_PER_BLOCK:(j + 1) * KEYS_PER_BLOCK, :]
        piece = ROW_TILE // dil
        return jnp.concatenate([ref[tt * ROW_TILE + j * piece:tt * ROW_TILE + (j + 1) * piece, :]
                                for tt in range(ATTN_STEP_ROWS // ROW_TILE)], axis=0)

    for j in range(BLOCKS_PER_ATTN_STEP):
        qb = block(q_ref, j)
        kself, vself = block(k_ref, j), block(v_ref, j)
        if j < dil:
            kprev, vprev = block(kp_ref, j), block(vp_ref, j)
        else:
            kprev, vprev = block(k_ref, j - dil), block(v_ref, j - dil)
        first = first_step if j < dil else 0
        start = (j // dil) * tail + (j % dil)
        rows = pl.ds(start, KEYS_PER_BLOCK, stride=dil) if dil > 1 else pl.ds(start, KEYS_PER_BLOCK)
        m_tile = jnp.zeros((KEYS_PER_BLOCK, LANES), F32)
        l_tile = jnp.ones((KEYS_PER_BLOCK, LANES), F32)
        for hh in range(HEADS_PER_GROUP):
            cols = slice(hh * HEAD_DIM, (hh + 1) * HEAD_DIM)
            kc = jnp.concatenate([kprev[:, cols], kself[:, cols]], axis=0)
            vc = jnp.concatenate([vprev[:, cols], vself[:, cols]], axis=0)
            s = lax.dot_general(qb[:, cols], kc, (((1,), (1,)), ((), ())), preferred_element_type=F32)
            s = s + bias_ref[first, hh]
            m = jnp.max(s, axis=-1, keepdims=True)
            p = jnp.exp2(s - m)
            l = jnp.sum(p, axis=-1, keepdims=True)
            acc = jnp.dot(p.astype(BF16), vc, preferred_element_type=F32)
            if dil > 1:
                oscr_ref[hh, rows, :] = acc
            else:
                o_ref[rows, cols] = acc.astype(BF16)
            m_tile = jnp.where(head_lane == hh, m, m_tile)
            l_tile = jnp.where(head_lane == hh, l, l_tile)
        if dil > 1:
            lscr_ref[0, rows, :] = m_tile
            lscr_ref[1, rows, :] = l_tile
        else:
            lse_ref[rows, 0:LANES] = m_tile
            lse_ref[rows, LANES:] = l_tile

    if dil > 1:
        for hh in range(HEADS_PER_GROUP):
            o_ref[:, hh * HEAD_DIM:(hh + 1) * HEAD_DIM] = oscr_ref[hh].astype(BF16)
        lse_ref[:, 0:LANES] = lscr_ref[0]
        lse_ref[:, LANES:] = lscr_ref[1]


def _attn_call(group, qkv):
    dil = GROUPS[group][1]
    tail = KEYS_PER_BLOCK * dil
    steps = SEQ // ATTN_STEP_ROWS
    tails_per_step = ATTN_STEP_ROWS // tail
    stacked = lambda blk: pl.BlockSpec((None, ATTN_STEP_ROWS, GROUP_COLS), lambda t: (blk, t, 0))
    prev_of = lambda blk: pl.BlockSpec((None, tail, GROUP_COLS),
                                       lambda t: (blk, jnp.maximum(t * tails_per_step - 1, 0), 0))
    q_blk, k_blk, v_blk = group, N_GROUPS + group, 2 * N_GROUPS + group
    cur = pl.BlockSpec((ATTN_STEP_ROWS, GROUP_COLS), lambda t: (t, 0))
    lse_blk = pl.BlockSpec((ATTN_STEP_ROWS, 2 * LANES), lambda t: (t, 0))
    scratch = [pltpu.VMEM((2, HEADS_PER_GROUP, KEYS_PER_BLOCK, 2 * KEYS_PER_BLOCK), F32)]
    if dil > 1:
        scratch += [pltpu.VMEM((HEADS_PER_GROUP, ATTN_STEP_ROWS, HEAD_DIM), F32),
                    pltpu.VMEM((2, ATTN_STEP_ROWS, LANES), F32)]
    return pl.pallas_call(
        functools.partial(_attn_kernel, group=group),
        grid=(steps,),
        in_specs=[stacked(q_blk), stacked(k_blk), stacked(v_blk), prev_of(k_blk), prev_of(v_blk)],
        out_specs=[cur, lse_blk],
        out_shape=[jax.ShapeDtypeStruct((SEQ, GROUP_COLS), BF16), jax.ShapeDtypeStruct((SEQ, 2 * LANES), F32)],
        scratch_shapes=scratch,
        compiler_params=_params(("arbitrary",)),
        name=f"prompt_attn_g{group}",
    )(qkv, qkv, qkv, qkv, qkv)


def _tail_kernel(x_ref, mod_ref, o0_ref, o1_ref, o2_ref, l0_ref, l1_ref, l2_ref, bin_ref,
                 wga_ref, wma0_ref, wma1_ref, wmb0_ref, wmb1_ref, b_ref, wpa_ref, wpb_ref, wo_ref, lng_ref, lnb_ref,
                 y_ref,
                 hb_ref, ain_ref, pre_ref, res_ref):
    shift = mod_ref[:, 0:D_MODEL]
    scale = mod_ref[:, D_MODEL:2 * D_MODEL]
    hb_ref[...] = (x_ref[...] * (1.0 + scale) + shift).astype(BF16)

    def gate_proj(wx_refs, col0, cols):
        wx_ref = wx_refs[cols.start // W_BLOCK]
        wcols = slice(cols.start % W_BLOCK, cols.start % W_BLOCK + MXU_COLS)
        bcols = slice(col0 + cols.start, col0 + cols.stop)
        return jnp.dot(hb_ref[...], wx_ref[:, wcols], preferred_element_type=F32) + b_ref[:, bcols]

    stats = (l0_ref, l1_ref, l2_ref)
    maxes = [st[:, 0:LANES] for st in stats]
    m = jnp.maximum(jnp.maximum(maxes[0], maxes[1]), maxes[2])
    es = [jnp.exp2(mg - m) for mg in maxes]
    inv = 1.0 / sum(e * st[:, LANES:] for e, st in zip(es, stats))
    weights = tuple(e * inv for e in es)
    o_refs = (o0_ref, o1_ref, o2_ref)
    for cols in _col_blocks(GROUP_COLS):
        heads = []
        for hh in range(cols.start // HEAD_DIM, cols.stop // HEAD_DIM):
            hcols = slice(hh * HEAD_DIM, (hh + 1) * HEAD_DIM)
            heads.append(sum(w[:, hh:hh + 1] * o_ref[:, hcols].astype(F32) for w, o_ref in zip(weights, o_refs)))
        ga = gate_proj((wga_ref,), COL_GA, cols)
        ain_ref[:, cols] = (jnp.concatenate(heads, axis=-1) * _silu(ga)).astype(BF16)

    for cols in _col_blocks(D_MODEL):
        a = jnp.dot(ain_ref[...], wpa_ref[:, cols], preferred_element_type=F32)
        bb = jnp.dot(bin_ref[...], wpb_ref[:, cols], preferred_element_type=F32)
        ma = gate_proj((wma0_ref, wma1_ref), COL_MA, cols)
        mb = gate_proj((wmb0_ref, wmb1_ref), COL_MB, cols)
        pre_ref[:, cols] = (_sigmoid(ma) * a + _sigmoid(mb) * bb).astype(BF16)
    for cols in _col_blocks(D_MODEL):
        gate = mod_ref[:, 2 * D_MODEL + cols.start:2 * D_MODEL + cols.stop]
        y = jnp.dot(pre_ref[...], wo_ref[:, cols], preferred_element_type=F32)
        res_ref[:, cols] = ALPHA * x_ref[:, cols] + gate * y
    mu, rstd = _row_stats(res_ref, D_MODEL)
    for cols in _col_blocks(D_MODEL):
        y_ref[:, cols] = (res_ref[:, cols] - mu) * rstd * lng_ref[:, cols] + lnb_ref[:, cols]


def _tail_call(x, mod_p, outs, lses, b_conv, w_bf, b_in, w_pa, w_pb, w_o, ln_g, ln_b):
    o_blk = pl.BlockSpec((ROW_TILE, GROUP_COLS), lambda i: (i, 0))
    lse_blk = pl.BlockSpec((ROW_TILE, 2 * LANES), lambda i: (i, 0))
    return pl.pallas_call(
        _tail_kernel,
        grid=(N_ROW_TILES,),
        in_specs=[
            pl.BlockSpec((ROW_TILE, D_MODEL), lambda i: (i, 0)),
            _row_vec(3 * D_MODEL),
            o_blk, o_blk, o_blk, lse_blk, lse_blk, lse_blk,
            pl.BlockSpec((ROW_TILE, CONV_CH), lambda i: (i, 0)),
            _w_cols(COL_GA, W_BLOCK),
            _w_cols(COL_MA, W_BLOCK), _w_cols(COL_MA + W_BLOCK, W_BLOCK),
            _w_cols(COL_MB, W_BLOCK), _w_cols(COL_MB + W_BLOCK, W_BLOCK),
            _row_vec(IN_COLS),
            _resident((GROUP_COLS, D_MODEL)),
            _resident((CONV_CH, D_MODEL)),
            _resident((D_MODEL, D_MODEL)),
            _row_vec(D_MODEL), _row_vec(D_MODEL),
        ],
        out_specs=pl.BlockSpec((ROW_TILE, D_MODEL), lambda i: (i, 0)),
        out_shape=jax.ShapeDtypeStruct((SEQ, D_MODEL), F32),
        scratch_shapes=[
            pltpu.VMEM((ROW_TILE, D_MODEL), BF16),
            pltpu.VMEM((ROW_TILE, GROUP_COLS), BF16),
            pltpu.VMEM((ROW_TILE, D_MODEL), BF16),
            pltpu.VMEM((ROW_TILE, D_MODEL), F32),
        ],
        compiler_params=_params(("arbitrary",)),
        name="prompt_tail",
    )(x, mod_p, *outs, *lses, b_conv, w_bf, w_bf, w_bf, w_bf, w_bf, b_in, w_pa, w_pb, w_o, ln_g, ln_b)


N_W_BLOCKS = 4
CAST_BLOCK = IN_COLS // N_W_BLOCKS
assert CAST_BLOCK * N_W_BLOCKS == IN_COLS and CAST_BLOCK % LANES == 0


def _sample_proj_kernel(x_ref, mod_ref, w_ref, b_ref,
                        wbf_ref, z_ref, qt0_ref, qt1_ref, qt2_ref, kv0_ref, kv1_ref, kv2_ref):
    n = pl.program_id(0)
    shift = mod_ref[:, 0:D_MODEL]
    scale = mod_ref[:, D_MODEL:2 * D_MODEL]
    hb = (x_ref[...] * (1.0 + scale) + shift).astype(BF16)
    wb = w_ref[...].astype(BF16)
    wbf_ref[...] = wb
    col0 = pl.multiple_of(n * CAST_BLOCK, LANES)
    z_ref[:, pl.ds(col0, CAST_BLOCK)] = (jnp.dot(hb, wb, preferred_element_type=F32)
                                         + b_ref[:, pl.ds(col0, CAST_BLOCK)])

    @pl.when(n == N_W_BLOCKS - 1)
    def _():
        zeros = jnp.zeros((DEC_BATCH, GROUP_COLS), F32)
        blk = lambda i: z_ref[:, i * GROUP_COLS:(i + 1) * GROUP_COLS]
        for g, (qt_ref, kv_ref) in enumerate(((qt0_ref, kv0_ref), (qt1_ref, kv1_ref), (qt2_ref, kv2_ref))):
            _store_kv_tiles(qt_ref, blk(g) * Q_SCALE, zeros)
            _store_kv_tiles(kv_ref, blk(N_GROUPS + g), blk(2 * N_GROUPS + g))


def _sample_proj_call(x_s, mod_s, w_in, b_in):
    full = lambda shape: pl.BlockSpec(shape, lambda n: (0,) * len(shape))
    tile_rows = DEC_BATCH * KV_TILE_ROWS
    tiles = jax.ShapeDtypeStruct((tile_rows, HEAD_DIM), F32)
    w_blk = pl.BlockSpec((D_MODEL, CAST_BLOCK), lambda n: (0, n))
    return pl.pallas_call(
        _sample_proj_kernel,
        grid=(N_W_BLOCKS,),
        in_specs=[full((DEC_BATCH, D_MODEL)), full((DEC_BATCH, 3 * D_MODEL)), w_blk, _row_vec(IN_COLS)],
        out_specs=[w_blk, full((DEC_BATCH, IN_COLS))] + [full((tile_rows, HEAD_DIM))] * 6,
        out_shape=[jax.ShapeDtypeStruct((D_MODEL, IN_COLS), BF16), jax.ShapeDtypeStruct((DEC_BATCH, IN_COLS), F32)]
                  + [tiles] * 6,
        compiler_params=_params(("arbitrary",)),
        name="sample_proj",
    )(x_s, mod_s, w_in, b_in)


def _sample_attn_kernel(qt0_ref, qt1_ref, qt2_ref, kv0_ref, kv1_ref, kv2_ref, c0_ref, c1_ref, c2_ref, o_ref,
                        bias_ref):
    qt_refs = (qt0_ref, qt1_ref, qt2_ref)
    kv_refs = (kv0_ref, kv1_ref, kv2_ref)
    c_refs = (c0_ref, c1_ref, c2_ref)
    half = HEADS_PER_GROUP
    tile = (KV_TILE_ROWS, HEAD_DIM)
    keys_tile = (KEYS_PER_BLOCK,) + tile

    @pl.when(pl.program_id(0) == 0)
    def _():
        key = lax.broadcasted_iota(jnp.int32, keys_tile, 0)
        sub = lax.broadcasted_iota(jnp.int32, keys_tile, 1)
        steps_back = (KEYS_PER_BLOCK - key).astype(F32)
        for g in range(N_GROUPS):
            slope_rows = jnp.zeros(keys_tile, F32)
            for hh in range(HEADS_PER_GROUP):
                slope_rows = jnp.where(sub == half + hh, _slope(g, hh), slope_rows)
            bias_ref[g] = -(slope_rows * (GROUPS[g][1] * steps_back)) * LOG2_E

    def body(b, carry):
        outs, lses = [], []
        for g in range(N_GROUPS):
            x = c_refs[g][b]
            qt = qt_refs[g][b] * LOG2_E
            kvn = kv_refs[g][b]
            s = jnp.broadcast_to(jnp.sum(x * qt[None], axis=-1, keepdims=True), keys_tile)
            s = pltpu.roll(s, half, axis=1) + bias_ref[g]
            s_self = jnp.broadcast_to(jnp.sum(qt * kvn, axis=-1, keepdims=True), tile)
            s_self = pltpu.roll(s_self, half, axis=0)
            m = jnp.maximum(jnp.max(s, axis=0), s_self)
            p = jnp.exp2(s - m[None])
            p_self = jnp.exp2(s_self - m)
            l = jnp.sum(p, axis=0) + p_self
            acc = jnp.sum(x * p, axis=0) + p_self * kvn
            outs.append(acc / l)
            lses.append(m + jnp.log2(l))
        mx = jnp.maximum(jnp.maximum(lses[0], lses[1]), lses[2])
        es = [jnp.exp2(ls - mx) for ls in lses]
        o_ref[b] = (es[0] * outs[0] + es[1] * outs[1] + es[2] * outs[2]) / (es[0] + es[1] + es[2])
        return carry

    lax.fori_loop(0, SAMPLE_BLOCK, body, 0, unroll=SAMPLE_UNROLL)


def _sample_attn_call(qts, kvs, caches):
    views = [c.reshape(DEC_BATCH, KEYS_PER_BLOCK, GROUPS[g][1], KV_TILE_ROWS, HEAD_DIM) for g, c in enumerate(caches)]
    tile3 = lambda a: a.reshape(DEC_BATCH, KV_TILE_ROWS, HEAD_DIM)
    tile_blk = pl.BlockSpec((SAMPLE_BLOCK, KV_TILE_ROWS, HEAD_DIM), lambda i: (i, 0, 0))
    cache_blk = pl.BlockSpec((SAMPLE_BLOCK, KEYS_PER_BLOCK, None, KV_TILE_ROWS, HEAD_DIM), lambda i: (i, 0, 0, 0, 0))
    return pl.pallas_call(
        _sample_attn_kernel,
        grid=(DEC_BATCH // SAMPLE_BLOCK,),
        in_specs=[tile_blk] * 6 + [cache_blk] * 3,
        out_specs=tile_blk,
        out_shape=jax.ShapeDtypeStruct((DEC_BATCH, KV_TILE_ROWS, HEAD_DIM), F32),
        scratch_shapes=[pltpu.VMEM((N_GROUPS, KEYS_PER_BLOCK, KV_TILE_ROWS, HEAD_DIM), F32)],
        compiler_params=_params(("arbitrary",)),
        name="sample_attn",
    )(*[tile3(a) for a in qts], *[tile3(a) for a in kvs], *views)


def _sample_tail_kernel(x_ref, mod_ref, o_ref, zr_ref, st_ref, cw_ref, cb_ref, cg_ref, cbeta_ref,
                        wpa_ref, wpb_ref, wo_ref, lng_ref, lnb_ref, y_ref, convs_ref):
    gate = mod_ref[:, 2 * D_MODEL:3 * D_MODEL]
    z = lambda col0, width: zr_ref[:, col0:col0 + width]
    u = z(COL_GLU_A, CONV_CH) * _sigmoid(z(COL_GLU_G, CONV_CH))
    hist = CONV_WIDTH - 1
    acc = cb_ref[...] + u * cw_ref[hist:hist + 1, :]
    for j in range(hist):
        acc = acc + st_ref[j] * cw_ref[j:j + 1, :]
    conv_out = _silu(_layer_norm(acc, cg_ref[...], cbeta_ref[...]))
    for j in range(hist - 1):
        convs_ref[j] = st_ref[j + 1]
    convs_ref[hist - 1] = u
    o_attn = jnp.concatenate(
        [o_ref[pl.ds(HEADS_PER_GROUP + hh, DEC_BATCH, stride=KV_TILE_ROWS), :] for hh in range(HEADS_PER_GROUP)],
        axis=-1)
    a = jnp.dot((o_attn * _silu(z(COL_GA, GROUP_COLS))).astype(BF16), wpa_ref[...], preferred_element_type=F32)
    b = jnp.dot((conv_out * _silu(z(COL_GB, CONV_CH))).astype(BF16), wpb_ref[...], preferred_element_type=F32)
    pre = (_sigmoid(z(COL_MA, D_MODEL)) * a + _sigmoid(z(COL_MB, D_MODEL)) * b).astype(BF16)
    y = jnp.dot(pre, wo_ref[...], preferred_element_type=F32)
    y_ref[...] = _layer_norm(ALPHA * x_ref[...] + gate * y, lng_ref[...], lnb_ref[...])


def _sample_tail_call(x_s, mod_s, o_s, zr, state_t, conv_w, conv_b, cn_g, cn_b, w_pa, w_pb, w_o, ln_g, ln_b):
    full = lambda shape: pl.BlockSpec(shape, lambda i: (0,) * len(shape))
    args = (x_s, mod_s, o_s, zr, state_t, conv_w, conv_b, cn_g, cn_b, w_pa, w_pb, w_o, ln_g, ln_b)
    return pl.pallas_call(
        _sample_tail_kernel,
        grid=(1,),
        in_specs=[full(a.shape) for a in args],
        out_specs=[full((DEC_BATCH, D_MODEL)), full(state_t.shape)],
        out_shape=[jax.ShapeDtypeStruct((DEC_BATCH, D_MODEL), F32),
                   jax.ShapeDtypeStruct(state_t.shape, F32)],
        compiler_params=_params(("arbitrary",)),
        name="sample_tail",
    )(*args)


def kernel(x_prompt, x_sample, c_prompt, c_sample, cache_kv_w128, cache_kv_w512, cache_kv_w2048, state_conv,
           w_c, b_c, w_in, b_in, conv_w, conv_b, conv_norm_g, conv_norm_b, w_pa, w_pb, w_o, ln_g, ln_b):
    assert x_prompt.shape == (1, SEQ, D_MODEL) and x_sample.shape == (DEC_BATCH, 1, D_MODEL)
    assert w_in.shape == (D_MODEL, IN_COLS)
    caches = (cache_kv_w128, cache_kv_w512, cache_kv_w2048)
    for (window, _), c in zip(GROUPS, caches):
        assert c.shape == (DEC_BATCH, window, 2, HEADS_PER_GROUP, HEAD_DIM)

    w_pa_b, w_pb_b, w_o_b = w_pa.astype(BF16), w_pb.astype(BF16), w_o.astype(BF16)
    vec = lambda a: a.reshape(1, -1)
    b_row = vec(b_in)
    conv_args = (conv_w, vec(conv_b), vec(conv_norm_g), vec(conv_norm_b))
    out_args = (w_pa_b, w_pb_b, w_o_b, vec(ln_g), vec(ln_b))

    mod_p, mod_s = _mod_call(c_prompt, c_sample, w_c, b_c)
    x_s = x_sample.reshape(DEC_BATCH, D_MODEL)
    w_bf, z_s, qt0, qt1, qt2, kvs0, kvs1, kvs2 = _sample_proj_call(x_s, mod_s, w_in, b_row)

    x_p = x_prompt.reshape(SEQ, D_MODEL)
    qkv, kv0, kv1, kv2, b_conv, conv_p = _front_call(x_p, mod_p, w_bf, b_row, *conv_args)
    attn = [_attn_call(g, qkv) for g in range(N_GROUPS)]
    y_p = _tail_call(x_p, mod_p, [a[0] for a in attn], [a[1] for a in attn], b_conv, w_bf, b_row, *out_args)

    o_s = _sample_attn_call((qt0, qt1, qt2), (kvs0, kvs1, kvs2), caches)
    o_s = o_s.reshape(DEC_BATCH * KV_TILE_ROWS, HEAD_DIM)
    state_t = jnp.transpose(state_conv, (1, 0, 2))
    y_s, conv_s_t = _sample_tail_call(x_s, mod_s, o_s, z_s, state_t, *conv_args, *out_args)

    kv_shape_p = lambda keep: (1, keep, 2, HEADS_PER_GROUP, HEAD_DIM)
    kv_shape_s = (DEC_BATCH, 1, 2, HEADS_PER_GROUP, HEAD_DIM)
    return (y_p.reshape(1, SEQ, D_MODEL),
            y_s.reshape(DEC_BATCH, 1, D_MODEL),
            kv0.reshape(kv_shape_p(GROUPS[0][0])),
            kv1.reshape(kv_shape_p(GROUPS[1][0])),
            kv2.reshape(kv_shape_p(GROUPS[2][0])),
            conv_p.reshape(1, CONV_WIDTH - 1, CONV_CH),
            kvs0.reshape(kv_shape_s), kvs1.reshape(kv_shape_s), kvs2.reshape(kv_shape_s),
            jnp.transpose(conv_s_t, (1, 0, 2)))
```

```python
import functools

import jax
import jax.numpy as jnp
from jax import lax
from jax.experimental import pallas as pl
from jax.experimental.pallas import tpu as pltpu

F32 = jnp.float32
BF16 = jnp.bfloat16

D_MODEL = 1024
SEQ = 16384
DEC_BATCH = 128
DEPTH = 1
HEAD_DIM = 128
HEADS_PER_GROUP = 4
GROUPS = ((128, 1), (512, 4), (2048, 16))
N_GROUPS = len(GROUPS)
N_HEADS = N_GROUPS * HEADS_PER_GROUP
GROUP_COLS = HEADS_PER_GROUP * HEAD_DIM
ATTN_QKV = N_HEADS * HEAD_DIM
QKV_COLS = 3 * ATTN_QKV
CONV_CH = D_MODEL // 2
CONV_WIDTH = 31
COL_GA = QKV_COLS
COL_GLU_A = COL_GA + GROUP_COLS
COL_GLU_G = COL_GLU_A + CONV_CH
COL_GB = COL_GLU_G + CONV_CH
COL_MA = COL_GB + CONV_CH
COL_MB = COL_MA + D_MODEL
IN_COLS = COL_MB + D_MODEL
ALPHA = (2.0 * DEPTH) ** 0.25
LN_EPS = 1e-5
NEG = -1e30
Q_SCALE = HEAD_DIM ** -0.5
LOG2_E = 1.4426950408889634
Q_SCALE_LOG2 = Q_SCALE * LOG2_E
KEYS_PER_BLOCK = 128
KV_TILE_ROWS = 2 * HEADS_PER_GROUP

LANES = 128
MXU_COLS = 256
W_BLOCK = 512
ROW_TILE = 512
ATTN_STEP_ROWS = 2048
BLOCKS_PER_ATTN_STEP = ATTN_STEP_ROWS // KEYS_PER_BLOCK
CONV_ROW_CHUNK = 128
HIST_ROWS = 32
SAMPLE_BLOCK = 8
SAMPLE_UNROLL = 4
VMEM_LIMIT_BYTES = 56 * 1024 * 1024


def _slope(group, head):
    return 2.0 ** (-8.0 * (group * HEADS_PER_GROUP + head + 1) / N_HEADS)


def _sigmoid(x):
    return 1.0 / (1.0 + jnp.exp(-x))


def _silu(x):
    return x * _sigmoid(x)


def _layer_norm(x, g, b):
    mu = jnp.mean(x, axis=-1, keepdims=True)
    xc = x - mu
    var = jnp.mean(xc * xc, axis=-1, keepdims=True)
    return xc * lax.rsqrt(var + LN_EPS) * g + b


def _col_blocks(width):
    return [slice(b * MXU_COLS, (b + 1) * MXU_COLS) for b in range(width // MXU_COLS)]


def _row_stats(ref, width):
    total = sum(jnp.sum(ref[:, cols], axis=-1, keepdims=True) for cols in _col_blocks(width))
    mu = total / width
    sq = sum(jnp.sum(jnp.square(ref[:, cols] - mu), axis=-1, keepdims=True) for cols in _col_blocks(width))
    return mu, lax.rsqrt(sq / width + LN_EPS)


def _resident(shape):
    return pl.BlockSpec(shape, lambda *_: (0,) * len(shape), pipeline_mode=pl.Buffered(1))


def _w_cols(col0, width):
    assert col0 % width == 0
    return pl.BlockSpec((D_MODEL, width), lambda *_: (0, col0 // width), pipeline_mode=pl.Buffered(1))


def _row_vec(n):
    return pl.BlockSpec((1, n), lambda *_: (0, 0))


def _params(semantics):
    return pltpu.CompilerParams(dimension_semantics=semantics, vmem_limit_bytes=VMEM_LIMIT_BYTES)


def _store_kv_tiles(ref, k, v):
    rows = k.shape[0]
    for hh in range(HEADS_PER_GROUP):
        cols = slice(hh * HEAD_DIM, (hh + 1) * HEAD_DIM)
        ref[pl.ds(hh, rows, stride=KV_TILE_ROWS), :] = k[:, cols]
        ref[pl.ds(HEADS_PER_GROUP + hh, rows, stride=KV_TILE_ROWS), :] = v[:, cols]


def _mod_kernel(cp_ref, cs_ref, w_ref, b_ref, mp_ref, ms_ref):
    w = w_ref[...].astype(BF16)
    cp = jnp.broadcast_to(cp_ref[...], (8, D_MODEL)).astype(BF16)
    mp_ref[...] = jnp.dot(cp, w, preferred_element_type=F32)[0:1] + b_ref[...]
    ms_ref[...] = jnp.dot(cs_ref[...].astype(BF16), w, preferred_element_type=F32) + b_ref[...]


def _mod_call(c_prompt, c_sample, w_c, b_c):
    n_blk = 3
    return pl.pallas_call(
        _mod_kernel,
        grid=(n_blk,),
        in_specs=[
            pl.BlockSpec((1, D_MODEL), lambda n: (0, 0)),
            pl.BlockSpec((DEC_BATCH, D_MODEL), lambda n: (0, 0)),
            pl.BlockSpec((D_MODEL, D_MODEL), lambda n: (0, n)),
            pl.BlockSpec((1, D_MODEL), lambda n: (0, n)),
        ],
        out_specs=[
            pl.BlockSpec((1, D_MODEL), lambda n: (0, n)),
            pl.BlockSpec((DEC_BATCH, D_MODEL), lambda n: (0, n)),
        ],
        out_shape=[
            jax.ShapeDtypeStruct((1, 3 * D_MODEL), F32),
            jax.ShapeDtypeStruct((DEC_BATCH, 3 * D_MODEL), F32),
        ],
        compiler_params=_params(("arbitrary",)),
        name="adaln_mod",
    )(c_prompt, c_sample, w_c, b_c.reshape(1, 3 * D_MODEL))


N_ROW_TILES = SEQ // ROW_TILE
KV2_TILES = GROUPS[2][0] // ROW_TILE


N_QKV_BLOCKS = QKV_COLS // GROUP_COLS
CONV_CHUNKS = (CONV_CH // LANES) * (ROW_TILE // CONV_ROW_CHUNK)
CONV_CHUNKS_PER_TRIP = 2
BLOCKS_PER_TRIP = 4
assert CONV_CHUNKS == CONV_CHUNKS_PER_TRIP * (N_QKV_BLOCKS - 1) and (N_QKV_BLOCKS - 1) % BLOCKS_PER_TRIP == 0


def _front_kernel(x_ref, mod_ref, w_ref, wa_ref, wg_ref, wgb_ref, b_ref, cw_ref, cb_ref, cg_ref, cbeta_ref,
                  qkv_ref, kv0_ref, kv1_ref, kv2_ref, bin_ref, convp_ref,
                  hs_ref, hq_ref, lhs_ref, uext_ref, cws_ref, cacc_ref, tail_ref):
    i = pl.program_id(0)
    shift = mod_ref[:, 0:D_MODEL]
    scale = mod_ref[:, D_MODEL:2 * D_MODEL]
    h = x_ref[...] * (1.0 + scale) + shift
    lhs_ref[0] = h.astype(BF16)

    n_slab = D_MODEL // LANES
    d1, d2 = GROUPS[1][1], GROUPS[2][1]
    assert d2 == d1 * d1
    r1, r2 = ROW_TILE // d1, ROW_TILE // d2
    for c in range(n_slab):
        cols = slice(c * LANES, (c + 1) * LANES)
        hs_ref[c] = h[:, cols]
        for b in range(d1):
            rows_b = hs_ref[c, pl.ds(b, r1, stride=d1), :]
            hq_ref[c, b * r1:(b + 1) * r1, :] = rows_b
            lhs_ref[1, b * r1:(b + 1) * r1, cols] = rows_b.astype(BF16)
        for r in range(d2):
            a, b = r // d1, r % d1
            lhs_ref[2, r * r2:(r + 1) * r2, cols] = hq_ref[c, pl.ds(b * r1 + a, r2, stride=d1), :].astype(BF16)

    def gate_proj(wx_ref, col0, cols):
        bcols = slice(col0 + cols.start, col0 + cols.stop)
        return jnp.dot(lhs_ref[0], wx_ref[:, cols], preferred_element_type=F32) + b_ref[:, bcols]

    n_cslab = CONV_CH // LANES

    @pl.when(i == 0)
    def _():
        uext_ref[:, 0:HIST_ROWS, :] = jnp.zeros((n_cslab, HIST_ROWS, LANES), F32)
        for c in range(n_cslab):
            cws_ref[c, 0:CONV_WIDTH, :] = cw_ref[:, c * LANES:(c + 1) * LANES]
            cws_ref[c, CONV_WIDTH:CONV_WIDTH + 1, :] = cb_ref[:, c * LANES:(c + 1) * LANES]

    slabs_per_blk = MXU_COLS // LANES
    for b, cols in enumerate(_col_blocks(CONV_CH)):
        u = gate_proj(wa_ref, COL_GLU_A, cols) * _sigmoid(gate_proj(wg_ref, COL_GLU_G, cols))
        convp_ref[:, cols] = u[ROW_TILE - (CONV_WIDTH - 1):, :]
        for s in range(slabs_per_blk):
            uext_ref[b * slabs_per_blk + s, HIST_ROWS:, :] = u[:, s * LANES:(s + 1) * LANES]

    lead = HIST_ROWS - (CONV_WIDTH - 1)
    chunks_per_slab = ROW_TILE // CONV_ROW_CHUNK
    chunk_shift = chunks_per_slab.bit_length() - 1
    assert chunks_per_slab == 1 << chunk_shift

    def conv_chunk(idx):
        c = lax.shift_right_logical(idx, chunk_shift)
        r0 = pl.multiple_of(jnp.bitwise_and(idx, chunks_per_slab - 1) * CONV_ROW_CHUNK, CONV_ROW_CHUNK)
        acc = jnp.broadcast_to(cws_ref[c, CONV_WIDTH:CONV_WIDTH + 1, :], (CONV_ROW_CHUNK, LANES))
        for j in range(CONV_WIDTH):
            acc = acc + uext_ref[c, pl.ds(r0 + lead + j, CONV_ROW_CHUNK), :] * cws_ref[c, j:j + 1, :]

        def store():
            cacc_ref[c, pl.ds(r0, CONV_ROW_CHUNK), :] = acc
        return store

    def qkv_block(n):
        col0 = pl.multiple_of(n * GROUP_COLS, GROUP_COLS)
        val = jnp.dot(lhs_ref[n % N_GROUPS], w_ref[:, pl.ds(col0, GROUP_COLS)], preferred_element_type=F32)
        val = val + b_ref[:, pl.ds(col0, GROUP_COLS)]

        def store():
            tail_ref[n] = val[ROW_TILE - GROUPS[0][0]:, :]
            qkv_ref[n] = (val * jnp.where(n < N_GROUPS, Q_SCALE_LOG2, 1.0)).astype(BF16)
        return store

    def trip(n, carry):
        stores = [conv_chunk(n * BLOCKS_PER_TRIP * CONV_CHUNKS_PER_TRIP + k)
                  for k in range(BLOCKS_PER_TRIP * CONV_CHUNKS_PER_TRIP)]
        stores += [qkv_block(n * BLOCKS_PER_TRIP + k) for k in range(BLOCKS_PER_TRIP)]
        for st in stores:
            st()
        return carry

    lax.fori_loop(0, (N_QKV_BLOCKS - 1) // BLOCKS_PER_TRIP, trip, 0)
    qkv_block(N_QKV_BLOCKS - 1)()

    for c in range(n_cslab):
        uext_ref[c, 0:HIST_ROWS, :] = uext_ref[c, ROW_TILE:ROW_TILE + HIST_ROWS, :]
    _store_kv_tiles(kv0_ref, tail_ref[N_GROUPS], tail_ref[2 * N_GROUPS])

    total = sum(jnp.sum(cacc_ref[c], axis=-1, keepdims=True) for c in range(n_cslab))
    mu = total / CONV_CH
    sq = sum(jnp.sum(jnp.square(cacc_ref[c] - mu), axis=-1, keepdims=True) for c in range(n_cslab))
    rstd = lax.rsqrt(sq / CONV_CH + LN_EPS)
    for b, cols in enumerate(_col_blocks(CONV_CH)):
        conv = jnp.concatenate([cacc_ref[b * slabs_per_blk + s] for s in range(slabs_per_blk)], axis=-1)
        normed = (conv - mu) * rstd * cg_ref[:, cols] + cbeta_ref[:, cols]
        bin_ref[:, cols] = (_silu(normed) * _silu(gate_proj(wgb_ref, COL_GB, cols))).astype(BF16)

    def proj_nat(blk):
        cols = slice(blk * GROUP_COLS, (blk + 1) * GROUP_COLS)
        return jnp.dot(lhs_ref[0], w_ref[:, cols], preferred_element_type=F32) + b_ref[:, cols]

    @pl.when(i == N_ROW_TILES - 1)
    def _():
        _store_kv_tiles(kv1_ref, proj_nat(N_GROUPS + 1), proj_nat(2 * N_GROUPS + 1))

    @pl.when(i >= N_ROW_TILES - KV2_TILES)
    def _():
        _store_kv_tiles(kv2_ref, proj_nat(N_GROUPS + 2), proj_nat(2 * N_GROUPS + 2))


def _front_call(x, mod_p, w_bf, b_in, conv_w, conv_b, cn_g, cn_b):
    nat = pl.BlockSpec((ROW_TILE, GROUP_COLS), lambda i: (i, 0))
    first_kv2_tile = N_ROW_TILES - KV2_TILES
    kv_shape = lambda g: jax.ShapeDtypeStruct((GROUPS[g][0] * KV_TILE_ROWS, HEAD_DIM), F32)
    return pl.pallas_call(
        _front_kernel,
        grid=(N_ROW_TILES,),
        in_specs=[
            pl.BlockSpec((ROW_TILE, D_MODEL), lambda i: (i, 0)),
            _row_vec(3 * D_MODEL),
            _w_cols(0, QKV_COLS), _w_cols(COL_GLU_A, W_BLOCK), _w_cols(COL_GLU_G, W_BLOCK), _w_cols(COL_GB, W_BLOCK),
            _row_vec(IN_COLS),
            pl.BlockSpec((CONV_WIDTH, CONV_CH), lambda i: (0, 0)),
            _row_vec(CONV_CH), _row_vec(CONV_CH), _row_vec(CONV_CH),
        ],
        out_specs=[pl.BlockSpec((N_QKV_BLOCKS, ROW_TILE, GROUP_COLS), lambda i: (0, i, 0)),
                   pl.BlockSpec((GROUPS[0][0] * KV_TILE_ROWS, HEAD_DIM), lambda i: (0, 0)),
                   pl.BlockSpec((GROUPS[1][0] * KV_TILE_ROWS, HEAD_DIM), lambda i: (0, 0)),
                   pl.BlockSpec((ROW_TILE * KV_TILE_ROWS, HEAD_DIM),
                                lambda i: (jnp.maximum(i - first_kv2_tile, 0), 0)),
                   nat,
                   pl.BlockSpec((CONV_WIDTH - 1, CONV_CH), lambda i: (0, 0))],
        out_shape=[jax.ShapeDtypeStruct((N_QKV_BLOCKS, SEQ, GROUP_COLS), BF16),
                   kv_shape(0), kv_shape(1), kv_shape(2),
                   jax.ShapeDtypeStruct((SEQ, GROUP_COLS), BF16),
                   jax.ShapeDtypeStruct((CONV_WIDTH - 1, CONV_CH), F32)],
        scratch_shapes=[
            pltpu.VMEM((D_MODEL // LANES, ROW_TILE, LANES), F32),
            pltpu.VMEM((D_MODEL // LANES, ROW_TILE, LANES), F32),
            pltpu.VMEM((N_GROUPS, ROW_TILE, D_MODEL), BF16),
            pltpu.VMEM((CONV_CH // LANES, HIST_ROWS + ROW_TILE, LANES), F32),
            pltpu.VMEM((CONV_CH // LANES, CONV_WIDTH + 1, LANES), F32),
            pltpu.VMEM((CONV_CH // LANES, ROW_TILE, LANES), F32),
            pltpu.VMEM((N_QKV_BLOCKS, GROUPS[0][0], GROUP_COLS), F32),
        ],
        compiler_params=_params(("arbitrary",)),
        name="prompt_front",
    )(x, mod_p, w_bf, w_bf, w_bf, w_bf, b_in, conv_w, conv_b, cn_g, cn_b)


def _attn_kernel(q_ref, k_ref, v_ref, kp_ref, vp_ref, o_ref, lse_ref, bias_ref, *scr, group):
    dil = GROUPS[group][1]
    tail = KEYS_PER_BLOCK * dil
    t = pl.program_id(0)

    @pl.when(t == 0)
    def _():
        row = lax.broadcasted_iota(jnp.int32, (KEYS_PER_BLOCK, 2 * KEYS_PER_BLOCK), 0)
        col = lax.broadcasted_iota(jnp.int32, (KEYS_PER_BLOCK, 2 * KEYS_PER_BLOCK), 1)
        dist = row - col + KEYS_PER_BLOCK
        valid = (dist >= 0) & (dist <= KEYS_PER_BLOCK)
        valid_first = valid & (col >= KEYS_PER_BLOCK)
        distf = (dil * dist).astype(F32)
        for hh in range(HEADS_PER_GROUP):
            b = -(_slope(group, hh) * distf) * LOG2_E
            bias_ref[0, hh] = jnp.where(valid, b, NEG)
            bias_ref[1, hh] = jnp.where(valid_first, b, NEG)

    first_step = (t == 0).astype(jnp.int32)
    head_lane = lax.broadcasted_iota(jnp.int32, (KEYS_PER_BLOCK, LANES), 1)
    if dil > 1:
        oscr_ref, lscr_ref = scr
    def block(ref, j):
        if tail < ATTN_STEP_ROWS:
            return ref[j * KEYS_PER_BLOCK:(j + 1) * KEYS_PER_BLOCK, :]
        piece = ROW_TILE // dil
        return jnp.concatenate([ref[tt * ROW_TILE + j * piece:tt * ROW_TILE + (j + 1) * piece, :]
                                for tt in range(ATTN_STEP_ROWS // ROW_TILE)], axis=0)

    for j in range(BLOCKS_PER_ATTN_STEP):
        qb = block(q_ref, j)
        kself, vself = block(k_ref, j), block(v_ref, j)
        if j < dil:
            kprev, vprev = block(kp_ref, j), block(vp_ref, j)
        else:
            kprev, vprev = block(k_ref, j - dil), block(v_ref, j - dil)
        first = first_step if j < dil else 0
        start = (j // dil) * tail + (j % dil)
        rows = pl.ds(start, KEYS_PER_BLOCK, stride=dil) if dil > 1 else pl.ds(start, KEYS_PER_BLOCK)
        m_tile = jnp.zeros((KEYS_PER_BLOCK, LANES), F32)
        l_tile = jnp.ones((KEYS_PER_BLOCK, LANES), F32)
        for hh in range(HEADS_PER_GROUP):
            cols = slice(hh * HEAD_DIM, (hh + 1) * HEAD_DIM)
            kc = jnp.concatenate([kprev[:, cols], kself[:, cols]], axis=0)
            vc = jnp.concatenate([vprev[:, cols], vself[:, cols]], axis=0)
            s = lax.dot_general(qb[:, cols], kc, (((1,), (1,)), ((), ())), preferred_element_type=F32)
            s = s + bias_ref[first, hh]
            m = jnp.max(s, axis=-1, keepdims=True)
            p = jnp.exp2(s - m)
            l = jnp.sum(p, axis=-1, keepdims=True)
            acc = jnp.dot(p.astype(BF16), vc, preferred_element_type=F32)
            if dil > 1:
                oscr_ref[hh, rows, :] = acc
            else:
                o_ref[rows, cols] = acc.astype(BF16)
            m_tile = jnp.where(head_lane == hh, m, m_tile)
            l_tile = jnp.where(head_lane == hh, l, l_tile)
        if dil > 1:
            lscr_ref[0, rows, :] = m_tile
            lscr_ref[1, rows, :] = l_tile
        else:
            lse_ref[rows, 0:LANES] = m_tile
            lse_ref[rows, LANES:] = l_tile

    if dil > 1:
        for hh in range(HEADS_PER_GROUP):
            o_ref[:, hh * HEAD_DIM:(hh + 1) * HEAD_DIM] = oscr_ref[hh].astype(BF16)
        lse_ref[:, 0:LANES] = lscr_ref[0]
        lse_ref[:, LANES:] = lscr_ref[1]


def _attn_call(group, qkv):
    dil = GROUPS[group][1]
    tail = KEYS_PER_BLOCK * dil
    steps = SEQ // ATTN_STEP_ROWS
    tails_per_step = ATTN_STEP_ROWS // tail
    stacked = lambda blk: pl.BlockSpec((None, ATTN_STEP_ROWS, GROUP_COLS), lambda t: (blk, t, 0))
    prev_of = lambda blk: pl.BlockSpec((None, tail, GROUP_COLS),
                                       lambda t: (blk, jnp.maximum(t * tails_per_step - 1, 0), 0))
    q_blk, k_blk, v_blk = group, N_GROUPS + group, 2 * N_GROUPS + group
    cur = pl.BlockSpec((ATTN_STEP_ROWS, GROUP_COLS), lambda t: (t, 0))
    lse_blk = pl.BlockSpec((ATTN_STEP_ROWS, 2 * LANES), lambda t: (t, 0))
    scratch = [pltpu.VMEM((2, HEADS_PER_GROUP, KEYS_PER_BLOCK, 2 * KEYS_PER_BLOCK), F32)]
    if dil > 1:
        scratch += [pltpu.VMEM((HEADS_PER_GROUP, ATTN_STEP_ROWS, HEAD_DIM), F32),
                    pltpu.VMEM((2, ATTN_STEP_ROWS, LANES), F32)]
    return pl.pallas_call(
        functools.partial(_attn_kernel, group=group),
        grid=(steps,),
        in_specs=[stacked(q_blk), stacked(k_blk), stacked(v_blk), prev_of(k_blk), prev_of(v_blk)],
        out_specs=[cur, lse_blk],
        out_shape=[jax.ShapeDtypeStruct((SEQ, GROUP_COLS), BF16), jax.ShapeDtypeStruct((SEQ, 2 * LANES), F32)],
        scratch_shapes=scratch,
        compiler_params=_params(("arbitrary",)),
        name=f"prompt_attn_g{group}",
    )(qkv, qkv, qkv, qkv, qkv)


def _tail_kernel(x_ref, mod_ref, o0_ref, o1_ref, o2_ref, l0_ref, l1_ref, l2_ref, bin_ref,
                 wga_ref, wma0_ref, wma1_ref, wmb0_ref, wmb1_ref, b_ref, wpa_ref, wpb_ref, wo_ref, lng_ref, lnb_ref,
                 y_ref,
                 hb_ref, ain_ref, pre_ref, res_ref):
    shift = mod_ref[:, 0:D_MODEL]
    scale = mod_ref[:, D_MODEL:2 * D_MODEL]
    hb_ref[...] = (x_ref[...] * (1.0 + scale) + shift).astype(BF16)

    def gate_proj(wx_refs, col0, cols):
        wx_ref = wx_refs[cols.start // W_BLOCK]
        wcols = slice(cols.start % W_BLOCK, cols.start % W_BLOCK + MXU_COLS)
        bcols = slice(col0 + cols.start, col0 + cols.stop)
        return jnp.dot(hb_ref[...], wx_ref[:, wcols], preferred_element_type=F32) + b_ref[:, bcols]

    stats = (l0_ref, l1_ref, l2_ref)
    maxes = [st[:, 0:LANES] for st in stats]
    m = jnp.maximum(jnp.maximum(maxes[0], maxes[1]), maxes[2])
    es = [jnp.exp2(mg - m) for mg in maxes]
    inv = 1.0 / sum(e * st[:, LANES:] for e, st in zip(es, stats))
    weights = tuple(e * inv for e in es)
    o_refs = (o0_ref, o1_ref, o2_ref)
    for cols in _col_blocks(GROUP_COLS):
        heads = []
        for hh in range(cols.start // HEAD_DIM, cols.stop // HEAD_DIM):
            hcols = slice(hh * HEAD_DIM, (hh + 1) * HEAD_DIM)
            heads.append(sum(w[:, hh:hh + 1] * o_ref[:, hcols].astype(F32) for w, o_ref in zip(weights, o_refs)))
        ga = gate_proj((wga_ref,), COL_GA, cols)
        ain_ref[:, cols] = (jnp.concatenate(heads, axis=-1) * _silu(ga)).astype(BF16)

    for cols in _col_blocks(D_MODEL):
        a = jnp.dot(ain_ref[...], wpa_ref[:, cols], preferred_element_type=F32)
        bb = jnp.dot(bin_ref[...], wpb_ref[:, cols], preferred_element_type=F32)
        ma = gate_proj((wma0_ref, wma1_ref), COL_MA, cols)
        mb = gate_proj((wmb0_ref, wmb1_ref), COL_MB, cols)
        pre_ref[:, cols] = (_sigmoid(ma) * a + _sigmoid(mb) * bb).astype(BF16)
    for cols in _col_blocks(D_MODEL):
        gate = mod_ref[:, 2 * D_MODEL + cols.start:2 * D_MODEL + cols.stop]
        y = jnp.dot(pre_ref[...], wo_ref[:, cols], preferred_element_type=F32)
        res_ref[:, cols] = ALPHA * x_ref[:, cols] + gate * y
    mu, rstd = _row_stats(res_ref, D_MODEL)
    for cols in _col_blocks(D_MODEL):
        y_ref[:, cols] = (res_ref[:, cols] - mu) * rstd * lng_ref[:, cols] + lnb_ref[:, cols]


def _tail_call(x, mod_p, outs, lses, b_conv, w_bf, b_in, w_pa, w_pb, w_o, ln_g, ln_b):
    o_blk = pl.BlockSpec((ROW_TILE, GROUP_COLS), lambda i: (i, 0))
    lse_blk = pl.BlockSpec((ROW_TILE, 2 * LANES), lambda i: (i, 0))
    return pl.pallas_call(
        _tail_kernel,
        grid=(N_ROW_TILES,),
        in_specs=[
            pl.BlockSpec((ROW_TILE, D_MODEL), lambda i: (i, 0)),
            _row_vec(3 * D_MODEL),
            o_blk, o_blk, o_blk, lse_blk, lse_blk, lse_blk,
            pl.BlockSpec((ROW_TILE, CONV_CH), lambda i: (i, 0)),
            _w_cols(COL_GA, W_BLOCK),
            _w_cols(COL_MA, W_BLOCK), _w_cols(COL_MA + W_BLOCK, W_BLOCK),
            _w_cols(COL_MB, W_BLOCK), _w_cols(COL_MB + W_BLOCK, W_BLOCK),
            _row_vec(IN_COLS),
            _resident((GROUP_COLS, D_MODEL)),
            _resident((CONV_CH, D_MODEL)),
            _resident((D_MODEL, D_MODEL)),
            _row_vec(D_MODEL), _row_vec(D_MODEL),
        ],
        out_specs=pl.BlockSpec((ROW_TILE, D_MODEL), lambda i: (i, 0)),
        out_shape=jax.ShapeDtypeStruct((SEQ, D_MODEL), F32),
        scratch_shapes=[
            pltpu.VMEM((ROW_TILE, D_MODEL), BF16),
            pltpu.VMEM((ROW_TILE, GROUP_COLS), BF16),
            pltpu.VMEM((ROW_TILE, D_MODEL), BF16),
            pltpu.VMEM((ROW_TILE, D_MODEL), F32),
        ],
        compiler_params=_params(("arbitrary",)),
        name="prompt_tail",
    )(x, mod_p, *outs, *lses, b_conv, w_bf, w_bf, w_bf, w_bf, w_bf, b_in, w_pa, w_pb, w_o, ln_g, ln_b)


N_W_BLOCKS = 4
CAST_BLOCK = IN_COLS // N_W_BLOCKS
assert CAST_BLOCK * N_W_BLOCKS == IN_COLS and CAST_BLOCK % LANES == 0


def _sample_proj_kernel(x_ref, mod_ref, w_ref, b_ref,
                        wbf_ref, z_ref, qt0_ref, qt1_ref, qt2_ref, kv0_ref, kv1_ref, kv2_ref):
    n = pl.program_id(0)
    shift = mod_ref[:, 0:D_MODEL]
    scale = mod_ref[:, D_MODEL:2 * D_MODEL]
    hb = (x_ref[...] * (1.0 + scale) + shift).astype(BF16)
    wb = w_ref[...].astype(BF16)
    wbf_ref[...] = wb
    col0 = pl.multiple_of(n * CAST_BLOCK, LANES)
    z_ref[:, pl.ds(col0, CAST_BLOCK)] = (jnp.dot(hb, wb, preferred_element_type=F32)
                                         + b_ref[:, pl.ds(col0, CAST_BLOCK)])

    @pl.when(n == N_W_BLOCKS - 1)
    def _():
        zeros = jnp.zeros((DEC_BATCH, GROUP_COLS), F32)
        blk = lambda i: z_ref[:, i * GROUP_COLS:(i + 1) * GROUP_COLS]
        for g, (qt_ref, kv_ref) in enumerate(((qt0_ref, kv0_ref), (qt1_ref, kv1_ref), (qt2_ref, kv2_ref))):
            _store_kv_tiles(qt_ref, blk(g) * Q_SCALE, zeros)
            _store_kv_tiles(kv_ref, blk(N_GROUPS + g), blk(2 * N_GROUPS + g))


def _sample_proj_call(x_s, mod_s, w_in, b_in):
    full = lambda shape: pl.BlockSpec(shape, lambda n: (0,) * len(shape))
    tile_rows = DEC_BATCH * KV_TILE_ROWS
    tiles = jax.ShapeDtypeStruct((tile_rows, HEAD_DIM), F32)
    w_blk = pl.BlockSpec((D_MODEL, CAST_BLOCK), lambda n: (0, n))
    return pl.pallas_call(
        _sample_proj_kernel,
        grid=(N_W_BLOCKS,),
        in_specs=[full((DEC_BATCH, D_MODEL)), full((DEC_BATCH, 3 * D_MODEL)), w_blk, _row_vec(IN_COLS)],
        out_specs=[w_blk, full((DEC_BATCH, IN_COLS))] + [full((tile_rows, HEAD_DIM))] * 6,
        out_shape=[jax.ShapeDtypeStruct((D_MODEL, IN_COLS), BF16), jax.ShapeDtypeStruct((DEC_BATCH, IN_COLS), F32)]
                  + [tiles] * 6,
        compiler_params=_params(("arbitrary",)),
        name="sample_proj",
    )(x_s, mod_s, w_in, b_in)


def _sample_attn_kernel(qt0_ref, qt1_ref, qt2_ref, kv0_ref, kv1_ref, kv2_ref, c0_ref, c1_ref, c2_ref, o_ref,
                        bias_ref):
    qt_refs = (qt0_ref, qt1_ref, qt2_ref)
    kv_refs = (kv0_ref, kv1_ref, kv2_ref)
    c_refs = (c0_ref, c1_ref, c2_ref)
    half = HEADS_PER_GROUP
    tile = (KV_TILE_ROWS, HEAD_DIM)
    keys_tile = (KEYS_PER_BLOCK,) + tile

    @pl.when(pl.program_id(0) == 0)
    def _():
        key = lax.broadcasted_iota(jnp.int32, keys_tile, 0)
        sub = lax.broadcasted_iota(jnp.int32, keys_tile, 1)
        steps_back = (KEYS_PER_BLOCK - key).astype(F32)
        for g in range(N_GROUPS):
            slope_rows = jnp.zeros(keys_tile, F32)
            for hh in range(HEADS_PER_GROUP):
                slope_rows = jnp.where(sub == half + hh, _slope(g, hh), slope_rows)
            bias_ref[g] = -(slope_rows * (GROUPS[g][1] * steps_back)) * LOG2_E

    def body(b, carry):
        outs, lses = [], []
        for g in range(N_GROUPS):
            x = c_refs[g][b]
            qt = qt_refs[g][b] * LOG2_E
            kvn = kv_refs[g][b]
            s = jnp.broadcast_to(jnp.sum(x * qt[None], axis=-1, keepdims=True), keys_tile)
            s = pltpu.roll(s, half, axis=1) + bias_ref[g]
            s_self = jnp.broadcast_to(jnp.sum(qt * kvn, axis=-1, keepdims=True), tile)
            s_self = pltpu.roll(s_self, half, axis=0)
            m = jnp.maximum(jnp.max(s, axis=0), s_self)
            p = jnp.exp2(s - m[None])
            p_self = jnp.exp2(s_self - m)
            l = jnp.sum(p, axis=0) + p_self
            acc = jnp.sum(x * p, axis=0) + p_self * kvn
            outs.append(acc / l)
            lses.append(m + jnp.log2(l))
        mx = jnp.maximum(jnp.maximum(lses[0], lses[1]), lses[2])
        es = [jnp.exp2(ls - mx) for ls in lses]
        o_ref[b] = (es[0] * outs[0] + es[1] * outs[1] + es[2] * outs[2]) / (es[0] + es[1] + es[2])
        return carry

    lax.fori_loop(0, SAMPLE_BLOCK, body, 0, unroll=SAMPLE_UNROLL)


def _sample_attn_call(qts, kvs, caches):
    views = [c.reshape(DEC_BATCH, KEYS_PER_BLOCK, GROUPS[g][1], KV_TILE_ROWS, HEAD_DIM) for g, c in enumerate(caches)]
    tile3 = lambda a: a.reshape(DEC_BATCH, KV_TILE_ROWS, HEAD_DIM)
    tile_blk = pl.BlockSpec((SAMPLE_BLOCK, KV_TILE_ROWS, HEAD_DIM), lambda i: (i, 0, 0))
    cache_blk = pl.BlockSpec((SAMPLE_BLOCK, KEYS_PER_BLOCK, None, KV_TILE_ROWS, HEAD_DIM), lambda i: (i, 0, 0, 0, 0))
    return pl.pallas_call(
        _sample_attn_kernel,
        grid=(DEC_BATCH // SAMPLE_BLOCK,),
        in_specs=[tile_blk] * 6 + [cache_blk] * 3,
        out_specs=tile_blk,
        out_shape=jax.ShapeDtypeStruct((DEC_BATCH, KV_TILE_ROWS, HEAD_DIM), F32),
        scratch_shapes=[pltpu.VMEM((N_GROUPS, KEYS_PER_BLOCK, KV_TILE_ROWS, HEAD_DIM), F32)],
        compiler_params=_params(("arbitrary",)),
        name="sample_attn",
    )(*[tile3(a) for a in qts], *[tile3(a) for a in kvs], *views)


def _sample_tail_kernel(x_ref, mod_ref, o_ref, zr_ref, st_ref, cw_ref, cb_ref, cg_ref, cbeta_ref,
                        wpa_ref, wpb_ref, wo_ref, lng_ref, lnb_ref, y_ref, convs_ref):
    gate = mod_ref[:, 2 * D_MODEL:3 * D_MODEL]
    z = lambda col0, width: zr_ref[:, col0:col0 + width]
    u = z(COL_GLU_A, CONV_CH) * _sigmoid(z(COL_GLU_G, CONV_CH))
    hist = CONV_WIDTH - 1
    acc = cb_ref[...] + u * cw_ref[hist:hist + 1, :]
    for j in range(hist):
        acc = acc + st_ref[j] * cw_ref[j:j + 1, :]
    conv_out = _silu(_layer_norm(acc, cg_ref[...], cbeta_ref[...]))
    for j in range(hist - 1):
        convs_ref[j] = st_ref[j + 1]
    convs_ref[hist - 1] = u
    o_attn = jnp.concatenate(
        [o_ref[pl.ds(HEADS_PER_GROUP + hh, DEC_BATCH, stride=KV_TILE_ROWS), :] for hh in range(HEADS_PER_GROUP)],
        axis=-1)
    a = jnp.dot((o_attn * _silu(z(COL_GA, GROUP_COLS))).astype(BF16), wpa_ref[...], preferred_element_type=F32)
    b = jnp.dot((conv_out * _silu(z(COL_GB, CONV_CH))).astype(BF16), wpb_ref[...], preferred_element_type=F32)
    pre = (_sigmoid(z(COL_MA, D_MODEL)) * a + _sigmoid(z(COL_MB, D_MODEL)) * b).astype(BF16)
    y = jnp.dot(pre, wo_ref[...], preferred_element_type=F32)
    y_ref[...] = _layer_norm(ALPHA * x_ref[...] + gate * y, lng_ref[...], lnb_ref[...])


def _sample_tail_call(x_s, mod_s, o_s, zr, state_t, conv_w, conv_b, cn_g, cn_b, w_pa, w_pb, w_o, ln_g, ln_b):
    full = lambda shape: pl.BlockSpec(shape, lambda i: (0,) * len(shape))
    args = (x_s, mod_s, o_s, zr, state_t, conv_w, conv_b, cn_g, cn_b, w_pa, w_pb, w_o, ln_g, ln_b)
    return pl.pallas_call(
        _sample_tail_kernel,
        grid=(1,),
        in_specs=[full(a.shape) for a in args],
        out_specs=[full((DEC_BATCH, D_MODEL)), full(state_t.shape)],
        out_shape=[jax.ShapeDtypeStruct((DEC_BATCH, D_MODEL), F32),
                   jax.ShapeDtypeStruct(state_t.shape, F32)],
        compiler_params=_params(("arbitrary",)),
        name="sample_tail",
    )(*args)


def kernel(x_prompt, x_sample, c_prompt, c_sample, cache_kv_w128, cache_kv_w512, cache_kv_w2048, state_conv,
           w_c, b_c, w_in, b_in, conv_w, conv_b, conv_norm_g, conv_norm_b, w_pa, w_pb, w_o, ln_g, ln_b):
    assert x_prompt.shape == (1, SEQ, D_MODEL) and x_sample.shape == (DEC_BATCH, 1, D_MODEL)
    assert w_in.shape == (D_MODEL, IN_COLS)
    caches = (cache_kv_w128, cache_kv_w512, cache_kv_w2048)
    for (window, _), c in zip(GROUPS, caches):
        assert c.shape == (DEC_BATCH, window, 2, HEADS_PER_GROUP, HEAD_DIM)

    w_pa_b, w_pb_b, w_o_b = w_pa.astype(BF16), w_pb.astype(BF16), w_o.astype(BF16)
    vec = lambda a: a.reshape(1, -1)
    b_row = vec(b_in)
    conv_args = (conv_w, vec(conv_b), vec(conv_norm_g), vec(conv_norm_b))
    out_args = (w_pa_b, w_pb_b, w_o_b, vec(ln_g), vec(ln_b))

    mod_p, mod_s = _mod_call(c_prompt, c_sample, w_c, b_c)
    x_s = x_sample.reshape(DEC_BATCH, D_MODEL)
    w_bf, z_s, qt0, qt1, qt2, kvs0, kvs1, kvs2 = _sample_proj_call(x_s, mod_s, w_in, b_row)

    x_p = x_prompt.reshape(SEQ, D_MODEL)
    qkv, kv0, kv1, kv2, b_conv, conv_p = _front_call(x_p, mod_p, w_bf, b_row, *conv_args)
    attn = [_attn_call(g, qkv) for g in range(N_GROUPS)]
    y_p = _tail_call(x_p, mod_p, [a[0] for a in attn], [a[1] for a in attn], b_conv, w_bf, b_row, *out_args)

    o_s = _sample_attn_call((qt0, qt1, qt2), (kvs0, kvs1, kvs2), caches)
    o_s = o_s.reshape(DEC_BATCH * KV_TILE_ROWS, HEAD_DIM)
    state_t = jnp.transpose(state_conv, (1, 0, 2))
    y_s, conv_s_t = _sample_tail_call(x_s, mod_s, o_s, z_s, state_t, *conv_args, *out_args)

    kv_shape_p = lambda keep: (1, keep, 2, HEADS_PER_GROUP, HEAD_DIM)
    kv_shape_s = (DEC_BATCH, 1, 2, HEADS_PER_GROUP, HEAD_DIM)
    return (y_p.reshape(1, SEQ, D_MODEL),
            y_s.reshape(DEC_BATCH, 1, D_MODEL),
            kv0.reshape(kv_shape_p(GROUPS[0][0])),
            kv1.reshape(kv_shape_p(GROUPS[1][0])),
            kv2.reshape(kv_shape_p(GROUPS[2][0])),
            conv_p.reshape(1, CONV_WIDTH - 1, CONV_CH),
            kvs0.reshape(kv_shape_s), kvs1.reshape(kv_shape_s), kvs2.reshape(kv_shape_s),
            jnp.transpose(conv_s_t, (1, 0, 2)))
```

```python
import functools

import jax
import jax.numpy as jnp
from jax import lax
from jax.experimental import pallas as pl
from jax.experimental.pallas import tpu as pltpu

F32 = jnp.float32
BF16 = jnp.bfloat16

D_MODEL = 1024
SEQ = 16384
DEC_BATCH = 128
DEPTH = 1
HEAD_DIM = 128
HEADS_PER_GROUP = 4
GROUPS = ((128, 1), (512, 4), (2048, 16))
N_GROUPS = len(GROUPS)
N_HEADS = N_GROUPS * HEADS_PER_GROUP
GROUP_COLS = HEADS_PER_GROUP * HEAD_DIM
ATTN_QKV = N_HEADS * HEAD_DIM
QKV_COLS = 3 * ATTN_QKV
CONV_CH = D_MODEL // 2
CONV_WIDTH = 31
COL_GA = QKV_COLS
COL_GLU_A = COL_GA + GROUP_COLS
COL_GLU_G = COL_GLU_A + CONV_CH
COL_GB = COL_GLU_G + CONV_CH
COL_MA = COL_GB + CONV_CH
COL_MB = COL_MA + D_MODEL
IN_COLS = COL_MB + D_MODEL
ALPHA = (2.0 * DEPTH) ** 0.25
LN_EPS = 1e-5
NEG = -1e30
Q_SCALE = HEAD_DIM ** -0.5
LOG2_E = 1.4426950408889634
Q_SCALE_LOG2 = Q_SCALE * LOG2_E
KEYS_PER_BLOCK = 128
KV_TILE_ROWS = 2 * HEADS_PER_GROUP

LANES = 128
MXU_COLS = 256
W_BLOCK = 512
ROW_TILE = 512
ATTN_STEP_ROWS = 2048
BLOCKS_PER_ATTN_STEP = ATTN_STEP_ROWS // KEYS_PER_BLOCK
CONV_ROW_CHUNK = 64
HIST_ROWS = 32
SAMPLE_BLOCK = 8
SAMPLE_UNROLL = 4
VMEM_LIMIT_BYTES = 56 * 1024 * 1024


def _slope(group, head):
    return 2.0 ** (-8.0 * (group * HEADS_PER_GROUP + head + 1) / N_HEADS)


def _sigmoid(x):
    return 1.0 / (1.0 + jnp.exp(-x))


def _silu(x):
    return x * _sigmoid(x)


def _layer_norm(x, g, b):
    mu = jnp.mean(x, axis=-1, keepdims=True)
    xc = x - mu
    var = jnp.mean(xc * xc, axis=-1, keepdims=True)
    return xc * lax.rsqrt(var + LN_EPS) * g + b


def _col_blocks(width):
    return [slice(b * MXU_COLS, (b + 1) * MXU_COLS) for b in range(width // MXU_COLS)]


def _row_stats(ref, width):
    total = sum(jnp.sum(ref[:, cols], axis=-1, keepdims=True) for cols in _col_blocks(width))
    mu = total / width
    sq = sum(jnp.sum(jnp.square(ref[:, cols] - mu), axis=-1, keepdims=True) for cols in _col_blocks(width))
    return mu, lax.rsqrt(sq / width + LN_EPS)


def _resident(shape):
    return pl.BlockSpec(shape, lambda *_: (0,) * len(shape), pipeline_mode=pl.Buffered(1))


def _w_cols(col0, width):
    assert col0 % width == 0
    return pl.BlockSpec((D_MODEL, width), lambda *_: (0, col0 // width), pipeline_mode=pl.Buffered(1))


def _row_vec(n):
    return pl.BlockSpec((1, n), lambda *_: (0, 0))


def _params(semantics):
    return pltpu.CompilerParams(dimension_semantics=semantics, vmem_limit_bytes=VMEM_LIMIT_BYTES)


def _store_kv_tiles(ref, k, v):
    rows = k.shape[0]
    for hh in range(HEADS_PER_GROUP):
        cols = slice(hh * HEAD_DIM, (hh + 1) * HEAD_DIM)
        ref[pl.ds(hh, rows, stride=KV_TILE_ROWS), :] = k[:, cols]
        ref[pl.ds(HEADS_PER_GROUP + hh, rows, stride=KV_TILE_ROWS), :] = v[:, cols]


def _mod_kernel(cp_ref, cs_ref, w_ref, b_ref, mp_ref, ms_ref):
    w = w_ref[...].astype(BF16)
    cp = jnp.broadcast_to(cp_ref[...], (8, D_MODEL)).astype(BF16)
    mp_ref[...] = jnp.dot(cp, w, preferred_element_type=F32)[0:1] + b_ref[...]
    ms_ref[...] = jnp.dot(cs_ref[...].astype(BF16), w, preferred_element_type=F32) + b_ref[...]


def _mod_call(c_prompt, c_sample, w_c, b_c):
    n_blk = 3
    return pl.pallas_call(
        _mod_kernel,
        grid=(n_blk,),
        in_specs=[
            pl.BlockSpec((1, D_MODEL), lambda n: (0, 0)),
            pl.BlockSpec((DEC_BATCH, D_MODEL), lambda n: (0, 0)),
            pl.BlockSpec((D_MODEL, D_MODEL), lambda n: (0, n)),
            pl.BlockSpec((1, D_MODEL), lambda n: (0, n)),
        ],
        out_specs=[
            pl.BlockSpec((1, D_MODEL), lambda n: (0, n)),
            pl.BlockSpec((DEC_BATCH, D_MODEL), lambda n: (0, n)),
        ],
        out_shape=[
            jax.ShapeDtypeStruct((1, 3 * D_MODEL), F32),
            jax.ShapeDtypeStruct((DEC_BATCH, 3 * D_MODEL), F32),
        ],
        compiler_params=_params(("arbitrary",)),
        name="adaln_mod",
    )(c_prompt, c_sample, w_c, b_c.reshape(1, 3 * D_MODEL))


N_ROW_TILES = SEQ // ROW_TILE
KV2_TILES = GROUPS[2][0] // ROW_TILE


N_QKV_BLOCKS = QKV_COLS // GROUP_COLS
CONV_CHUNKS = (CONV_CH // LANES) * (ROW_TILE // CONV_ROW_CHUNK)
CONV_CHUNKS_PER_TRIP = 4
BLOCKS_PER_TRIP = 8
assert CONV_CHUNKS == CONV_CHUNKS_PER_TRIP * (N_QKV_BLOCKS - 1) and (N_QKV_BLOCKS - 1) % BLOCKS_PER_TRIP == 0


def _front_kernel(x_ref, mod_ref, w_ref, wa_ref, wg_ref, wgb_ref, b_ref, cw_ref, cb_ref, cg_ref, cbeta_ref,
                  qkv_ref, kv0_ref, kv1_ref, kv2_ref, bin_ref, convp_ref,
                  hs_ref, hq_ref, lhs_ref, uext_ref, cws_ref, cacc_ref, tail_ref):
    i = pl.program_id(0)
    shift = mod_ref[:, 0:D_MODEL]
    scale = mod_ref[:, D_MODEL:2 * D_MODEL]
    h = x_ref[...] * (1.0 + scale) + shift
    lhs_ref[0] = h.astype(BF16)

    n_slab = D_MODEL // LANES
    d1, d2 = GROUPS[1][1], GROUPS[2][1]
    assert d2 == d1 * d1
    r1, r2 = ROW_TILE // d1, ROW_TILE // d2
    for c in range(n_slab):
        cols = slice(c * LANES, (c + 1) * LANES)
        hs_ref[c] = h[:, cols]
        for b in range(d1):
            rows_b = hs_ref[c, pl.ds(b, r1, stride=d1), :]
            hq_ref[c, b * r1:(b + 1) * r1, :] = rows_b
            lhs_ref[1, b * r1:(b + 1) * r1, cols] = rows_b.astype(BF16)
        for r in range(d2):
            a, b = r // d1, r % d1
            lhs_ref[2, r * r2:(r + 1) * r2, cols] = hq_ref[c, pl.ds(b * r1 + a, r2, stride=d1), :].astype(BF16)

    def gate_proj(wx_ref, col0, cols):
        bcols = slice(col0 + cols.start, col0 + cols.stop)
        return jnp.dot(lhs_ref[0], wx_ref[:, cols], preferred_element_type=F32) + b_ref[:, bcols]

    n_cslab = CONV_CH // LANES

    @pl.when(i == 0)
    def _():
        uext_ref[:, 0:HIST_ROWS, :] = jnp.zeros((n_cslab, HIST_ROWS, LANES), F32)
        for c in range(n_cslab):
            cws_ref[c, 0:CONV_WIDTH, :] = cw_ref[:, c * LANES:(c + 1) * LANES]
            cws_ref[c, CONV_WIDTH:CONV_WIDTH + 1, :] = cb_ref[:, c * LANES:(c + 1) * LANES]

    slabs_per_blk = MXU_COLS // LANES
    for b, cols in enumerate(_col_blocks(CONV_CH)):
        u = gate_proj(wa_ref, COL_GLU_A, cols) * _sigmoid(gate_proj(wg_ref, COL_GLU_G, cols))
        convp_ref[:, cols] = u[ROW_TILE - (CONV_WIDTH - 1):, :]
        for s in range(slabs_per_blk):
            uext_ref[b * slabs_per_blk + s, HIST_ROWS:, :] = u[:, s * LANES:(s + 1) * LANES]

    lead = HIST_ROWS - (CONV_WIDTH - 1)
    chunks_per_slab = ROW_TILE // CONV_ROW_CHUNK
    chunk_shift = chunks_per_slab.bit_length() - 1
    assert chunks_per_slab == 1 << chunk_shift

    def conv_chunk(idx):
        c = lax.shift_right_logical(idx, chunk_shift)
        r0 = pl.multiple_of(jnp.bitwise_and(idx, chunks_per_slab - 1) * CONV_ROW_CHUNK, CONV_ROW_CHUNK)
        acc = jnp.broadcast_to(cws_ref[c, CONV_WIDTH:CONV_WIDTH + 1, :], (CONV_ROW_CHUNK, LANES))
        for j in range(CONV_WIDTH):
            acc = acc + uext_ref[c, pl.ds(r0 + lead + j, CONV_ROW_CHUNK), :] * cws_ref[c, j:j + 1, :]

        def store():
            cacc_ref[c, pl.ds(r0, CONV_ROW_CHUNK), :] = acc
        return store

    def qkv_block(n):
        col0 = pl.multiple_of(n * GROUP_COLS, GROUP_COLS)
        val = jnp.dot(lhs_ref[n % N_GROUPS], w_ref[:, pl.ds(col0, GROUP_COLS)], preferred_element_type=F32)
        val = val + b_ref[:, pl.ds(col0, GROUP_COLS)]

        def store():
            tail_ref[n] = val[ROW_TILE - GROUPS[0][0]:, :]
            qkv_ref[n] = (val * jnp.where(n < N_GROUPS, Q_SCALE_LOG2, 1.0)).astype(BF16)
        return store

    def trip(n, carry):
        stores = [conv_chunk(n * BLOCKS_PER_TRIP * CONV_CHUNKS_PER_TRIP + k)
                  for k in range(BLOCKS_PER_TRIP * CONV_CHUNKS_PER_TRIP)]
        stores += [qkv_block(n * BLOCKS_PER_TRIP + k) for k in range(BLOCKS_PER_TRIP)]
        for st in stores:
            st()
        return carry

    lax.fori_loop(0, jnp.minimum(i + 1, (N_QKV_BLOCKS - 1) // BLOCKS_PER_TRIP), trip, 0)
    qkv_block(N_QKV_BLOCKS - 1)()

    for c in range(n_cslab):
        uext_ref[c, 0:HIST_ROWS, :] = uext_ref[c, ROW_TILE:ROW_TILE + HIST_ROWS, :]
    _store_kv_tiles(kv0_ref, tail_ref[N_GROUPS], tail_ref[2 * N_GROUPS])

    total = sum(jnp.sum(cacc_ref[c], axis=-1, keepdims=True) for c in range(n_cslab))
    mu = total / CONV_CH
    sq = sum(jnp.sum(jnp.square(cacc_ref[c] - mu), axis=-1, keepdims=True) for c in range(n_cslab))
    rstd = lax.rsqrt(sq / CONV_CH + LN_EPS)
    for b, cols in enumerate(_col_blocks(CONV_CH)):
        conv = jnp.concatenate([cacc_ref[b * slabs_per_blk + s] for s in range(slabs_per_blk)], axis=-1)
        normed = (conv - mu) * rstd * cg_ref[:, cols] + cbeta_ref[:, cols]
        bin_ref[:, cols] = (_silu(normed) * _silu(gate_proj(wgb_ref, COL_GB, cols))).astype(BF16)

    def proj_nat(blk):
        cols = slice(blk * GROUP_COLS, (blk + 1) * GROUP_COLS)
        return jnp.dot(lhs_ref[0], w_ref[:, cols], preferred_element_type=F32) + b_ref[:, cols]

    @pl.when(i == N_ROW_TILES - 1)
    def _():
        _store_kv_tiles(kv1_ref, proj_nat(N_GROUPS + 1), proj_nat(2 * N_GROUPS + 1))

    @pl.when(i >= N_ROW_TILES - KV2_TILES)
    def _():
        _store_kv_tiles(kv2_ref, proj_nat(N_GROUPS + 2), proj_nat(2 * N_GROUPS + 2))


def _front_call(x, mod_p, w_bf, b_in, conv_w, conv_b, cn_g, cn_b):
    nat = pl.BlockSpec((ROW_TILE, GROUP_COLS), lambda i: (i, 0))
    first_kv2_tile = N_ROW_TILES - KV2_TILES
    kv_shape = lambda g: jax.ShapeDtypeStruct((GROUPS[g][0] * KV_TILE_ROWS, HEAD_DIM), F32)
    return pl.pallas_call(
        _front_kernel,
        grid=(N_ROW_TILES,),
        in_specs=[
            pl.BlockSpec((ROW_TILE, D_MODEL), lambda i: (i, 0)),
            _row_vec(3 * D_MODEL),
            _w_cols(0, QKV_COLS), _w_cols(COL_GLU_A, W_BLOCK), _w_cols(COL_GLU_G, W_BLOCK), _w_cols(COL_GB, W_BLOCK),
            _row_vec(IN_COLS),
            pl.BlockSpec((CONV_WIDTH, CONV_CH), lambda i: (0, 0)),
            _row_vec(CONV_CH), _row_vec(CONV_CH), _row_vec(CONV_CH),
        ],
        out_specs=[pl.BlockSpec((N_QKV_BLOCKS, ROW_TILE, GROUP_COLS), lambda i: (0, i, 0)),
                   pl.BlockSpec((GROUPS[0][0] * KV_TILE_ROWS, HEAD_DIM), lambda i: (0, 0)),
                   pl.BlockSpec((GROUPS[1][0] * KV_TILE_ROWS, HEAD_DIM), lambda i: (0, 0)),
                   pl.BlockSpec((ROW_TILE * KV_TILE_ROWS, HEAD_DIM),
                                lambda i: (jnp.maximum(i - first_kv2_tile, 0), 0)),
                   nat,
                   pl.BlockSpec((CONV_WIDTH - 1, CONV_CH), lambda i: (0, 0))],
        out_shape=[jax.ShapeDtypeStruct((N_QKV_BLOCKS, SEQ, GROUP_COLS), BF16),
                   kv_shape(0), kv_shape(1), kv_shape(2),
                   jax.ShapeDtypeStruct((SEQ, GROUP_COLS), BF16),
                   jax.ShapeDtypeStruct((CONV_WIDTH - 1, CONV_CH), F32)],
        scratch_shapes=[
            pltpu.VMEM((D_MODEL // LANES, ROW_TILE, LANES), F32),
            pltpu.VMEM((D_MODEL // LANES, ROW_TILE, LANES), F32),
            pltpu.VMEM((N_GROUPS, ROW_TILE, D_MODEL), BF16),
            pltpu.VMEM((CONV_CH // LANES, HIST_ROWS + ROW_TILE, LANES), F32),
            pltpu.VMEM((CONV_CH // LANES, CONV_WIDTH + 1, LANES), F32),
            pltpu.VMEM((CONV_CH // LANES, ROW_TILE, LANES), F32),
            pltpu.VMEM((N_QKV_BLOCKS, GROUPS[0][0], GROUP_COLS), F32),
        ],
        compiler_params=_params(("arbitrary",)),
        name="prompt_front",
    )(x, mod_p, w_bf, w_bf, w_bf, w_bf, b_in, conv_w, conv_b, cn_g, cn_b)


def _attn_kernel(q_ref, k_ref, v_ref, kp_ref, vp_ref, o_ref, lse_ref, bias_ref, *scr, group):
    dil = GROUPS[group][1]
    tail = KEYS_PER_BLOCK * dil
    t = pl.program_id(0)

    @pl.when(t == 0)
    def _():
        row = lax.broadcasted_iota(jnp.int32, (KEYS_PER_BLOCK, 2 * KEYS_PER_BLOCK), 0)
        col = lax.broadcasted_iota(jnp.int32, (KEYS_PER_BLOCK, 2 * KEYS_PER_BLOCK), 1)
        dist = row - col + KEYS_PER_BLOCK
        valid = (dist >= 0) & (dist <= KEYS_PER_BLOCK)
        valid_first = valid & (col >= KEYS_PER_BLOCK)
        distf = (dil * dist).astype(F32)
        for hh in range(HEADS_PER_GROUP):
            b = -(_slope(group, hh) * distf) * LOG2_E
            bias_ref[0, hh] = jnp.where(valid, b, NEG)
            bias_ref[1, hh] = jnp.where(valid_first, b, NEG)

    first_step = (t == 0).astype(jnp.int32)
    head_lane = lax.broadcasted_iota(jnp.int32, (KEYS_PER_BLOCK, LANES), 1)
    if dil > 1:
        oscr_ref, lscr_ref = scr
    def block(ref, j):
        if tail < ATTN_STEP_ROWS:
            return ref[j * KEYS_PER_BLOCK:(j + 1) * KEYS_PER_BLOCK, :]
        piece = ROW_TILE // dil
        return jnp.concatenate([ref[tt * ROW_TILE + j * piece:tt * ROW_TILE + (j + 1) * piece, :]
                                for tt in range(ATTN_STEP_ROWS // ROW_TILE)], axis=0)

    for j in range(BLOCKS_PER_ATTN_STEP):
        qb = block(q_ref, j)
        kself, vself = block(k_ref, j), block(v_ref, j)
        if j < dil:
            kprev, vprev = block(kp_ref, j), block(vp_ref, j)
        else:
            kprev, vprev = block(k_ref, j - dil), block(v_ref, j - dil)
        first = first_step if j < dil else 0
        start = (j // dil) * tail + (j % dil)
        rows = pl.ds(start, KEYS_PER_BLOCK, stride=dil) if dil > 1 else pl.ds(start, KEYS_PER_BLOCK)
        m_tile = jnp.zeros((KEYS_PER_BLOCK, LANES), F32)
        l_tile = jnp.ones((KEYS_PER_BLOCK, LANES), F32)
        for hh in range(HEADS_PER_GROUP):
            cols = slice(hh * HEAD_DIM, (hh + 1) * HEAD_DIM)
            kc = jnp.concatenate([kprev[:, cols], kself[:, cols]], axis=0)
            vc = jnp.concatenate([vprev[:, cols], vself[:, cols]], axis=0)
            s = lax.dot_general(qb[:, cols], kc, (((1,), (1,)), ((), ())), preferred_element_type=F32)
            s = s + bias_ref[first, hh]
            m = jnp.max(s, axis=-1, keepdims=True)
            p = jnp.exp2(s - m)
            l = jnp.sum(p, axis=-1, keepdims=True)
            acc = jnp.dot(p.astype(BF16), vc, preferred_element_type=F32)
            if dil > 1:
                oscr_ref[hh, rows, :] = acc
            else:
                o_ref[rows, cols] = acc.astype(BF16)
            m_tile = jnp.where(head_lane == hh, m, m_tile)
            l_tile = jnp.where(head_lane == hh, l, l_tile)
        if dil > 1:
            lscr_ref[0, rows, :] = m_tile
            lscr_ref[1, rows, :] = l_tile
        else:
            lse_ref[rows, 0:LANES] = m_tile
            lse_ref[rows, LANES:] = l_tile

    if dil > 1:
        for hh in range(HEADS_PER_GROUP):
            o_ref[:, hh * HEAD_DIM:(hh + 1) * HEAD_DIM] = oscr_ref[hh].astype(BF16)
        lse_ref[:, 0:LANES] = lscr_ref[0]
        lse_ref[:, LANES:] = lscr_ref[1]


def _attn_call(group, qkv):
    dil = GROUPS[group][1]
    tail = KEYS_PER_BLOCK * dil
    steps = SEQ // ATTN_STEP_ROWS
    tails_per_step = ATTN_STEP_ROWS // tail
    stacked = lambda blk: pl.BlockSpec((None, ATTN_STEP_ROWS, GROUP_COLS), lambda t: (blk, t, 0))
    prev_of = lambda blk: pl.BlockSpec((None, tail, GROUP_COLS),
                                       lambda t: (blk, jnp.maximum(t * tails_per_step - 1, 0), 0))
    q_blk, k_blk, v_blk = group, N_GROUPS + group, 2 * N_GROUPS + group
    cur = pl.BlockSpec((ATTN_STEP_ROWS, GROUP_COLS), lambda t: (t, 0))
    lse_blk = pl.BlockSpec((ATTN_STEP_ROWS, 2 * LANES), lambda t: (t, 0))
    scratch = [pltpu.VMEM((2, HEADS_PER_GROUP, KEYS_PER_BLOCK, 2 * KEYS_PER_BLOCK), F32)]
    if dil > 1:
        scratch += [pltpu.VMEM((HEADS_PER_GROUP, ATTN_STEP_ROWS, HEAD_DIM), F32),
                    pltpu.VMEM((2, ATTN_STEP_ROWS, LANES), F32)]
    return pl.pallas_call(
        functools.partial(_attn_kernel, group=group),
        grid=(steps,),
        in_specs=[stacked(q_blk), stacked(k_blk), stacked(v_blk), prev_of(k_blk), prev_of(v_blk)],
        out_specs=[cur, lse_blk],
        out_shape=[jax.ShapeDtypeStruct((SEQ, GROUP_COLS), BF16), jax.ShapeDtypeStruct((SEQ, 2 * LANES), F32)],
        scratch_shapes=scratch,
        compiler_params=_params(("arbitrary",)),
        name=f"prompt_attn_g{group}",
    )(qkv, qkv, qkv, qkv, qkv)


def _tail_kernel(x_ref, mod_ref, o0_ref, o1_ref, o2_ref, l0_ref, l1_ref, l2_ref, bin_ref,
                 wga_ref, wma0_ref, wma1_ref, wmb0_ref, wmb1_ref, b_ref, wpa_ref, wpb_ref, wo_ref, lng_ref, lnb_ref,
                 y_ref,
                 hb_ref, ain_ref, pre_ref, res_ref):
    shift = mod_ref[:, 0:D_MODEL]
    scale = mod_ref[:, D_MODEL:2 * D_MODEL]
    hb_ref[...] = (x_ref[...] * (1.0 + scale) + shift).astype(BF16)

    def gate_proj(wx_refs, col0, cols):
        wx_ref = wx_refs[cols.start // W_BLOCK]
        wcols = slice(cols.start % W_BLOCK, cols.start % W_BLOCK + MXU_COLS)
        bcols = slice(col0 + cols.start, col0 + cols.stop)
        return jnp.dot(hb_ref[...], wx_ref[:, wcols], preferred_element_type=F32) + b_ref[:, bcols]

    stats = (l0_ref, l1_ref, l2_ref)
    maxes = [st[:, 0:LANES] for st in stats]
    m = jnp.maximum(jnp.maximum(maxes[0], maxes[1]), maxes[2])
    es = [jnp.exp2(mg - m) for mg in maxes]
    inv = 1.0 / sum(e * st[:, LANES:] for e, st in zip(es, stats))
    weights = tuple(e * inv for e in es)
    o_refs = (o0_ref, o1_ref, o2_ref)
    for cols in _col_blocks(GROUP_COLS):
        heads = []
        for hh in range(cols.start // HEAD_DIM, cols.stop // HEAD_DIM):
            hcols = slice(hh * HEAD_DIM, (hh + 1) * HEAD_DIM)
            heads.append(sum(w[:, hh:hh + 1] * o_ref[:, hcols].astype(F32) for w, o_ref in zip(weights, o_refs)))
        ga = gate_proj((wga_ref,), COL_GA, cols)
        ain_ref[:, cols] = (jnp.concatenate(heads, axis=-1) * _silu(ga)).astype(BF16)

    for cols in _col_blocks(D_MODEL):
        a = jnp.dot(ain_ref[...], wpa_ref[:, cols], preferred_element_type=F32)
        bb = jnp.dot(bin_ref[...], wpb_ref[:, cols], preferred_element_type=F32)
        ma = gate_proj((wma0_ref, wma1_ref), COL_MA, cols)
        mb = gate_proj((wmb0_ref, wmb1_ref), COL_MB, cols)
        pre_ref[:, cols] = (_sigmoid(ma) * a + _sigmoid(mb) * bb).astype(BF16)
    for cols in _col_blocks(D_MODEL):
        gate = mod_ref[:, 2 * D_MODEL + cols.start:2 * D_MODEL + cols.stop]
        y = jnp.dot(pre_ref[...], wo_ref[:, cols], preferred_element_type=F32)
        res_ref[:, cols] = ALPHA * x_ref[:, cols] + gate * y
    mu, rstd = _row_stats(res_ref, D_MODEL)
    for cols in _col_blocks(D_MODEL):
        y_ref[:, cols] = (res_ref[:, cols] - mu) * rstd * lng_ref[:, cols] + lnb_ref[:, cols]


def _tail_call(x, mod_p, outs, lses, b_conv, w_bf, b_in, w_pa, w_pb, w_o, ln_g, ln_b):
    o_blk = pl.BlockSpec((ROW_TILE, GROUP_COLS), lambda i: (i, 0))
    lse_blk = pl.BlockSpec((ROW_TILE, 2 * LANES), lambda i: (i, 0))
    return pl.pallas_call(
        _tail_kernel,
        grid=(N_ROW_TILES,),
        in_specs=[
            pl.BlockSpec((ROW_TILE, D_MODEL), lambda i: (i, 0)),
            _row_vec(3 * D_MODEL),
            o_blk, o_blk, o_blk, lse_blk, lse_blk, lse_blk,
            pl.BlockSpec((ROW_TILE, CONV_CH), lambda i: (i, 0)),
            _w_cols(COL_GA, W_BLOCK),
            _w_cols(COL_MA, W_BLOCK), _w_cols(COL_MA + W_BLOCK, W_BLOCK),
            _w_cols(COL_MB, W_BLOCK), _w_cols(COL_MB + W_BLOCK, W_BLOCK),
            _row_vec(IN_COLS),
            _resident((GROUP_COLS, D_MODEL)),
            _resident((CONV_CH, D_MODEL)),
            _resident((D_MODEL, D_MODEL)),
            _row_vec(D_MODEL), _row_vec(D_MODEL),
        ],
        out_specs=pl.BlockSpec((ROW_TILE, D_MODEL), lambda i: (i, 0)),
        out_shape=jax.ShapeDtypeStruct((SEQ, D_MODEL), F32),
        scratch_shapes=[
            pltpu.VMEM((ROW_TILE, D_MODEL), BF16),
            pltpu.VMEM((ROW_TILE, GROUP_COLS), BF16),
            pltpu.VMEM((ROW_TILE, D_MODEL), BF16),
            pltpu.VMEM((ROW_TILE, D_MODEL), F32),
        ],
        compiler_params=_params(("arbitrary",)),
        name="prompt_tail",
    )(x, mod_p, *outs, *lses, b_conv, w_bf, w_bf, w_bf, w_bf, w_bf, b_in, w_pa, w_pb, w_o, ln_g, ln_b)


N_W_BLOCKS = 4
CAST_BLOCK = IN_COLS // N_W_BLOCKS
assert CAST_BLOCK * N_W_BLOCKS == IN_COLS and CAST_BLOCK % LANES == 0


def _sample_proj_kernel(x_ref, mod_ref, w_ref, b_ref,
                        wbf_ref, z_ref, qt0_ref, qt1_ref, qt2_ref, kv0_ref, kv1_ref, kv2_ref):
    n = pl.program_id(0)
    shift = mod_ref[:, 0:D_MODEL]
    scale = mod_ref[:, D_MODEL:2 * D_MODEL]
    hb = (x_ref[...] * (1.0 + scale) + shift).astype(BF16)
    wb = w_ref[...].astype(BF16)
    wbf_ref[...] = wb
    col0 = pl.multiple_of(n * CAST_BLOCK, LANES)
    z_ref[:, pl.ds(col0, CAST_BLOCK)] = (jnp.dot(hb, wb, preferred_element_type=F32)
                                         + b_ref[:, pl.ds(col0, CAST_BLOCK)])

    @pl.when(n == N_W_BLOCKS - 1)
    def _():
        zeros = jnp.zeros((DEC_BATCH, GROUP_COLS), F32)
        blk = lambda i: z_ref[:, i * GROUP_COLS:(i + 1) * GROUP_COLS]
        for g, (qt_ref, kv_ref) in enumerate(((qt0_ref, kv0_ref), (qt1_ref, kv1_ref), (qt2_ref, kv2_ref))):
            _store_kv_tiles(qt_ref, blk(g) * Q_SCALE, zeros)
            _store_kv_tiles(kv_ref, blk(N_GROUPS + g), blk(2 * N_GROUPS + g))


def _sample_proj_call(x_s, mod_s, w_in, b_in):
    full = lambda shape: pl.BlockSpec(shape, lambda n: (0,) * len(shape))
    tile_rows = DEC_BATCH * KV_TILE_ROWS
    tiles = jax.ShapeDtypeStruct((tile_rows, HEAD_DIM), F32)
    w_blk = pl.BlockSpec((D_MODEL, CAST_BLOCK), lambda n: (0, n))
    return pl.pallas_call(
        _sample_proj_kernel,
        grid=(N_W_BLOCKS,),
        in_specs=[full((DEC_BATCH, D_MODEL)), full((DEC_BATCH, 3 * D_MODEL)), w_blk, _row_vec(IN_COLS)],
        out_specs=[w_blk, full((DEC_BATCH, IN_COLS))] + [full((tile_rows, HEAD_DIM))] * 6,
        out_shape=[jax.ShapeDtypeStruct((D_MODEL, IN_COLS), BF16), jax.ShapeDtypeStruct((DEC_BATCH, IN_COLS), F32)]
                  + [tiles] * 6,
        compiler_params=_params(("arbitrary",)),
        name="sample_proj",
    )(x_s, mod_s, w_in, b_in)


def _sample_attn_kernel(qt0_ref, qt1_ref, qt2_ref, kv0_ref, kv1_ref, kv2_ref, c0_ref, c1_ref, c2_ref, o_ref,
                        bias_ref):
    qt_refs = (qt0_ref, qt1_ref, qt2_ref)
    kv_refs = (kv0_ref, kv1_ref, kv2_ref)
    c_refs = (c0_ref, c1_ref, c2_ref)
    half = HEADS_PER_GROUP
    tile = (KV_TILE_ROWS, HEAD_DIM)
    keys_tile = (KEYS_PER_BLOCK,) + tile

    @pl.when(pl.program_id(0) == 0)
    def _():
        key = lax.broadcasted_iota(jnp.int32, keys_tile, 0)
        sub = lax.broadcasted_iota(jnp.int32, keys_tile, 1)
        steps_back = (KEYS_PER_BLOCK - key).astype(F32)
        for g in range(N_GROUPS):
            slope_rows = jnp.zeros(keys_tile, F32)
            for hh in range(HEADS_PER_GROUP):
                slope_rows = jnp.where(sub == half + hh, _slope(g, hh), slope_rows)
            bias_ref[g] = -(slope_rows * (GROUPS[g][1] * steps_back)) * LOG2_E

    def body(b, carry):
        outs, lses = [], []
        for g in range(N_GROUPS):
            x = c_refs[g][b]
            qt = qt_refs[g][b] * LOG2_E
            kvn = kv_refs[g][b]
            s = jnp.broadcast_to(jnp.sum(x * qt[None], axis=-1, keepdims=True), keys_tile)
            s = pltpu.roll(s, half, axis=1) + bias_ref[g]
            s_self = jnp.broadcast_to(jnp.sum(qt * kvn, axis=-1, keepdims=True), tile)
            s_self = pltpu.roll(s_self, half, axis=0)
            m = jnp.maximum(jnp.max(s, axis=0), s_self)
            p = jnp.exp2(s - m[None])
            p_self = jnp.exp2(s_self - m)
            l = jnp.sum(p, axis=0) + p_self
            acc = jnp.sum(x * p, axis=0) + p_self * kvn
            outs.append(acc / l)
            lses.append(m + jnp.log2(l))
        mx = jnp.maximum(jnp.maximum(lses[0], lses[1]), lses[2])
        es = [jnp.exp2(ls - mx) for ls in lses]
        o_ref[b] = (es[0] * outs[0] + es[1] * outs[1] + es[2] * outs[2]) / (es[0] + es[1] + es[2])
        return carry

    lax.fori_loop(0, SAMPLE_BLOCK, body, 0, unroll=SAMPLE_UNROLL)


def _sample_attn_call(qts, kvs, caches):
    views = [c.reshape(DEC_BATCH, KEYS_PER_BLOCK, GROUPS[g][1], KV_TILE_ROWS, HEAD_DIM) for g, c in enumerate(caches)]
    tile3 = lambda a: a.reshape(DEC_BATCH, KV_TILE_ROWS, HEAD_DIM)
    tile_blk = pl.BlockSpec((SAMPLE_BLOCK, KV_TILE_ROWS, HEAD_DIM), lambda i: (i, 0, 0))
    cache_blk = pl.BlockSpec((SAMPLE_BLOCK, KEYS_PER_BLOCK, None, KV_TILE_ROWS, HEAD_DIM), lambda i: (i, 0, 0, 0, 0))
    return pl.pallas_call(
        _sample_attn_kernel,
        grid=(DEC_BATCH // SAMPLE_BLOCK,),
        in_specs=[tile_blk] * 6 + [cache_blk] * 3,
        out_specs=tile_blk,
        out_shape=jax.ShapeDtypeStruct((DEC_BATCH, KV_TILE_ROWS, HEAD_DIM), F32),
        scratch_shapes=[pltpu.VMEM((N_GROUPS, KEYS_PER_BLOCK, KV_TILE_ROWS, HEAD_DIM), F32)],
        compiler_params=_params(("arbitrary",)),
        name="sample_attn",
    )(*[tile3(a) for a in qts], *[tile3(a) for a in kvs], *views)


def _sample_tail_kernel(x_ref, mod_ref, o_ref, zr_ref, st_ref, cw_ref, cb_ref, cg_ref, cbeta_ref,
                        wpa_ref, wpb_ref, wo_ref, lng_ref, lnb_ref, y_ref, convs_ref):
    gate = mod_ref[:, 2 * D_MODEL:3 * D_MODEL]
    z = lambda col0, width: zr_ref[:, col0:col0 + width]
    u = z(COL_GLU_A, CONV_CH) * _sigmoid(z(COL_GLU_G, CONV_CH))
    hist = CONV_WIDTH - 1
    acc = cb_ref[...] + u * cw_ref[hist:hist + 1, :]
    for j in range(hist):
        acc = acc + st_ref[j] * cw_ref[j:j + 1, :]
    conv_out = _silu(_layer_norm(acc, cg_ref[...], cbeta_ref[...]))
    for j in range(hist - 1):
        convs_ref[j] = st_ref[j + 1]
    convs_ref[hist - 1] = u
    o_attn = jnp.concatenate(
        [o_ref[pl.ds(HEADS_PER_GROUP + hh, DEC_BATCH, stride=KV_TILE_ROWS), :] for hh in range(HEADS_PER_GROUP)],
        axis=-1)
    a = jnp.dot((o_attn * _silu(z(COL_GA, GROUP_COLS))).astype(BF16), wpa_ref[...], preferred_element_type=F32)
    b = jnp.dot((conv_out * _silu(z(COL_GB, CONV_CH))).astype(BF16), wpb_ref[...], preferred_element_type=F32)
    pre = (_sigmoid(z(COL_MA, D_MODEL)) * a + _sigmoid(z(COL_MB, D_MODEL)) * b).astype(BF16)
    y = jnp.dot(pre, wo_ref[...], preferred_element_type=F32)
    y_ref[...] = _layer_norm(ALPHA * x_ref[...] + gate * y, lng_ref[...], lnb_ref[...])


def _sample_tail_call(x_s, mod_s, o_s, zr, state_t, conv_w, conv_b, cn_g, cn_b, w_pa, w_pb, w_o, ln_g, ln_b):
    full = lambda shape: pl.BlockSpec(shape, lambda i: (0,) * len(shape))
    args = (x_s, mod_s, o_s, zr, state_t, conv_w, conv_b, cn_g, cn_b, w_pa, w_pb, w_o, ln_g, ln_b)
    return pl.pallas_call(
        _sample_tail_kernel,
        grid=(1,),
        in_specs=[full(a.shape) for a in args],
        out_specs=[full((DEC_BATCH, D_MODEL)), full(state_t.shape)],
        out_shape=[jax.ShapeDtypeStruct((DEC_BATCH, D_MODEL), F32),
                   jax.ShapeDtypeStruct(state_t.shape, F32)],
        compiler_params=_params(("arbitrary",)),
        name="sample_tail",
    )(*args)


def kernel(x_prompt, x_sample, c_prompt, c_sample, cache_kv_w128, cache_kv_w512, cache_kv_w2048, state_conv,
           w_c, b_c, w_in, b_in, conv_w, conv_b, conv_norm_g, conv_norm_b, w_pa, w_pb, w_o, ln_g, ln_b):
    assert x_prompt.shape == (1, SEQ, D_MODEL) and x_sample.shape == (DEC_BATCH, 1, D_MODEL)
    assert w_in.shape == (D_MODEL, IN_COLS)
    caches = (cache_kv_w128, cache_kv_w512, cache_kv_w2048)
    for (window, _), c in zip(GROUPS, caches):
        assert c.shape == (DEC_BATCH, window, 2, HEADS_PER_GROUP, HEAD_DIM)

    w_pa_b, w_pb_b, w_o_b = w_pa.astype(BF16), w_pb.astype(BF16), w_o.astype(BF16)
    vec = lambda a: a.reshape(1, -1)
    b_row = vec(b_in)
    conv_args = (conv_w, vec(conv_b), vec(conv_norm_g), vec(conv_norm_b))
    out_args = (w_pa_b, w_pb_b, w_o_b, vec(ln_g), vec(ln_b))

    mod_p, mod_s = _mod_call(c_prompt, c_sample, w_c, b_c)
    x_s = x_sample.reshape(DEC_BATCH, D_MODEL)
    w_bf, z_s, qt0, qt1, qt2, kvs0, kvs1, kvs2 = _sample_proj_call(x_s, mod_s, w_in, b_row)

    x_p = x_prompt.reshape(SEQ, D_MODEL)
    qkv, kv0, kv1, kv2, b_conv, conv_p = _front_call(x_p, mod_p, w_bf, b_row, *conv_args)
    attn = [_attn_call(g, qkv) for g in range(N_GROUPS)]
    y_p = _tail_call(x_p, mod_p, [a[0] for a in attn], [a[1] for a in attn], b_conv, w_bf, b_row, *out_args)

    o_s = _sample_attn_call((qt0, qt1, qt2), (kvs0, kvs1, kvs2), caches)
    o_s = o_s.reshape(DEC_BATCH * KV_TILE_ROWS, HEAD_DIM)
    state_t = jnp.transpose(state_conv, (1, 0, 2))
    y_s, conv_s_t = _sample_tail_call(x_s, mod_s, o_s, z_s, state_t, *conv_args, *out_args)

    kv_shape_p = lambda keep: (1, keep, 2, HEADS_PER_GROUP, HEAD_DIM)
    kv_shape_s = (DEC_BATCH, 1, 2, HEADS_PER_GROUP, HEAD_DIM)
    return (y_p.reshape(1, SEQ, D_MODEL),
            y_s.reshape(DEC_BATCH, 1, D_MODEL),
            kv0.reshape(kv_shape_p(GROUPS[0][0])),
            kv1.reshape(kv_shape_p(GROUPS[1][0])),
            kv2.reshape(kv_shape_p(GROUPS[2][0])),
            conv_p.reshape(1, CONV_WIDTH - 1, CONV_CH),
            kvs0.reshape(kv_shape_s), kvs1.reshape(kv_shape_s), kvs2.reshape(kv_shape_s),
            jnp.transpose(conv_s_t, (1, 0, 2)))
```

```python
import functools

import jax
import jax.numpy as jnp
from jax import lax
from jax.experimental import pallas as pl
from jax.experimental.pallas import tpu as pltpu

F32 = jnp.float32
BF16 = jnp.bfloat16

D_MODEL = 1024
SEQ = 16384
DEC_BATCH = 128
DEPTH = 1
HEAD_DIM = 128
HEADS_PER_GROUP = 4
GROUPS = ((128, 1), (512, 4), (2048, 16))
N_GROUPS = len(GROUPS)
N_HEADS = N_GROUPS * HEADS_PER_GROUP
GROUP_COLS = HEADS_PER_GROUP * HEAD_DIM
ATTN_QKV = N_HEADS * HEAD_DIM
QKV_COLS = 3 * ATTN_QKV
CONV_CH = D_MODEL // 2
CONV_WIDTH = 31
COL_GA = QKV_COLS
COL_GLU_A = COL_GA + GROUP_COLS
COL_GLU_G = COL_GLU_A + CONV_CH
COL_GB = COL_GLU_G + CONV_CH
COL_MA = COL_GB + CONV_CH
COL_MB = COL_MA + D_MODEL
IN_COLS = COL_MB + D_MODEL
ALPHA = (2.0 * DEPTH) ** 0.25
LN_EPS = 1e-5
NEG = -1e30
Q_SCALE = HEAD_DIM ** -0.5
LOG2_E = 1.4426950408889634
Q_SCALE_LOG2 = Q_SCALE * LOG2_E
KEYS_PER_BLOCK = 128
KV_TILE_ROWS = 2 * HEADS_PER_GROUP

LANES = 128
MXU_COLS = 256
W_BLOCK = 512
ROW_TILE = 512
ATTN_STEP_ROWS = 2048
BLOCKS_PER_ATTN_STEP = ATTN_STEP_ROWS // KEYS_PER_BLOCK
CONV_ROW_CHUNK = 32
HIST_ROWS = 32
SAMPLE_BLOCK = 8
SAMPLE_UNROLL = 4
VMEM_LIMIT_BYTES = 56 * 1024 * 1024


def _slope(group, head):
    return 2.0 ** (-8.0 * (group * HEADS_PER_GROUP + head + 1) / N_HEADS)


def _sigmoid(x):
    return 1.0 / (1.0 + jnp.exp(-x))


def _silu(x):
    return x * _sigmoid(x)


def _layer_norm(x, g, b):
    mu = jnp.mean(x, axis=-1, keepdims=True)
    xc = x - mu
    var = jnp.mean(xc * xc, axis=-1, keepdims=True)
    return xc * lax.rsqrt(var + LN_EPS) * g + b


def _col_blocks(width):
    return [slice(b * MXU_COLS, (b + 1) * MXU_COLS) for b in range(width // MXU_COLS)]


def _row_stats(ref, width):
    total = sum(jnp.sum(ref[:, cols], axis=-1, keepdims=True) for cols in _col_blocks(width))
    mu = total / width
    sq = sum(jnp.sum(jnp.square(ref[:, cols] - mu), axis=-1, keepdims=True) for cols in _col_blocks(width))
    return mu, lax.rsqrt(sq / width + LN_EPS)


def _resident(shape):
    return pl.BlockSpec(shape, lambda *_: (0,) * len(shape), pipeline_mode=pl.Buffered(1))


def _w_cols(col0, width):
    assert col0 % width == 0
    return pl.BlockSpec((D_MODEL, width), lambda *_: (0, col0 // width), pipeline_mode=pl.Buffered(1))


def _row_vec(n):
    return pl.BlockSpec((1, n), lambda *_: (0, 0))


def _params(semantics):
    return pltpu.CompilerParams(dimension_semantics=semantics, vmem_limit_bytes=VMEM_LIMIT_BYTES)


def _store_kv_tiles(ref, k, v):
    rows = k.shape[0]
    for hh in range(HEADS_PER_GROUP):
        cols = slice(hh * HEAD_DIM, (hh + 1) * HEAD_DIM)
        ref[pl.ds(hh, rows, stride=KV_TILE_ROWS), :] = k[:, cols]
        ref[pl.ds(HEADS_PER_GROUP + hh, rows, stride=KV_TILE_ROWS), :] = v[:, cols]


def _mod_kernel(cp_ref, cs_ref, w_ref, b_ref, mp_ref, ms_ref):
    w = w_ref[...].astype(BF16)
    cp = jnp.broadcast_to(cp_ref[...], (8, D_MODEL)).astype(BF16)
    mp_ref[...] = jnp.dot(cp, w, preferred_element_type=F32)[0:1] + b_ref[...]
    ms_ref[...] = jnp.dot(cs_ref[...].astype(BF16), w, preferred_element_type=F32) + b_ref[...]


def _mod_call(c_prompt, c_sample, w_c, b_c):
    n_blk = 3
    return pl.pallas_call(
        _mod_kernel,
        grid=(n_blk,),
        in_specs=[
            pl.BlockSpec((1, D_MODEL), lambda n: (0, 0)),
            pl.BlockSpec((DEC_BATCH, D_MODEL), lambda n: (0, 0)),
            pl.BlockSpec((D_MODEL, D_MODEL), lambda n: (0, n)),
            pl.BlockSpec((1, D_MODEL), lambda n: (0, n)),
        ],
        out_specs=[
            pl.BlockSpec((1, D_MODEL), lambda n: (0, n)),
            pl.BlockSpec((DEC_BATCH, D_MODEL), lambda n: (0, n)),
        ],
        out_shape=[
            jax.ShapeDtypeStruct((1, 3 * D_MODEL), F32),
            jax.ShapeDtypeStruct((DEC_BATCH, 3 * D_MODEL), F32),
        ],
        compiler_params=_params(("arbitrary",)),
        name="adaln_mod",
    )(c_prompt, c_sample, w_c, b_c.reshape(1, 3 * D_MODEL))


N_ROW_TILES = SEQ // ROW_TILE
KV2_TILES = GROUPS[2][0] // ROW_TILE


N_QKV_BLOCKS = QKV_COLS // GROUP_COLS
CONV_CHUNKS = (CONV_CH // LANES) * (ROW_TILE // CONV_ROW_CHUNK)
CONV_CHUNKS_PER_TRIP = 8
BLOCKS_PER_TRIP = 8
assert CONV_CHUNKS == CONV_CHUNKS_PER_TRIP * (N_QKV_BLOCKS - 1) and (N_QKV_BLOCKS - 1) % BLOCKS_PER_TRIP == 0


def _front_kernel(x_ref, mod_ref, w_ref, wa_ref, wg_ref, wgb_ref, b_ref, cw_ref, cb_ref, cg_ref, cbeta_ref,
                  qkv_ref, kv0_ref, kv1_ref, kv2_ref, bin_ref, convp_ref,
                  hs_ref, hq_ref, lhs_ref, uext_ref, cws_ref, cacc_ref, tail_ref):
    i = pl.program_id(0)
    shift = mod_ref[:, 0:D_MODEL]
    scale = mod_ref[:, D_MODEL:2 * D_MODEL]
    h = x_ref[...] * (1.0 + scale) + shift
    lhs_ref[0] = h.astype(BF16)

    n_slab = D_MODEL // LANES
    d1, d2 = GROUPS[1][1], GROUPS[2][1]
    assert d2 == d1 * d1
    r1, r2 = ROW_TILE // d1, ROW_TILE // d2
    for c in range(n_slab):
        cols = slice(c * LANES, (c + 1) * LANES)
        hs_ref[c] = h[:, cols]
        for b in range(d1):
            rows_b = hs_ref[c, pl.ds(b, r1, stride=d1), :]
            hq_ref[c, b * r1:(b + 1) * r1, :] = rows_b
            lhs_ref[1, b * r1:(b + 1) * r1, cols] = rows_b.astype(BF16)
        for r in range(d2):
            a, b = r // d1, r % d1
            lhs_ref[2, r * r2:(r + 1) * r2, cols] = hq_ref[c, pl.ds(b * r1 + a, r2, stride=d1), :].astype(BF16)

    def gate_proj(wx_ref, col0, cols):
        bcols = slice(col0 + cols.start, col0 + cols.stop)
        return jnp.dot(lhs_ref[0], wx_ref[:, cols], preferred_element_type=F32) + b_ref[:, bcols]

    n_cslab = CONV_CH // LANES

    @pl.when(i == 0)
    def _():
        uext_ref[:, 0:HIST_ROWS, :] = jnp.zeros((n_cslab, HIST_ROWS, LANES), F32)
        for c in range(n_cslab):
            cws_ref[c, 0:CONV_WIDTH, :] = cw_ref[:, c * LANES:(c + 1) * LANES]
            cws_ref[c, CONV_WIDTH:CONV_WIDTH + 1, :] = cb_ref[:, c * LANES:(c + 1) * LANES]

    slabs_per_blk = MXU_COLS // LANES
    for b, cols in enumerate(_col_blocks(CONV_CH)):
        u = gate_proj(wa_ref, COL_GLU_A, cols) * _sigmoid(gate_proj(wg_ref, COL_GLU_G, cols))
        convp_ref[:, cols] = u[ROW_TILE - (CONV_WIDTH - 1):, :]
        for s in range(slabs_per_blk):
            uext_ref[b * slabs_per_blk + s, HIST_ROWS:, :] = u[:, s * LANES:(s + 1) * LANES]

    lead = HIST_ROWS - (CONV_WIDTH - 1)
    chunks_per_slab = ROW_TILE // CONV_ROW_CHUNK
    chunk_shift = chunks_per_slab.bit_length() - 1
    assert chunks_per_slab == 1 << chunk_shift

    def conv_chunk(idx):
        c = lax.shift_right_logical(idx, chunk_shift)
        r0 = pl.multiple_of(jnp.bitwise_and(idx, chunks_per_slab - 1) * CONV_ROW_CHUNK, CONV_ROW_CHUNK)
        acc = jnp.broadcast_to(cws_ref[c, CONV_WIDTH:CONV_WIDTH + 1, :], (CONV_ROW_CHUNK, LANES))
        for j in range(CONV_WIDTH):
            acc = acc + uext_ref[c, pl.ds(r0 + lead + j, CONV_ROW_CHUNK), :] * cws_ref[c, j:j + 1, :]

        def store():
            cacc_ref[c, pl.ds(r0, CONV_ROW_CHUNK), :] = acc
        return store

    def qkv_block(n):
        col0 = pl.multiple_of(n * GROUP_COLS, GROUP_COLS)
        val = jnp.dot(lhs_ref[n % N_GROUPS], w_ref[:, pl.ds(col0, GROUP_COLS)], preferred_element_type=F32)
        val = val + b_ref[:, pl.ds(col0, GROUP_COLS)]

        def store():
            tail_ref[n] = val[ROW_TILE - GROUPS[0][0]:, :]
            qkv_ref[n] = (val * jnp.where(n < N_GROUPS, Q_SCALE_LOG2, 1.0)).astype(BF16)
        return store

    def trip(n, carry):
        stores = [conv_chunk(n * BLOCKS_PER_TRIP * CONV_CHUNKS_PER_TRIP + k)
                  for k in range(BLOCKS_PER_TRIP * CONV_CHUNKS_PER_TRIP)]
        stores += [qkv_block(n * BLOCKS_PER_TRIP + k) for k in range(BLOCKS_PER_TRIP)]
        for st in stores:
            st()
        return carry

    lax.fori_loop(0, jnp.minimum(i + 1, (N_QKV_BLOCKS - 1) // BLOCKS_PER_TRIP), trip, 0)
    qkv_block(N_QKV_BLOCKS - 1)()

    for c in range(n_cslab):
        uext_ref[c, 0:HIST_ROWS, :] = uext_ref[c, ROW_TILE:ROW_TILE + HIST_ROWS, :]
    _store_kv_tiles(kv0_ref, tail_ref[N_GROUPS], tail_ref[2 * N_GROUPS])

    total = sum(jnp.sum(cacc_ref[c], axis=-1, keepdims=True) for c in range(n_cslab))
    mu = total / CONV_CH
    sq = sum(jnp.sum(jnp.square(cacc_ref[c] - mu), axis=-1, keepdims=True) for c in range(n_cslab))
    rstd = lax.rsqrt(sq / CONV_CH + LN_EPS)
    for b, cols in enumerate(_col_blocks(CONV_CH)):
        conv = jnp.concatenate([cacc_ref[b * slabs_per_blk + s] for s in range(slabs_per_blk)], axis=-1)
        normed = (conv - mu) * rstd * cg_ref[:, cols] + cbeta_ref[:, cols]
        bin_ref[:, cols] = (_silu(normed) * _silu(gate_proj(wgb_ref, COL_GB, cols))).astype(BF16)

    def proj_nat(blk):
        cols = slice(blk * GROUP_COLS, (blk + 1) * GROUP_COLS)
        return jnp.dot(lhs_ref[0], w_ref[:, cols], preferred_element_type=F32) + b_ref[:, cols]

    @pl.when(i == N_ROW_TILES - 1)
    def _():
        _store_kv_tiles(kv1_ref, proj_nat(N_GROUPS + 1), proj_nat(2 * N_GROUPS + 1))

    @pl.when(i >= N_ROW_TILES - KV2_TILES)
    def _():
        _store_kv_tiles(kv2_ref, proj_nat(N_GROUPS + 2), proj_nat(2 * N_GROUPS + 2))


def _front_call(x, mod_p, w_bf, b_in, conv_w, conv_b, cn_g, cn_b):
    nat = pl.BlockSpec((ROW_TILE, GROUP_COLS), lambda i: (i, 0))
    first_kv2_tile = N_ROW_TILES - KV2_TILES
    kv_shape = lambda g: jax.ShapeDtypeStruct((GROUPS[g][0] * KV_TILE_ROWS, HEAD_DIM), F32)
    return pl.pallas_call(
        _front_kernel,
        grid=(N_ROW_TILES,),
        in_specs=[
            pl.BlockSpec((ROW_TILE, D_MODEL), lambda i: (i, 0)),
            _row_vec(3 * D_MODEL),
            _w_cols(0, QKV_COLS), _w_cols(COL_GLU_A, W_BLOCK), _w_cols(COL_GLU_G, W_BLOCK), _w_cols(COL_GB, W_BLOCK),
            _row_vec(IN_COLS),
            pl.BlockSpec((CONV_WIDTH, CONV_CH), lambda i: (0, 0)),
            _row_vec(CONV_CH), _row_vec(CONV_CH), _row_vec(CONV_CH),
        ],
        out_specs=[pl.BlockSpec((N_QKV_BLOCKS, ROW_TILE, GROUP_COLS), lambda i: (0, i, 0)),
                   pl.BlockSpec((GROUPS[0][0] * KV_TILE_ROWS, HEAD_DIM), lambda i: (0, 0)),
                   pl.BlockSpec((GROUPS[1][0] * KV_TILE_ROWS, HEAD_DIM), lambda i: (0, 0)),
                   pl.BlockSpec((ROW_TILE * KV_TILE_ROWS, HEAD_DIM),
                                lambda i: (jnp.maximum(i - first_kv2_tile, 0), 0)),
                   nat,
                   pl.BlockSpec((CONV_WIDTH - 1, CONV_CH), lambda i: (0, 0))],
        out_shape=[jax.ShapeDtypeStruct((N_QKV_BLOCKS, SEQ, GROUP_COLS), BF16),
                   kv_shape(0), kv_shape(1), kv_shape(2),
                   jax.ShapeDtypeStruct((SEQ, GROUP_COLS), BF16),
                   jax.ShapeDtypeStruct((CONV_WIDTH - 1, CONV_CH), F32)],
        scratch_shapes=[
            pltpu.VMEM((D_MODEL // LANES, ROW_TILE, LANES), F32),
            pltpu.VMEM((D_MODEL // LANES, ROW_TILE, LANES), F32),
            pltpu.VMEM((N_GROUPS, ROW_TILE, D_MODEL), BF16),
            pltpu.VMEM((CONV_CH // LANES, HIST_ROWS + ROW_TILE, LANES), F32),
            pltpu.VMEM((CONV_CH // LANES, CONV_WIDTH + 1, LANES), F32),
            pltpu.VMEM((CONV_CH // LANES, ROW_TILE, LANES), F32),
            pltpu.VMEM((N_QKV_BLOCKS, GROUPS[0][0], GROUP_COLS), F32),
        ],
        compiler_params=_params(("arbitrary",)),
        name="prompt_front",
    )(x, mod_p, w_bf, w_bf, w_bf, w_bf, b_in, conv_w, conv_b, cn_g, cn_b)


def _attn_kernel(q_ref, k_ref, v_ref, kp_ref, vp_ref, o_ref, lse_ref, bias_ref, *scr, group):
    dil = GROUPS[group][1]
    tail = KEYS_PER_BLOCK * dil
    t = pl.program_id(0)

    @pl.when(t == 0)
    def _():
        row = lax.broadcasted_iota(jnp.int32, (KEYS_PER_BLOCK, 2 * KEYS_PER_BLOCK), 0)
        col = lax.broadcasted_iota(jnp.int32, (KEYS_PER_BLOCK, 2 * KEYS_PER_BLOCK), 1)
        dist = row - col + KEYS_PER_BLOCK
        valid = (dist >= 0) & (dist <= KEYS_PER_BLOCK)
        valid_first = valid & (col >= KEYS_PER_BLOCK)
        distf = (dil * dist).astype(F32)
        for hh in range(HEADS_PER_GROUP):
            b = -(_slope(group, hh) * distf) * LOG2_E
            bias_ref[0, hh] = jnp.where(valid, b, NEG)
            bias_ref[1, hh] = jnp.where(valid_first, b, NEG)

    first_step = (t == 0).astype(jnp.int32)
    head_lane = lax.broadcasted_iota(jnp.int32, (KEYS_PER_BLOCK, LANES), 1)
    if dil > 1:
        oscr_ref, lscr_ref = scr
    def block(ref, j):
        if tail < ATTN_STEP_ROWS:
            return ref[j * KEYS_PER_BLOCK:(j + 1) * KEYS_PER_BLOCK, :]
        piece = ROW_TILE // dil
        return jnp.concatenate([ref[tt * ROW_TILE + j * piece:tt * ROW_TILE + (j + 1) * piece, :]
                                for tt in range(ATTN_STEP_ROWS // ROW_TILE)], axis=0)

    for j in range(BLOCKS_PER_ATTN_STEP):
        qb = block(q_ref, j)
        kself, vself = block(k_ref, j), block(v_ref, j)
        if j < dil:
            kprev, vprev = block(kp_ref, j), block(vp_ref, j)
        else:
            kprev, vprev = block(k_ref, j - dil), block(v_ref, j - dil)
        first = first_step if j < dil else 0
        start = (j // dil) * tail + (j % dil)
        rows = pl.ds(start, KEYS_PER_BLOCK, stride=dil) if dil > 1 else pl.ds(start, KEYS_PER_BLOCK)
        m_tile = jnp.zeros((KEYS_PER_BLOCK, LANES), F32)
        l_tile = jnp.ones((KEYS_PER_BLOCK, LANES), F32)
        for hh in range(HEADS_PER_GROUP):
            cols = slice(hh * HEAD_DIM, (hh + 1) * HEAD_DIM)
            kc = jnp.concatenate([kprev[:, cols], kself[:, cols]], axis=0)
            vc = jnp.concatenate([vprev[:, cols], vself[:, cols]], axis=0)
            s = lax.dot_general(qb[:, cols], kc, (((1,), (1,)), ((), ())), preferred_element_type=F32)
            s = s + bias_ref[first, hh]
            m = jnp.max(s, axis=-1, keepdims=True)
            p = jnp.exp2(s - m)
            l = jnp.sum(p, axis=-1, keepdims=True)
            acc = jnp.dot(p.astype(BF16), vc, preferred_element_type=F32)
            if dil > 1:
                oscr_ref[hh, rows, :] = acc
            else:
                o_ref[rows, cols] = acc.astype(BF16)
            m_tile = jnp.where(head_lane == hh, m, m_tile)
            l_tile = jnp.where(head_lane == hh, l, l_tile)
        if dil > 1:
            lscr_ref[0, rows, :] = m_tile
            lscr_ref[1, rows, :] = l_tile
        else:
            lse_ref[rows, 0:LANES] = m_tile
            lse_ref[rows, LANES:] = l_tile

    if dil > 1:
        for hh in range(HEADS_PER_GROUP):
            o_ref[:, hh * HEAD_DIM:(hh + 1) * HEAD_DIM] = oscr_ref[hh].astype(BF16)
        lse_ref[:, 0:LANES] = lscr_ref[0]
        lse_ref[:, LANES:] = lscr_ref[1]


def _attn_call(group, qkv):
    dil = GROUPS[group][1]
    tail = KEYS_PER_BLOCK * dil
    steps = SEQ // ATTN_STEP_ROWS
    tails_per_step = ATTN_STEP_ROWS // tail
    stacked = lambda blk: pl.BlockSpec((None, ATTN_STEP_ROWS, GROUP_COLS), lambda t: (blk, t, 0))
    prev_of = lambda blk: pl.BlockSpec((None, tail, GROUP_COLS),
                                       lambda t: (blk, jnp.maximum(t * tails_per_step - 1, 0), 0))
    q_blk, k_blk, v_blk = group, N_GROUPS + group, 2 * N_GROUPS + group
    cur = pl.BlockSpec((ATTN_STEP_ROWS, GROUP_COLS), lambda t: (t, 0))
    lse_blk = pl.BlockSpec((ATTN_STEP_ROWS, 2 * LANES), lambda t: (t, 0))
    scratch = [pltpu.VMEM((2, HEADS_PER_GROUP, KEYS_PER_BLOCK, 2 * KEYS_PER_BLOCK), F32)]
    if dil > 1:
        scratch += [pltpu.VMEM((HEADS_PER_GROUP, ATTN_STEP_ROWS, HEAD_DIM), F32),
                    pltpu.VMEM((2, ATTN_STEP_ROWS, LANES), F32)]
    return pl.pallas_call(
        functools.partial(_attn_kernel, group=group),
        grid=(steps,),
        in_specs=[stacked(q_blk), stacked(k_blk), stacked(v_blk), prev_of(k_blk), prev_of(v_blk)],
        out_specs=[cur, lse_blk],
        out_shape=[jax.ShapeDtypeStruct((SEQ, GROUP_COLS), BF16), jax.ShapeDtypeStruct((SEQ, 2 * LANES), F32)],
        scratch_shapes=scratch,
        compiler_params=_params(("arbitrary",)),
        name=f"prompt_attn_g{group}",
    )(qkv, qkv, qkv, qkv, qkv)


def _tail_kernel(x_ref, mod_ref, o0_ref, o1_ref, o2_ref, l0_ref, l1_ref, l2_ref, bin_ref,
                 wga_ref, wma0_ref, wma1_ref, wmb0_ref, wmb1_ref, b_ref, wpa_ref, wpb_ref, wo_ref, lng_ref, lnb_ref,
                 y_ref,
                 hb_ref, ain_ref, pre_ref, res_ref):
    shift = mod_ref[:, 0:D_MODEL]
    scale = mod_ref[:, D_MODEL:2 * D_MODEL]
    hb_ref[...] = (x_ref[...] * (1.0 + scale) + shift).astype(BF16)

    def gate_proj(wx_refs, col0, cols):
        wx_ref = wx_refs[cols.start // W_BLOCK]
        wcols = slice(cols.start % W_BLOCK, cols.start % W_BLOCK + MXU_COLS)
        bcols = slice(col0 + cols.start, col0 + cols.stop)
        return jnp.dot(hb_ref[...], wx_ref[:, wcols], preferred_element_type=F32) + b_ref[:, bcols]

    stats = (l0_ref, l1_ref, l2_ref)
    maxes = [st[:, 0:LANES] for st in stats]
    m = jnp.maximum(jnp.maximum(maxes[0], maxes[1]), maxes[2])
    es = [jnp.exp2(mg - m) for mg in maxes]
    inv = 1.0 / sum(e * st[:, LANES:] for e, st in zip(es, stats))
    weights = tuple(e * inv for e in es)
    o_refs = (o0_ref, o1_ref, o2_ref)
    for cols in _col_blocks(GROUP_COLS):
        heads = []
        for hh in range(cols.start // HEAD_DIM, cols.stop // HEAD_DIM):
            hcols = slice(hh * HEAD_DIM, (hh + 1) * HEAD_DIM)
            heads.append(sum(w[:, hh:hh + 1] * o_ref[:, hcols].astype(F32) for w, o_ref in zip(weights, o_refs)))
        ga = gate_proj((wga_ref,), COL_GA, cols)
        ain_ref[:, cols] = (jnp.concatenate(heads, axis=-1) * _silu(ga)).astype(BF16)

    for cols in _col_blocks(D_MODEL):
        a = jnp.dot(ain_ref[...], wpa_ref[:, cols], preferred_element_type=F32)
        bb = jnp.dot(bin_ref[...], wpb_ref[:, cols], preferred_element_type=F32)
        ma = gate_proj((wma0_ref, wma1_ref), COL_MA, cols)
        mb = gate_proj((wmb0_ref, wmb1_ref), COL_MB, cols)
        pre_ref[:, cols] = (_sigmoid(ma) * a + _sigmoid(mb) * bb).astype(BF16)
    for cols in _col_blocks(D_MODEL):
        gate = mod_ref[:, 2 * D_MODEL + cols.start:2 * D_MODEL + cols.stop]
        y = jnp.dot(pre_ref[...], wo_ref[:, cols], preferred_element_type=F32)
        res_ref[:, cols] = ALPHA * x_ref[:, cols] + gate * y
    mu, rstd = _row_stats(res_ref, D_MODEL)
    for cols in _col_blocks(D_MODEL):
        y_ref[:, cols] = (res_ref[:, cols] - mu) * rstd * lng_ref[:, cols] + lnb_ref[:, cols]


def _tail_call(x, mod_p, outs, lses, b_conv, w_bf, b_in, w_pa, w_pb, w_o, ln_g, ln_b):
    o_blk = pl.BlockSpec((ROW_TILE, GROUP_COLS), lambda i: (i, 0))
    lse_blk = pl.BlockSpec((ROW_TILE, 2 * LANES), lambda i: (i, 0))
    return pl.pallas_call(
        _tail_kernel,
        grid=(N_ROW_TILES,),
        in_specs=[
            pl.BlockSpec((ROW_TILE, D_MODEL), lambda i: (i, 0)),
            _row_vec(3 * D_MODEL),
            o_blk, o_blk, o_blk, lse_blk, lse_blk, lse_blk,
            pl.BlockSpec((ROW_TILE, CONV_CH), lambda i: (i, 0)),
            _w_cols(COL_GA, W_BLOCK),
            _w_cols(COL_MA, W_BLOCK), _w_cols(COL_MA + W_BLOCK, W_BLOCK),
            _w_cols(COL_MB, W_BLOCK), _w_cols(COL_MB + W_BLOCK, W_BLOCK),
            _row_vec(IN_COLS),
            _resident((GROUP_COLS, D_MODEL)),
            _resident((CONV_CH, D_MODEL)),
            _resident((D_MODEL, D_MODEL)),
            _row_vec(D_MODEL), _row_vec(D_MODEL),
        ],
        out_specs=pl.BlockSpec((ROW_TILE, D_MODEL), lambda i: (i, 0)),
        out_shape=jax.ShapeDtypeStruct((SEQ, D_MODEL), F32),
        scratch_shapes=[
            pltpu.VMEM((ROW_TILE, D_MODEL), BF16),
            pltpu.VMEM((ROW_TILE, GROUP_COLS), BF16),
            pltpu.VMEM((ROW_TILE, D_MODEL), BF16),
            pltpu.VMEM((ROW_TILE, D_MODEL), F32),
        ],
        compiler_params=_params(("arbitrary",)),
        name="prompt_tail",
    )(x, mod_p, *outs, *lses, b_conv, w_bf, w_bf, w_bf, w_bf, w_bf, b_in, w_pa, w_pb, w_o, ln_g, ln_b)


N_W_BLOCKS = 4
CAST_BLOCK = IN_COLS // N_W_BLOCKS
assert CAST_BLOCK * N_W_BLOCKS == IN_COLS and CAST_BLOCK % LANES == 0


def _sample_proj_kernel(x_ref, mod_ref, w_ref, b_ref,
                        wbf_ref, z_ref, qt0_ref, qt1_ref, qt2_ref, kv0_ref, kv1_ref, kv2_ref):
    n = pl.program_id(0)
    shift = mod_ref[:, 0:D_MODEL]
    scale = mod_ref[:, D_MODEL:2 * D_MODEL]
    hb = (x_ref[...] * (1.0 + scale) + shift).astype(BF16)
    wb = w_ref[...].astype(BF16)
    wbf_ref[...] = wb
    col0 = pl.multiple_of(n * CAST_BLOCK, LANES)
    z_ref[:, pl.ds(col0, CAST_BLOCK)] = (jnp.dot(hb, wb, preferred_element_type=F32)
                                         + b_ref[:, pl.ds(col0, CAST_BLOCK)])

    @pl.when(n == N_W_BLOCKS - 1)
    def _():
        zeros = jnp.zeros((DEC_BATCH, GROUP_COLS), F32)
        blk = lambda i: z_ref[:, i * GROUP_COLS:(i + 1) * GROUP_COLS]
        for g, (qt_ref, kv_ref) in enumerate(((qt0_ref, kv0_ref), (qt1_ref, kv1_ref), (qt2_ref, kv2_ref))):
            _store_kv_tiles(qt_ref, blk(g) * Q_SCALE, zeros)
            _store_kv_tiles(kv_ref, blk(N_GROUPS + g), blk(2 * N_GROUPS + g))


def _sample_proj_call(x_s, mod_s, w_in, b_in):
    full = lambda shape: pl.BlockSpec(shape, lambda n: (0,) * len(shape))
    tile_rows = DEC_BATCH * KV_TILE_ROWS
    tiles = jax.ShapeDtypeStruct((tile_rows, HEAD_DIM), F32)
    w_blk = pl.BlockSpec((D_MODEL, CAST_BLOCK), lambda n: (0, n))
    return pl.pallas_call(
        _sample_proj_kernel,
        grid=(N_W_BLOCKS,),
        in_specs=[full((DEC_BATCH, D_MODEL)), full((DEC_BATCH, 3 * D_MODEL)), w_blk, _row_vec(IN_COLS)],
        out_specs=[w_blk, full((DEC_BATCH, IN_COLS))] + [full((tile_rows, HEAD_DIM))] * 6,
        out_shape=[jax.ShapeDtypeStruct((D_MODEL, IN_COLS), BF16), jax.ShapeDtypeStruct((DEC_BATCH, IN_COLS), F32)]
                  + [tiles] * 6,
        compiler_params=_params(("arbitrary",)),
        name="sample_proj",
    )(x_s, mod_s, w_in, b_in)


def _sample_attn_kernel(qt0_ref, qt1_ref, qt2_ref, kv0_ref, kv1_ref, kv2_ref, c0_ref, c1_ref, c2_ref, o_ref,
                        bias_ref):
    qt_refs = (qt0_ref, qt1_ref, qt2_ref)
    kv_refs = (kv0_ref, kv1_ref, kv2_ref)
    c_refs = (c0_ref, c1_ref, c2_ref)
    half = HEADS_PER_GROUP
    tile = (KV_TILE_ROWS, HEAD_DIM)
    keys_tile = (KEYS_PER_BLOCK,) + tile

    @pl.when(pl.program_id(0) == 0)
    def _():
        key = lax.broadcasted_iota(jnp.int32, keys_tile, 0)
        sub = lax.broadcasted_iota(jnp.int32, keys_tile, 1)
        steps_back = (KEYS_PER_BLOCK - key).astype(F32)
        for g in range(N_GROUPS):
            slope_rows = jnp.zeros(keys_tile, F32)
            for hh in range(HEADS_PER_GROUP):
                slope_rows = jnp.where(sub == half + hh, _slope(g, hh), slope_rows)
            bias_ref[g] = -(slope_rows * (GROUPS[g][1] * steps_back)) * LOG2_E

    def body(b, carry):
        outs, lses = [], []
        for g in range(N_GROUPS):
            x = c_refs[g][b]
            qt = qt_refs[g][b] * LOG2_E
            kvn = kv_refs[g][b]
            s = jnp.broadcast_to(jnp.sum(x * qt[None], axis=-1, keepdims=True), keys_tile)
            s = pltpu.roll(s, half, axis=1) + bias_ref[g]
            s_self = jnp.broadcast_to(jnp.sum(qt * kvn, axis=-1, keepdims=True), tile)
            s_self = pltpu.roll(s_self, half, axis=0)
            m = jnp.maximum(jnp.max(s, axis=0), s_self)
            p = jnp.exp2(s - m[None])
            p_self = jnp.exp2(s_self - m)
            l = jnp.sum(p, axis=0) + p_self
            acc = jnp.sum(x * p, axis=0) + p_self * kvn
            outs.append(acc / l)
            lses.append(m + jnp.log2(l))
        mx = jnp.maximum(jnp.maximum(lses[0], lses[1]), lses[2])
        es = [jnp.exp2(ls - mx) for ls in lses]
        o_ref[b] = (es[0] * outs[0] + es[1] * outs[1] + es[2] * outs[2]) / (es[0] + es[1] + es[2])
        return carry

    lax.fori_loop(0, SAMPLE_BLOCK, body, 0, unroll=SAMPLE_UNROLL)


def _sample_attn_call(qts, kvs, caches):
    views = [c.reshape(DEC_BATCH, KEYS_PER_BLOCK, GROUPS[g][1], KV_TILE_ROWS, HEAD_DIM) for g, c in enumerate(caches)]
    tile3 = lambda a: a.reshape(DEC_BATCH, KV_TILE_ROWS, HEAD_DIM)
    tile_blk = pl.BlockSpec((SAMPLE_BLOCK, KV_TILE_ROWS, HEAD_DIM), lambda i: (i, 0, 0))
    cache_blk = pl.BlockSpec((SAMPLE_BLOCK, KEYS_PER_BLOCK, None, KV_TILE_ROWS, HEAD_DIM), lambda i: (i, 0, 0, 0, 0))
    return pl.pallas_call(
        _sample_attn_kernel,
        grid=(DEC_BATCH // SAMPLE_BLOCK,),
        in_specs=[tile_blk] * 6 + [cache_blk] * 3,
        out_specs=tile_blk,
        out_shape=jax.ShapeDtypeStruct((DEC_BATCH, KV_TILE_ROWS, HEAD_DIM), F32),
        scratch_shapes=[pltpu.VMEM((N_GROUPS, KEYS_PER_BLOCK, KV_TILE_ROWS, HEAD_DIM), F32)],
        compiler_params=_params(("arbitrary",)),
        name="sample_attn",
    )(*[tile3(a) for a in qts], *[tile3(a) for a in kvs], *views)


def _sample_tail_kernel(x_ref, mod_ref, o_ref, zr_ref, st_ref, cw_ref, cb_ref, cg_ref, cbeta_ref,
                        wpa_ref, wpb_ref, wo_ref, lng_ref, lnb_ref, y_ref, convs_ref):
    gate = mod_ref[:, 2 * D_MODEL:3 * D_MODEL]
    z = lambda col0, width: zr_ref[:, col0:col0 + width]
    u = z(COL_GLU_A, CONV_CH) * _sigmoid(z(COL_GLU_G, CONV_CH))
    hist = CONV_WIDTH - 1
    acc = cb_ref[...] + u * cw_ref[hist:hist + 1, :]
    for j in range(hist):
        acc = acc + st_ref[j] * cw_ref[j:j + 1, :]
    conv_out = _silu(_layer_norm(acc, cg_ref[...], cbeta_ref[...]))
    for j in range(hist - 1):
        convs_ref[j] = st_ref[j + 1]
    convs_ref[hist - 1] = u
    o_attn = jnp.concatenate(
        [o_ref[pl.ds(HEADS_PER_GROUP + hh, DEC_BATCH, stride=KV_TILE_ROWS), :] for hh in range(HEADS_PER_GROUP)],
        axis=-1)
    a = jnp.dot((o_attn * _silu(z(COL_GA, GROUP_COLS))).astype(BF16), wpa_ref[...], preferred_element_type=F32)
    b = jnp.dot((conv_out * _silu(z(COL_GB, CONV_CH))).astype(BF16), wpb_ref[...], preferred_element_type=F32)
    pre = (_sigmoid(z(COL_MA, D_MODEL)) * a + _sigmoid(z(COL_MB, D_MODEL)) * b).astype(BF16)
    y = jnp.dot(pre, wo_ref[...], preferred_element_type=F32)
    y_ref[...] = _layer_norm(ALPHA * x_ref[...] + gate * y, lng_ref[...], lnb_ref[...])


def _sample_tail_call(x_s, mod_s, o_s, zr, state_t, conv_w, conv_b, cn_g, cn_b, w_pa, w_pb, w_o, ln_g, ln_b):
    full = lambda shape: pl.BlockSpec(shape, lambda i: (0,) * len(shape))
    args = (x_s, mod_s, o_s, zr, state_t, conv_w, conv_b, cn_g, cn_b, w_pa, w_pb, w_o, ln_g, ln_b)
    return pl.pallas_call(
        _sample_tail_kernel,
        grid=(1,),
        in_specs=[full(a.shape) for a in args],
        out_specs=[full((DEC_BATCH, D_MODEL)), full(state_t.shape)],
        out_shape=[jax.ShapeDtypeStruct((DEC_BATCH, D_MODEL), F32),
                   jax.ShapeDtypeStruct(state_t.shape, F32)],
        compiler_params=_params(("arbitrary",)),
        name="sample_tail",
    )(*args)


def kernel(x_prompt, x_sample, c_prompt, c_sample, cache_kv_w128, cache_kv_w512, cache_kv_w2048, state_conv,
           w_c, b_c, w_in, b_in, conv_w, conv_b, conv_norm_g, conv_norm_b, w_pa, w_pb, w_o, ln_g, ln_b):
    assert x_prompt.shape == (1, SEQ, D_MODEL) and x_sample.shape == (DEC_BATCH, 1, D_MODEL)
    assert w_in.shape == (D_MODEL, IN_COLS)
    caches = (cache_kv_w128, cache_kv_w512, cache_kv_w2048)
    for (window, _), c in zip(GROUPS, caches):
        assert c.shape == (DEC_BATCH, window, 2, HEADS_PER_GROUP, HEAD_DIM)

    w_pa_b, w_pb_b, w_o_b = w_pa.astype(BF16), w_pb.astype(BF16), w_o.astype(BF16)
    vec = lambda a: a.reshape(1, -1)
    b_row = vec(b_in)
    conv_args = (conv_w, vec(conv_b), vec(conv_norm_g), vec(conv_norm_b))
    out_args = (w_pa_b, w_pb_b, w_o_b, vec(ln_g), vec(ln_b))

    mod_p, mod_s = _mod_call(c_prompt, c_sample, w_c, b_c)
    x_s = x_sample.reshape(DEC_BATCH, D_MODEL)
    w_bf, z_s, qt0, qt1, qt2, kvs0, kvs1, kvs2 = _sample_proj_call(x_s, mod_s, w_in, b_row)

    x_p = x_prompt.reshape(SEQ, D_MODEL)
    qkv, kv0, kv1, kv2, b_conv, conv_p = _front_call(x_p, mod_p, w_bf, b_row, *conv_args)
    attn = [_attn_call(g, qkv) for g in range(N_GROUPS)]
    y_p = _tail_call(x_p, mod_p, [a[0] for a in attn], [a[1] for a in attn], b_conv, w_bf, b_row, *out_args)

    o_s = _sample_attn_call((qt0, qt1, qt2), (kvs0, kvs1, kvs2), caches)
    o_s = o_s.reshape(DEC_BATCH * KV_TILE_ROWS, HEAD_DIM)
    state_t = jnp.transpose(state_conv, (1, 0, 2))
    y_s, conv_s_t = _sample_tail_call(x_s, mod_s, o_s, z_s, state_t, *conv_args, *out_args)

    kv_shape_p = lambda keep: (1, keep, 2, HEADS_PER_GROUP, HEAD_DIM)
    kv_shape_s = (DEC_BATCH, 1, 2, HEADS_PER_GROUP, HEAD_DIM)
    return (y_p.reshape(1, SEQ, D_MODEL),
            y_s.reshape(DEC_BATCH, 1, D_MODEL),
            kv0.reshape(kv_shape_p(GROUPS[0][0])),
            kv1.reshape(kv_shape_p(GROUPS[1][0])),
            kv2.reshape(kv_shape_p(GROUPS[2][0])),
            conv_p.reshape(1, CONV_WIDTH - 1, CONV_CH),
            kvs0.reshape(kv_shape_s), kvs1.reshape(kv_shape_s), kvs2.reshape(kv_shape_s),
            jnp.transpose(conv_s_t, (1, 0, 2)))
```

```python
import functools

import jax
import jax.numpy as jnp
from jax import lax
from jax.experimental import pallas as pl
from jax.experimental.pallas import tpu as pltpu

F32 = jnp.float32
BF16 = jnp.bfloat16

D_MODEL = 1024
SEQ = 16384
DEC_BATCH = 128
DEPTH = 1
HEAD_DIM = 128
HEADS_PER_GROUP = 4
GROUPS = ((128, 1), (512, 4), (2048, 16))
N_GROUPS = len(GROUPS)
N_HEADS = N_GROUPS * HEADS_PER_GROUP
GROUP_COLS = HEADS_PER_GROUP * HEAD_DIM
ATTN_QKV = N_HEADS * HEAD_DIM
QKV_COLS = 3 * ATTN_QKV
CONV_CH = D_MODEL // 2
CONV_WIDTH = 31
COL_GA = QKV_COLS
COL_GLU_A = COL_GA + GROUP_COLS
COL_GLU_G = COL_GLU_A + CONV_CH
COL_GB = COL_GLU_G + CONV_CH
COL_MA = COL_GB + CONV_CH
COL_MB = COL_MA + D_MODEL
IN_COLS = COL_MB + D_MODEL
ALPHA = (2.0 * DEPTH) ** 0.25
LN_EPS = 1e-5
NEG = -1e30
Q_SCALE = HEAD_DIM ** -0.5
LOG2_E = 1.4426950408889634
Q_SCALE_LOG2 = Q_SCALE * LOG2_E
KEYS_PER_BLOCK = 128
KV_TILE_ROWS = 2 * HEADS_PER_GROUP

LANES = 128
MXU_COLS = 256
W_BLOCK = 512
ROW_TILE = 512
ATTN_STEP_ROWS = 2048
BLOCKS_PER_ATTN_STEP = ATTN_STEP_ROWS // KEYS_PER_BLOCK
CONV_ROW_CHUNK = 32
HIST_ROWS = 32
SAMPLE_BLOCK = 8
SAMPLE_UNROLL = 4
VMEM_LIMIT_BYTES = 56 * 1024 * 1024


def _slope(group, head):
    return 2.0 ** (-8.0 * (group * HEADS_PER_GROUP + head + 1) / N_HEADS)


def _sigmoid(x):
    return 1.0 / (1.0 + jnp.exp(-x))


def _silu(x):
    return x * _sigmoid(x)


def _layer_norm(x, g, b):
    mu = jnp.mean(x, axis=-1, keepdims=True)
    xc = x - mu
    var = jnp.mean(xc * xc, axis=-1, keepdims=True)
    return xc * lax.rsqrt(var + LN_EPS) * g + b


def _col_blocks(width):
    return [slice(b * MXU_COLS, (b + 1) * MXU_COLS) for b in range(width // MXU_COLS)]


def _row_stats(ref, width):
    total = sum(jnp.sum(ref[:, cols], axis=-1, keepdims=True) for cols in _col_blocks(width))
    mu = total / width
    sq = sum(jnp.sum(jnp.square(ref[:, cols] - mu), axis=-1, keepdims=True) for cols in _col_blocks(width))
    return mu, lax.rsqrt(sq / width + LN_EPS)


def _resident(shape):
    return pl.BlockSpec(shape, lambda *_: (0,) * len(shape), pipeline_mode=pl.Buffered(1))


def _w_cols(col0, width):
    assert col0 % width == 0
    return pl.BlockSpec((D_MODEL, width), lambda *_: (0, col0 // width), pipeline_mode=pl.Buffered(1))


def _row_vec(n):
    return pl.BlockSpec((1, n), lambda *_: (0, 0))


def _params(semantics):
    return pltpu.CompilerParams(dimension_semantics=semantics, vmem_limit_bytes=VMEM_LIMIT_BYTES)


def _store_kv_tiles(ref, k, v):
    rows = k.shape[0]
    for hh in range(HEADS_PER_GROUP):
        cols = slice(hh * HEAD_DIM, (hh + 1) * HEAD_DIM)
        ref[pl.ds(hh, rows, stride=KV_TILE_ROWS), :] = k[:, cols]
        ref[pl.ds(HEADS_PER_GROUP + hh, rows, stride=KV_TILE_ROWS), :] = v[:, cols]


def _mod_kernel(cp_ref, cs_ref, w_ref, b_ref, mp_ref, ms_ref):
    w = w_ref[...].astype(BF16)
    cp = jnp.broadcast_to(cp_ref[...], (8, D_MODEL)).astype(BF16)
    mp_ref[...] = jnp.dot(cp, w, preferred_element_type=F32)[0:1] + b_ref[...]
    ms_ref[...] = jnp.dot(cs_ref[...].astype(BF16), w, preferred_element_type=F32) + b_ref[...]


def _mod_call(c_prompt, c_sample, w_c, b_c):
    n_blk = 3
    return pl.pallas_call(
        _mod_kernel,
        grid=(n_blk,),
        in_specs=[
            pl.BlockSpec((1, D_MODEL), lambda n: (0, 0)),
            pl.BlockSpec((DEC_BATCH, D_MODEL), lambda n: (0, 0)),
            pl.BlockSpec((D_MODEL, D_MODEL), lambda n: (0, n)),
            pl.BlockSpec((1, D_MODEL), lambda n: (0, n)),
        ],
        out_specs=[
            pl.BlockSpec((1, D_MODEL), lambda n: (0, n)),
            pl.BlockSpec((DEC_BATCH, D_MODEL), lambda n: (0, n)),
        ],
        out_shape=[
            jax.ShapeDtypeStruct((1, 3 * D_MODEL), F32),
            jax.ShapeDtypeStruct((DEC_BATCH, 3 * D_MODEL), F32),
        ],
        compiler_params=_params(("arbitrary",)),
        name="adaln_mod",
    )(c_prompt, c_sample, w_c, b_c.reshape(1, 3 * D_MODEL))


N_ROW_TILES = SEQ // ROW_TILE
KV2_TILES = GROUPS[2][0] // ROW_TILE


N_QKV_BLOCKS = QKV_COLS // GROUP_COLS
CONV_CHUNKS = (CONV_CH // LANES) * (ROW_TILE // CONV_ROW_CHUNK)
CONV_CHUNKS_PER_TRIP = 8
BLOCKS_PER_TRIP = 8
assert CONV_CHUNKS == CONV_CHUNKS_PER_TRIP * (N_QKV_BLOCKS - 1) and (N_QKV_BLOCKS - 1) % BLOCKS_PER_TRIP == 0


def _front_kernel(x_ref, mod_ref, w_ref, wa_ref, wg_ref, wgb_ref, b_ref, cw_ref, cb_ref, cg_ref, cbeta_ref,
                  qkv_ref, kv0_ref, kv1_ref, kv2_ref, bin_ref, convp_ref,
                  hs_ref, hq_ref, lhs_ref, uext_ref, cws_ref, cacc_ref, tail_ref):
    i = pl.program_id(0)
    shift = mod_ref[:, 0:D_MODEL]
    scale = mod_ref[:, D_MODEL:2 * D_MODEL]
    h = x_ref[...] * (1.0 + scale) + shift
    lhs_ref[0] = h.astype(BF16)

    n_slab = D_MODEL // LANES
    d1, d2 = GROUPS[1][1], GROUPS[2][1]
    assert d2 == d1 * d1
    r1, r2 = ROW_TILE // d1, ROW_TILE // d2
    for c in range(n_slab):
        cols = slice(c * LANES, (c + 1) * LANES)
        hs_ref[c] = h[:, cols]
        for b in range(d1):
            rows_b = hs_ref[c, pl.ds(b, r1, stride=d1), :]
            hq_ref[c, b * r1:(b + 1) * r1, :] = rows_b
            lhs_ref[1, b * r1:(b + 1) * r1, cols] = rows_b.astype(BF16)
        for r in range(d2):
            a, b = r // d1, r % d1
            lhs_ref[2, r * r2:(r + 1) * r2, cols] = hq_ref[c, pl.ds(b * r1 + a, r2, stride=d1), :].astype(BF16)

    def gate_proj(wx_ref, col0, cols):
        bcols = slice(col0 + cols.start, col0 + cols.stop)
        return jnp.dot(lhs_ref[0], wx_ref[:, cols], preferred_element_type=F32) + b_ref[:, bcols]

    n_cslab = CONV_CH // LANES

    @pl.when(i == 0)
    def _():
        uext_ref[:, 0:HIST_ROWS, :] = jnp.zeros((n_cslab, HIST_ROWS, LANES), F32)
        for c in range(n_cslab):
            cws_ref[c, 0:CONV_WIDTH, :] = cw_ref[:, c * LANES:(c + 1) * LANES]
            cws_ref[c, CONV_WIDTH:CONV_WIDTH + 1, :] = cb_ref[:, c * LANES:(c + 1) * LANES]

    slabs_per_blk = MXU_COLS // LANES
    for b, cols in enumerate(_col_blocks(CONV_CH)):
        u = gate_proj(wa_ref, COL_GLU_A, cols) * _sigmoid(gate_proj(wg_ref, COL_GLU_G, cols))
        convp_ref[:, cols] = u[ROW_TILE - (CONV_WIDTH - 1):, :]
        for s in range(slabs_per_blk):
            uext_ref[b * slabs_per_blk + s, HIST_ROWS:, :] = u[:, s * LANES:(s + 1) * LANES]

    lead = HIST_ROWS - (CONV_WIDTH - 1)
    chunks_per_slab = ROW_TILE // CONV_ROW_CHUNK
    chunk_shift = chunks_per_slab.bit_length() - 1
    assert chunks_per_slab == 1 << chunk_shift

    def conv_chunk(idx):
        c = lax.shift_right_logical(idx, chunk_shift)
        r0 = pl.multiple_of(jnp.bitwise_and(idx, chunks_per_slab - 1) * CONV_ROW_CHUNK, CONV_ROW_CHUNK)
        acc = jnp.broadcast_to(cws_ref[c, CONV_WIDTH:CONV_WIDTH + 1, :], (CONV_ROW_CHUNK, LANES))
        for j in range(CONV_WIDTH):
            acc = acc + uext_ref[c, pl.ds(r0 + lead + j, CONV_ROW_CHUNK), :] * cws_ref[c, j:j + 1, :]

        def store():
            cacc_ref[c, pl.ds(r0, CONV_ROW_CHUNK), :] = acc
        return store

    def qkv_block(n):
        col0 = pl.multiple_of(n * GROUP_COLS, GROUP_COLS)
        val = jnp.dot(lhs_ref[n % N_GROUPS], w_ref[:, pl.ds(col0, GROUP_COLS)], preferred_element_type=F32)
        val = val + b_ref[:, pl.ds(col0, GROUP_COLS)]

        def store():
            tail_ref[n] = val[ROW_TILE - GROUPS[0][0]:, :]
            qkv_ref[n] = (val * jnp.where(n < N_GROUPS, Q_SCALE_LOG2, 1.0)).astype(BF16)
        return store

    def trip(n, carry):
        stores = [conv_chunk(n * BLOCKS_PER_TRIP * CONV_CHUNKS_PER_TRIP + k)
                  for k in range(BLOCKS_PER_TRIP * CONV_CHUNKS_PER_TRIP)]
        stores += [qkv_block(n * BLOCKS_PER_TRIP + k) for k in range(BLOCKS_PER_TRIP)]
        for st in stores:
            st()
        return carry

    lax.fori_loop(0, jnp.minimum(i + 1, (N_QKV_BLOCKS - 1) // BLOCKS_PER_TRIP), trip, 0)
    qkv_block(N_QKV_BLOCKS - 1)()

    for c in range(n_cslab):
        uext_ref[c, 0:HIST_ROWS, :] = uext_ref[c, ROW_TILE:ROW_TILE + HIST_ROWS, :]
    _store_kv_tiles(kv0_ref, tail_ref[N_GROUPS], tail_ref[2 * N_GROUPS])

    total = sum(jnp.sum(cacc_ref[c], axis=-1, keepdims=True) for c in range(n_cslab))
    mu = total / CONV_CH
    sq = sum(jnp.sum(jnp.square(cacc_ref[c] - mu), axis=-1, keepdims=True) for c in range(n_cslab))
    rstd = lax.rsqrt(sq / CONV_CH + LN_EPS)
    for b, cols in enumerate(_col_blocks(CONV_CH)):
        conv = jnp.concatenate([cacc_ref[b * slabs_per_blk + s] for s in range(slabs_per_blk)], axis=-1)
        normed = (conv - mu) * rstd * cg_ref[:, cols] + cbeta_ref[:, cols]
        bin_ref[:, cols] = (_silu(normed) * _silu(gate_proj(wgb_ref, COL_GB, cols))).astype(BF16)

    def proj_nat(blk):
        cols = slice(blk * GROUP_COLS, (blk + 1) * GROUP_COLS)
        return jnp.dot(lhs_ref[0], w_ref[:, cols], preferred_element_type=F32) + b_ref[:, cols]

    @pl.when(i == N_ROW_TILES - 1)
    def _():
        _store_kv_tiles(kv1_ref, proj_nat(N_GROUPS + 1), proj_nat(2 * N_GROUPS + 1))

    @pl.when(i >= N_ROW_TILES - KV2_TILES)
    def _():
        _store_kv_tiles(kv2_ref, proj_nat(N_GROUPS + 2), proj_nat(2 * N_GROUPS + 2))


def _front_call(x, mod_p, w_bf, b_in, conv_w, conv_b, cn_g, cn_b):
    nat = pl.BlockSpec((ROW_TILE, GROUP_COLS), lambda i: (i, 0))
    first_kv2_tile = N_ROW_TILES - KV2_TILES
    kv_shape = lambda g: jax.ShapeDtypeStruct((GROUPS[g][0] * KV_TILE_ROWS, HEAD_DIM), F32)
    return pl.pallas_call(
        _front_kernel,
        grid=(N_ROW_TILES,),
        in_specs=[
            pl.BlockSpec((ROW_TILE, D_MODEL), lambda i: (i, 0)),
            _row_vec(3 * D_MODEL),
            _w_cols(0, QKV_COLS), _w_cols(COL_GLU_A, W_BLOCK), _w_cols(COL_GLU_G, W_BLOCK), _w_cols(COL_GB, W_BLOCK),
            _row_vec(IN_COLS),
            pl.BlockSpec((CONV_WIDTH, CONV_CH), lambda i: (0, 0)),
            _row_vec(CONV_CH), _row_vec(CONV_CH), _row_vec(CONV_CH),
        ],
        out_specs=[pl.BlockSpec((N_QKV_BLOCKS, ROW_TILE, GROUP_COLS), lambda i: (0, i, 0)),
                   pl.BlockSpec((GROUPS[0][0] * KV_TILE_ROWS, HEAD_DIM), lambda i: (0, 0)),
                   pl.BlockSpec((GROUPS[1][0] * KV_TILE_ROWS, HEAD_DIM), lambda i: (0, 0)),
                   pl.BlockSpec((ROW_TILE * KV_TILE_ROWS, HEAD_DIM),
                                lambda i: (jnp.maximum(i - first_kv2_tile, 0), 0)),
                   nat,
                   pl.BlockSpec((CONV_WIDTH - 1, CONV_CH), lambda i: (0, 0))],
        out_shape=[jax.ShapeDtypeStruct((N_QKV_BLOCKS, SEQ, GROUP_COLS), BF16),
                   kv_shape(0), kv_shape(1), kv_shape(2),
                   jax.ShapeDtypeStruct((SEQ, GROUP_COLS), BF16),
                   jax.ShapeDtypeStruct((CONV_WIDTH - 1, CONV_CH), F32)],
        scratch_shapes=[
            pltpu.VMEM((D_MODEL // LANES, ROW_TILE, LANES), F32),
            pltpu.VMEM((D_MODEL // LANES, ROW_TILE, LANES), F32),
            pltpu.VMEM((N_GROUPS, ROW_TILE, D_MODEL), BF16),
            pltpu.VMEM((CONV_CH // LANES, HIST_ROWS + ROW_TILE, LANES), F32),
            pltpu.VMEM((CONV_CH // LANES, CONV_WIDTH + 1, LANES), F32),
            pltpu.VMEM((CONV_CH // LANES, ROW_TILE, LANES), F32),
            pltpu.VMEM((N_QKV_BLOCKS, GROUPS[0][0], GROUP_COLS), F32),
        ],
        compiler_params=_params(("arbitrary",)),
        name="prompt_front",
    )(x, mod_p, w_bf, w_bf, w_bf, w_bf, b_in, conv_w, conv_b, cn_g, cn_b)


def _attn_kernel(q_ref, k_ref, v_ref, kp_ref, vp_ref, o_ref, lse_ref, bias_ref, oscr_ref, lscr_ref):
    g = pl.program_id(0)
    for group in range(N_GROUPS):
        pl.when(g == group)(functools.partial(_attn_group, q_ref, k_ref, v_ref, kp_ref, vp_ref, o_ref, lse_ref,
                                              bias_ref, oscr_ref, lscr_ref, group=group))


def _attn_group(q_ref, k_ref, v_ref, kp_ref, vp_ref, o_ref, lse_ref, bias_ref, oscr_ref, lscr_ref, *, group):
    dil = GROUPS[group][1]
    tail = KEYS_PER_BLOCK * dil
    t = pl.program_id(1)

    @pl.when(t == 0)
    def _():
        row = lax.broadcasted_iota(jnp.int32, (KEYS_PER_BLOCK, 2 * KEYS_PER_BLOCK), 0)
        col = lax.broadcasted_iota(jnp.int32, (KEYS_PER_BLOCK, 2 * KEYS_PER_BLOCK), 1)
        dist = row - col + KEYS_PER_BLOCK
        valid = (dist >= 0) & (dist <= KEYS_PER_BLOCK)
        valid_first = valid & (col >= KEYS_PER_BLOCK)
        distf = (dil * dist).astype(F32)
        for hh in range(HEADS_PER_GROUP):
            b = -(_slope(group, hh) * distf) * LOG2_E
            bias_ref[0, hh] = jnp.where(valid, b, NEG)
            bias_ref[1, hh] = jnp.where(valid_first, b, NEG)

    first_step = (t == 0).astype(jnp.int32)
    head_lane = lax.broadcasted_iota(jnp.int32, (KEYS_PER_BLOCK, LANES), 1)
    def block(ref, j, prev=False):
        if tail < ATTN_STEP_ROWS:
            base = ATTN_STEP_ROWS - tail if prev else 0
            return ref[base + j * KEYS_PER_BLOCK:base + (j + 1) * KEYS_PER_BLOCK, :]
        piece = ROW_TILE // dil
        return jnp.concatenate([ref[tt * ROW_TILE + j * piece:tt * ROW_TILE + (j + 1) * piece, :]
                                for tt in range(ATTN_STEP_ROWS // ROW_TILE)], axis=0)

    for j in range(BLOCKS_PER_ATTN_STEP):
        qb = block(q_ref, j)
        kself, vself = block(k_ref, j), block(v_ref, j)
        if j < dil:
            kprev, vprev = block(kp_ref, j, prev=True), block(vp_ref, j, prev=True)
        else:
            kprev, vprev = block(k_ref, j - dil), block(v_ref, j - dil)
        first = first_step if j < dil else 0
        start = (j // dil) * tail + (j % dil)
        rows = pl.ds(start, KEYS_PER_BLOCK, stride=dil) if dil > 1 else pl.ds(start, KEYS_PER_BLOCK)
        m_tile = jnp.zeros((KEYS_PER_BLOCK, LANES), F32)
        l_tile = jnp.ones((KEYS_PER_BLOCK, LANES), F32)
        for hh in range(HEADS_PER_GROUP):
            cols = slice(hh * HEAD_DIM, (hh + 1) * HEAD_DIM)
            kc = jnp.concatenate([kprev[:, cols], kself[:, cols]], axis=0)
            vc = jnp.concatenate([vprev[:, cols], vself[:, cols]], axis=0)
            s = lax.dot_general(qb[:, cols], kc, (((1,), (1,)), ((), ())), preferred_element_type=F32)
            s = s + bias_ref[first, hh]
            m = jnp.max(s, axis=-1, keepdims=True)
            p = jnp.exp2(s - m)
            l = jnp.sum(p, axis=-1, keepdims=True)
            acc = jnp.dot(p.astype(BF16), vc, preferred_element_type=F32)
            if dil > 1:
                oscr_ref[hh, rows, :] = acc
            else:
                o_ref[rows, cols] = acc.astype(BF16)
            m_tile = jnp.where(head_lane == hh, m, m_tile)
            l_tile = jnp.where(head_lane == hh, l, l_tile)
        if dil > 1:
            lscr_ref[0, rows, :] = m_tile
            lscr_ref[1, rows, :] = l_tile
        else:
            lse_ref[rows, 0:LANES] = m_tile
            lse_ref[rows, LANES:] = l_tile

    if dil > 1:
        for hh in range(HEADS_PER_GROUP):
            o_ref[:, hh * HEAD_DIM:(hh + 1) * HEAD_DIM] = oscr_ref[hh].astype(BF16)
        lse_ref[:, 0:LANES] = lscr_ref[0]
        lse_ref[:, LANES:] = lscr_ref[1]


def _attn_call(qkv):
    steps = SEQ // ATTN_STEP_ROWS
    cur_of = lambda first: pl.BlockSpec((None, ATTN_STEP_ROWS, GROUP_COLS), lambda g, t: (first + g, t, 0))
    prev_of = lambda first: pl.BlockSpec((None, ATTN_STEP_ROWS, GROUP_COLS),
                                         lambda g, t: (first + g, jnp.maximum(t - 1, 0), 0))
    return pl.pallas_call(
        _attn_kernel,
        grid=(N_GROUPS, steps),
        in_specs=[cur_of(0), cur_of(N_GROUPS), cur_of(2 * N_GROUPS), prev_of(N_GROUPS), prev_of(2 * N_GROUPS)],
        out_specs=[cur_of(0),
                   pl.BlockSpec((None, ATTN_STEP_ROWS, 2 * LANES), lambda g, t: (g, t, 0))],
        out_shape=[jax.ShapeDtypeStruct((N_GROUPS, SEQ, GROUP_COLS), BF16),
                   jax.ShapeDtypeStruct((N_GROUPS, SEQ, 2 * LANES), F32)],
        scratch_shapes=[
            pltpu.VMEM((2, HEADS_PER_GROUP, KEYS_PER_BLOCK, 2 * KEYS_PER_BLOCK), F32),
            pltpu.VMEM((HEADS_PER_GROUP, ATTN_STEP_ROWS, HEAD_DIM), F32),
            pltpu.VMEM((2, ATTN_STEP_ROWS, LANES), F32),
        ],
        compiler_params=_params(("arbitrary", "arbitrary")),
        name="prompt_attn",
    )(qkv, qkv, qkv, qkv, qkv)


def _tail_kernel(x_ref, mod_ref, o0_ref, o1_ref, o2_ref, l0_ref, l1_ref, l2_ref, bin_ref,
                 wga_ref, wma0_ref, wma1_ref, wmb0_ref, wmb1_ref, b_ref, wpa_ref, wpb_ref, wo_ref, lng_ref, lnb_ref,
                 y_ref,
                 hb_ref, ain_ref, pre_ref, res_ref):
    shift = mod_ref[:, 0:D_MODEL]
    scale = mod_ref[:, D_MODEL:2 * D_MODEL]
    hb_ref[...] = (x_ref[...] * (1.0 + scale) + shift).astype(BF16)

    def gate_proj(wx_refs, col0, cols):
        wx_ref = wx_refs[cols.start // W_BLOCK]
        wcols = slice(cols.start % W_BLOCK, cols.start % W_BLOCK + MXU_COLS)
        bcols = slice(col0 + cols.start, col0 + cols.stop)
        return jnp.dot(hb_ref[...], wx_ref[:, wcols], preferred_element_type=F32) + b_ref[:, bcols]

    stats = (l0_ref, l1_ref, l2_ref)
    maxes = [st[:, 0:LANES] for st in stats]
    m = jnp.maximum(jnp.maximum(maxes[0], maxes[1]), maxes[2])
    es = [jnp.exp2(mg - m) for mg in maxes]
    inv = 1.0 / sum(e * st[:, LANES:] for e, st in zip(es, stats))
    weights = tuple(e * inv for e in es)
    o_refs = (o0_ref, o1_ref, o2_ref)
    for cols in _col_blocks(GROUP_COLS):
        heads = []
        for hh in range(cols.start // HEAD_DIM, cols.stop // HEAD_DIM):
            hcols = slice(hh * HEAD_DIM, (hh + 1) * HEAD_DIM)
            heads.append(sum(w[:, hh:hh + 1] * o_ref[:, hcols].astype(F32) for w, o_ref in zip(weights, o_refs)))
        ga = gate_proj((wga_ref,), COL_GA, cols)
        ain_ref[:, cols] = (jnp.concatenate(heads, axis=-1) * _silu(ga)).astype(BF16)

    for cols in _col_blocks(D_MODEL):
        a = jnp.dot(ain_ref[...], wpa_ref[:, cols], preferred_element_type=F32)
        bb = jnp.dot(bin_ref[...], wpb_ref[:, cols], preferred_element_type=F32)
        ma = gate_proj((wma0_ref, wma1_ref), COL_MA, cols)
        mb = gate_proj((wmb0_ref, wmb1_ref), COL_MB, cols)
        pre_ref[:, cols] = (_sigmoid(ma) * a + _sigmoid(mb) * bb).astype(BF16)
    for cols in _col_blocks(D_MODEL):
        gate = mod_ref[:, 2 * D_MODEL + cols.start:2 * D_MODEL + cols.stop]
        y = jnp.dot(pre_ref[...], wo_ref[:, cols], preferred_element_type=F32)
        res_ref[:, cols] = ALPHA * x_ref[:, cols] + gate * y
    mu, rstd = _row_stats(res_ref, D_MODEL)
    for cols in _col_blocks(D_MODEL):
        y_ref[:, cols] = (res_ref[:, cols] - mu) * rstd * lng_ref[:, cols] + lnb_ref[:, cols]


def _tail_call(x, mod_p, outs, lses, b_conv, w_bf, b_in, w_pa, w_pb, w_o, ln_g, ln_b):
    o_blk = lambda g: pl.BlockSpec((None, ROW_TILE, GROUP_COLS), lambda i: (g, i, 0))
    lse_blk = lambda g: pl.BlockSpec((None, ROW_TILE, 2 * LANES), lambda i: (g, i, 0))
    return pl.pallas_call(
        _tail_kernel,
        grid=(N_ROW_TILES,),
        in_specs=[
            pl.BlockSpec((ROW_TILE, D_MODEL), lambda i: (i, 0)),
            _row_vec(3 * D_MODEL),
            o_blk(0), o_blk(1), o_blk(2), lse_blk(0), lse_blk(1), lse_blk(2),
            pl.BlockSpec((ROW_TILE, CONV_CH), lambda i: (i, 0)),
            _w_cols(COL_GA, W_BLOCK),
            _w_cols(COL_MA, W_BLOCK), _w_cols(COL_MA + W_BLOCK, W_BLOCK),
            _w_cols(COL_MB, W_BLOCK), _w_cols(COL_MB + W_BLOCK, W_BLOCK),
            _row_vec(IN_COLS),
            _resident((GROUP_COLS, D_MODEL)),
            _resident((CONV_CH, D_MODEL)),
            _resident((D_MODEL, D_MODEL)),
            _row_vec(D_MODEL), _row_vec(D_MODEL),
        ],
        out_specs=pl.BlockSpec((ROW_TILE, D_MODEL), lambda i: (i, 0)),
        out_shape=jax.ShapeDtypeStruct((SEQ, D_MODEL), F32),
        scratch_shapes=[
            pltpu.VMEM((ROW_TILE, D_MODEL), BF16),
            pltpu.VMEM((ROW_TILE, GROUP_COLS), BF16),
            pltpu.VMEM((ROW_TILE, D_MODEL), BF16),
            pltpu.VMEM((ROW_TILE, D_MODEL), F32),
        ],
        compiler_params=_params(("arbitrary",)),
        name="prompt_tail",
    )(x, mod_p, outs, outs, outs, lses, lses, lses, b_conv, w_bf, w_bf, w_bf, w_bf, w_bf, b_in, w_pa, w_pb, w_o, ln_g, ln_b)


N_W_BLOCKS = 4
CAST_BLOCK = IN_COLS // N_W_BLOCKS
assert CAST_BLOCK * N_W_BLOCKS == IN_COLS and CAST_BLOCK % LANES == 0


def _sample_proj_kernel(x_ref, mod_ref, w_ref, b_ref,
                        wbf_ref, z_ref, qt0_ref, qt1_ref, qt2_ref, kv0_ref, kv1_ref, kv2_ref):
    n = pl.program_id(0)
    shift = mod_ref[:, 0:D_MODEL]
    scale = mod_ref[:, D_MODEL:2 * D_MODEL]
    hb = (x_ref[...] * (1.0 + scale) + shift).astype(BF16)
    wb = w_ref[...].astype(BF16)
    wbf_ref[...] = wb
    col0 = pl.multiple_of(n * CAST_BLOCK, LANES)
    z_ref[:, pl.ds(col0, CAST_BLOCK)] = (jnp.dot(hb, wb, preferred_element_type=F32)
                                         + b_ref[:, pl.ds(col0, CAST_BLOCK)])

    @pl.when(n == N_W_BLOCKS - 1)
    def _():
        zeros = jnp.zeros((DEC_BATCH, GROUP_COLS), F32)
        blk = lambda i: z_ref[:, i * GROUP_COLS:(i + 1) * GROUP_COLS]
        for g, (qt_ref, kv_ref) in enumerate(((qt0_ref, kv0_ref), (qt1_ref, kv1_ref), (qt2_ref, kv2_ref))):
            _store_kv_tiles(qt_ref, blk(g) * Q_SCALE, zeros)
            _store_kv_tiles(kv_ref, blk(N_GROUPS + g), blk(2 * N_GROUPS + g))


def _sample_proj_call(x_s, mod_s, w_in, b_in):
    full = lambda shape: pl.BlockSpec(shape, lambda n: (0,) * len(shape))
    tile_rows = DEC_BATCH * KV_TILE_ROWS
    tiles = jax.ShapeDtypeStruct((tile_rows, HEAD_DIM), F32)
    w_blk = pl.BlockSpec((D_MODEL, CAST_BLOCK), lambda n: (0, n))
    return pl.pallas_call(
        _sample_proj_kernel,
        grid=(N_W_BLOCKS,),
        in_specs=[full((DEC_BATCH, D_MODEL)), full((DEC_BATCH, 3 * D_MODEL)), w_blk, _row_vec(IN_COLS)],
        out_specs=[w_blk, full((DEC_BATCH, IN_COLS))] + [full((tile_rows, HEAD_DIM))] * 6,
        out_shape=[jax.ShapeDtypeStruct((D_MODEL, IN_COLS), BF16), jax.ShapeDtypeStruct((DEC_BATCH, IN_COLS), F32)]
                  + [tiles] * 6,
        compiler_params=_params(("arbitrary",)),
        name="sample_proj",
    )(x_s, mod_s, w_in, b_in)


def _sample_attn_kernel(qt0_ref, qt1_ref, qt2_ref, kv0_ref, kv1_ref, kv2_ref, c0_ref, c1_ref, c2_ref, o_ref,
                        bias_ref):
    qt_refs = (qt0_ref, qt1_ref, qt2_ref)
    kv_refs = (kv0_ref, kv1_ref, kv2_ref)
    c_refs = (c0_ref, c1_ref, c2_ref)
    half = HEADS_PER_GROUP
    tile = (KV_TILE_ROWS, HEAD_DIM)
    keys_tile = (KEYS_PER_BLOCK,) + tile

    @pl.when(pl.program_id(0) == 0)
    def _():
        key = lax.broadcasted_iota(jnp.int32, keys_tile, 0)
        sub = lax.broadcasted_iota(jnp.int32, keys_tile, 1)
        steps_back = (KEYS_PER_BLOCK - key).astype(F32)
        for g in range(N_GROUPS):
            slope_rows = jnp.zeros(keys_tile, F32)
            for hh in range(HEADS_PER_GROUP):
                slope_rows = jnp.where(sub == half + hh, _slope(g, hh), slope_rows)
            bias_ref[g] = -(slope_rows * (GROUPS[g][1] * steps_back)) * LOG2_E

    def body(b, carry):
        outs, lses = [], []
        for g in range(N_GROUPS):
            x = c_refs[g][b]
            qt = qt_refs[g][b] * LOG2_E
            kvn = kv_refs[g][b]
            s = jnp.broadcast_to(jnp.sum(x * qt[None], axis=-1, keepdims=True), keys_tile)
            s = pltpu.roll(s, half, axis=1) + bias_ref[g]
            s_self = jnp.broadcast_to(jnp.sum(qt * kvn, axis=-1, keepdims=True), tile)
            s_self = pltpu.roll(s_self, half, axis=0)
            m = jnp.maximum(jnp.max(s, axis=0), s_self)
            p = jnp.exp2(s - m[None])
            p_self = jnp.exp2(s_self - m)
            l = jnp.sum(p, axis=0) + p_self
            acc = jnp.sum(x * p, axis=0) + p_self * kvn
            outs.append(acc / l)
            lses.append(m + jnp.log2(l))
        mx = jnp.maximum(jnp.maximum(lses[0], lses[1]), lses[2])
        es = [jnp.exp2(ls - mx) for ls in lses]
        o_ref[b] = (es[0] * outs[0] + es[1] * outs[1] + es[2] * outs[2]) / (es[0] + es[1] + es[2])
        return carry

    lax.fori_loop(0, SAMPLE_BLOCK, body, 0, unroll=SAMPLE_UNROLL)


def _sample_attn_call(qts, kvs, caches):
    views = [c.reshape(DEC_BATCH, KEYS_PER_BLOCK, GROUPS[g][1], KV_TILE_ROWS, HEAD_DIM) for g, c in enumerate(caches)]
    tile3 = lambda a: a.reshape(DEC_BATCH, KV_TILE_ROWS, HEAD_DIM)
    tile_blk = pl.BlockSpec((SAMPLE_BLOCK, KV_TILE_ROWS, HEAD_DIM), lambda i: (i, 0, 0))
    cache_blk = pl.BlockSpec((SAMPLE_BLOCK, KEYS_PER_BLOCK, None, KV_TILE_ROWS, HEAD_DIM), lambda i: (i, 0, 0, 0, 0))
    return pl.pallas_call(
        _sample_attn_kernel,
        grid=(DEC_BATCH // SAMPLE_BLOCK,),
        in_specs=[tile_blk] * 6 + [cache_blk] * 3,
        out_specs=tile_blk,
        out_shape=jax.ShapeDtypeStruct((DEC_BATCH, KV_TILE_ROWS, HEAD_DIM), F32),
        scratch_shapes=[pltpu.VMEM((N_GROUPS, KEYS_PER_BLOCK, KV_TILE_ROWS, HEAD_DIM), F32)],
        compiler_params=_params(("arbitrary",)),
        name="sample_attn",
    )(*[tile3(a) for a in qts], *[tile3(a) for a in kvs], *views)


def _sample_tail_kernel(x_ref, mod_ref, o_ref, zr_ref, st_ref, cw_ref, cb_ref, cg_ref, cbeta_ref,
                        wpa_ref, wpb_ref, wo_ref, lng_ref, lnb_ref, y_ref, convs_ref):
    gate = mod_ref[:, 2 * D_MODEL:3 * D_MODEL]
    z = lambda col0, width: zr_ref[:, col0:col0 + width]
    u = z(COL_GLU_A, CONV_CH) * _sigmoid(z(COL_GLU_G, CONV_CH))
    hist = CONV_WIDTH - 1
    acc = cb_ref[...] + u * cw_ref[hist:hist + 1, :]
    for j in range(hist):
        acc = acc + st_ref[j] * cw_ref[j:j + 1, :]
    conv_out = _silu(_layer_norm(acc, cg_ref[...], cbeta_ref[...]))
    for j in range(hist - 1):
        convs_ref[j] = st_ref[j + 1]
    convs_ref[hist - 1] = u
    o_attn = jnp.concatenate(
        [o_ref[pl.ds(HEADS_PER_GROUP + hh, DEC_BATCH, stride=KV_TILE_ROWS), :] for hh in range(HEADS_PER_GROUP)],
        axis=-1)
    a = jnp.dot((o_attn * _silu(z(COL_GA, GROUP_COLS))).astype(BF16), wpa_ref[...], preferred_element_type=F32)
    b = jnp.dot((conv_out * _silu(z(COL_GB, CONV_CH))).astype(BF16), wpb_ref[...], preferred_element_type=F32)
    pre = (_sigmoid(z(COL_MA, D_MODEL)) * a + _sigmoid(z(COL_MB, D_MODEL)) * b).astype(BF16)
    y = jnp.dot(pre, wo_ref[...], preferred_element_type=F32)
    y_ref[...] = _layer_norm(ALPHA * x_ref[...] + gate * y, lng_ref[...], lnb_ref[...])


def _sample_tail_call(x_s, mod_s, o_s, zr, state_t, conv_w, conv_b, cn_g, cn_b, w_pa, w_pb, w_o, ln_g, ln_b):
    full = lambda shape: pl.BlockSpec(shape, lambda i: (0,) * len(shape))
    args = (x_s, mod_s, o_s, zr, state_t, conv_w, conv_b, cn_g, cn_b, w_pa, w_pb, w_o, ln_g, ln_b)
    return pl.pallas_call(
        _sample_tail_kernel,
        grid=(1,),
        in_specs=[full(a.shape) for a in args],
        out_specs=[full((DEC_BATCH, D_MODEL)), full(state_t.shape)],
        out_shape=[jax.ShapeDtypeStruct((DEC_BATCH, D_MODEL), F32),
                   jax.ShapeDtypeStruct(state_t.shape, F32)],
        compiler_params=_params(("arbitrary",)),
        name="sample_tail",
    )(*args)


def kernel(x_prompt, x_sample, c_prompt, c_sample, cache_kv_w128, cache_kv_w512, cache_kv_w2048, state_conv,
           w_c, b_c, w_in, b_in, conv_w, conv_b, conv_norm_g, conv_norm_b, w_pa, w_pb, w_o, ln_g, ln_b):
    assert x_prompt.shape == (1, SEQ, D_MODEL) and x_sample.shape == (DEC_BATCH, 1, D_MODEL)
    assert w_in.shape == (D_MODEL, IN_COLS)
    caches = (cache_kv_w128, cache_kv_w512, cache_kv_w2048)
    for (window, _), c in zip(GROUPS, caches):
        assert c.shape == (DEC_BATCH, window, 2, HEADS_PER_GROUP, HEAD_DIM)

    w_pa_b, w_pb_b, w_o_b = w_pa.astype(BF16), w_pb.astype(BF16), w_o.astype(BF16)
    vec = lambda a: a.reshape(1, -1)
    b_row = vec(b_in)
    conv_args = (conv_w, vec(conv_b), vec(conv_norm_g), vec(conv_norm_b))
    out_args = (w_pa_b, w_pb_b, w_o_b, vec(ln_g), vec(ln_b))

    mod_p, mod_s = _mod_call(c_prompt, c_sample, w_c, b_c)
    x_s = x_sample.reshape(DEC_BATCH, D_MODEL)
    w_bf, z_s, qt0, qt1, qt2, kvs0, kvs1, kvs2 = _sample_proj_call(x_s, mod_s, w_in, b_row)

    x_p = x_prompt.reshape(SEQ, D_MODEL)
    qkv, kv0, kv1, kv2, b_conv, conv_p = _front_call(x_p, mod_p, w_bf, b_row, *conv_args)
    numerators, stats = _attn_call(qkv)
    y_p = _tail_call(x_p, mod_p, numerators, stats, b_conv, w_bf, b_row, *out_args)

    o_s = _sample_attn_call((qt0, qt1, qt2), (kvs0, kvs1, kvs2), caches)
    o_s = o_s.reshape(DEC_BATCH * KV_TILE_ROWS, HEAD_DIM)
    state_t = jnp.transpose(state_conv, (1, 0, 2))
    y_s, conv_s_t = _sample_tail_call(x_s, mod_s, o_s, z_s, state_t, *conv_args, *out_args)

    kv_shape_p = lambda keep: (1, keep, 2, HEADS_PER_GROUP, HEAD_DIM)
    kv_shape_s = (DEC_BATCH, 1, 2, HEADS_PER_GROUP, HEAD_DIM)
    return (y_p.reshape(1, SEQ, D_MODEL),
            y_s.reshape(DEC_BATCH, 1, D_MODEL),
            kv0.reshape(kv_shape_p(GROUPS[0][0])),
            kv1.reshape(kv_shape_p(GROUPS[1][0])),
            kv2.reshape(kv_shape_p(GROUPS[2][0])),
            conv_p.reshape(1, CONV_WIDTH - 1, CONV_CH),
            kvs0.reshape(kv_shape_s), kvs1.reshape(kv_shape_s), kvs2.reshape(kv_shape_s),
            jnp.transpose(conv_s_t, (1, 0, 2)))
```
